```python
import jax
import jax.numpy as jnp
from jax import lax
import numpy as np

D_MODEL = 2048
BATCH = 8
SEQ = 4096
DEPTH = 4

ROPE_THETA = 500000.0
Q_BLOCK = 128
NORM_EPS = 1e-6
MIXER_ORDER = ('dsa', 'fox', 'nsa')
N_MIXERS = len(MIXER_ORDER)

DSA_HEADS = 32
DSA_Q_RANK = 512
DSA_KV_RANK = 256
DSA_QK_DIM = 192
DSA_ROPE = DSA_QK_DIM // 4
DSA_NOPE = DSA_QK_DIM - DSA_ROPE
DSA_V_DIM = 128
IDX_HEADS = 16
IDX_DIM = 128
IDX_ROPE = IDX_DIM // 4
DSA_TOPK_MAX = 256
DSA_IN_SIZES = (DSA_Q_RANK, DSA_KV_RANK, DSA_ROPE, IDX_DIM, IDX_HEADS)
DSA_IN = sum(DSA_IN_SIZES)
DSA_QUP = DSA_HEADS * DSA_QK_DIM + IDX_HEADS * IDX_DIM

FOX_HEADS = 16
FOX_DIM = 128
FOX_IN = 3 * FOX_HEADS * FOX_DIM + FOX_HEADS

NSA_HEADS = 48
NSA_GROUPS = 4
NSA_HPG = NSA_HEADS // NSA_GROUPS
NSA_QK_DIM = 192
NSA_ROPE = NSA_QK_DIM // 4
NSA_V_DIM = 128
CMP_BLOCK = 32
CMP_STRIDE = 16
CMP_HIDDEN = 256
SLC_BLOCK = 64
SLC_TOPK = 16
WINDOW = 512
NSA_N_BRANCH = 3
NSA_KD = NSA_GROUPS * NSA_QK_DIM
NSA_VD = NSA_GROUPS * NSA_V_DIM
NSA_IN_SIZES = (NSA_HEADS * NSA_QK_DIM, NSA_KD, NSA_VD, NSA_KD, NSA_VD, NSA_KD, NSA_VD, NSA_N_BRANCH * NSA_HEADS)
NSA_IN = sum(NSA_IN_SIZES)

D_FF = 5632
CONV_WIDTH = 3

kernel_name = 'hybrid_dsa_fox_nsa_convffn_trunk'


def split_last(x, sizes):
    cuts = [int(c) for c in np.cumsum(sizes)[:-1]]
    return jnp.split(x, cuts, axis=-1)


def rms_norm(x, g):
    xf = x.astype(jnp.float32)
    y = xf * lax.rsqrt(jnp.mean(xf * xf, axis=-1, keepdims=True) + NORM_EPS)
    return (y * g.astype(jnp.float32)).astype(x.dtype)


def layer_norm(x, g, b):
    xf = x.astype(jnp.float32)
    mu = jnp.mean(xf, axis=-1, keepdims=True)
    var = jnp.mean(jnp.square(xf - mu), axis=-1, keepdims=True)
    y = (xf - mu) * lax.rsqrt(var + NORM_EPS)
    return (y * g.astype(jnp.float32) + b.astype(jnp.float32)).astype(x.dtype)


def rope_tables(positions, rot):
    inv = ROPE_THETA ** (-jnp.arange(0, rot, 2, dtype=jnp.float32) / rot)
    ang = positions.astype(jnp.float32)[..., None] * inv
    return jnp.cos(ang), jnp.sin(ang)


def apply_partial_rope(x, cos, sin):
    half = cos.shape[-1]
    rot = 2 * half
    xf = x[..., :rot].astype(jnp.float32)
    x1, x2 = xf[..., :half], xf[..., half:]
    c, s = cos[:, :, None, :], sin[:, :, None, :]
    r = jnp.concatenate([x1 * c - x2 * s, x1 * s + x2 * c], axis=-1).astype(x.dtype)
    return jnp.concatenate([r, x[..., rot:]], axis=-1)


def masked_softmax(logits, mask):
    z = jnp.where(mask, logits.astype(jnp.float32), -jnp.inf)
    m = jnp.max(z, axis=-1, keepdims=True)
    m = jnp.where(jnp.isfinite(m), m, 0.0)
    e = jnp.where(mask, jnp.exp(z - m), 0.0)
    den = jnp.sum(e, axis=-1, keepdims=True)
    return e / jnp.where(den > 0, den, 1.0)


def sweep_query_blocks(block_fn, seq):
    out = lax.map(block_fn, jnp.arange(seq // Q_BLOCK))
    out = jnp.moveaxis(out, 0, 1)
    return out.reshape((out.shape[0], seq) + out.shape[3:])


def dsa_mixer(h, positions, w_in, q_norm, kv_norm, idx_ln_g, idx_ln_b, w_qup, w_uk, w_uv, w_out):
    B, S, _ = h.shape
    n_keep = min(DSA_TOPK_MAX, S // 4)
    scale = DSA_QK_DIM ** -0.5
    cq, ckv, k_pe, k_idx, w_idx = split_last(h @ w_in, DSA_IN_SIZES)
    q, q_idx = split_last(rms_norm(cq, q_norm) @ w_qup, (DSA_HEADS * DSA_QK_DIM, IDX_HEADS * IDX_DIM))
    cos, sin = rope_tables(positions, DSA_ROPE)
    q = apply_partial_rope(q.reshape(B, S, DSA_HEADS, DSA_QK_DIM), cos, sin)
    q_pe, q_nope = q[..., :DSA_ROPE], q[..., DSA_ROPE:]
    k_pe = apply_partial_rope(k_pe[:, :, None, :], cos, sin)[:, :, 0]
    latent = jnp.concatenate([rms_norm(ckv, kv_norm), k_pe], axis=-1)
    icos, isin = rope_tables(positions, IDX_ROPE)
    q_idx = apply_partial_rope(q_idx.reshape(B, S, IDX_HEADS, IDX_DIM), icos, isin)
    k_idx = apply_partial_rope(layer_norm(k_idx, idx_ln_g, idx_ln_b)[:, :, None, :], icos, isin)[:, :, 0]
    w_idx = w_idx * (IDX_HEADS ** -0.5 * IDX_DIM ** -0.5)
    key_pos = jnp.arange(S)

    def block(i):
        q0 = i * Q_BLOCK
        t = q0 + jnp.arange(Q_BLOCK)
        causal = key_pos[None, :] <= t[:, None]
        qi = lax.dynamic_slice_in_dim(q_idx, q0, Q_BLOCK, 1)
        wi = lax.dynamic_slice_in_dim(w_idx, q0, Q_BLOCK, 1)
        dots = jnp.einsum('bqhd,bsd->bqhs', qi, k_idx)
        score = jnp.einsum('bqhs,bqh->bqs', jax.nn.relu(dots).astype(jnp.float32), wi.astype(jnp.float32))
        score = jnp.where(causal[None], score, -jnp.inf)
        _, sel = lax.top_k(score, n_keep)
        valid = sel <= t[None, :, None]
        kv = jax.vmap(lambda a, ix: a[ix])(latent, sel)
        c_sel, kpe_sel = kv[..., :DSA_KV_RANK], kv[..., DSA_KV_RANK:]
        qn = lax.dynamic_slice_in_dim(q_nope, q0, Q_BLOCK, 1)
        qp = lax.dynamic_slice_in_dim(q_pe, q0, Q_BLOCK, 1)
        q_lat = jnp.einsum('bqhn,hrn->bqhr', qn, w_uk)
        logits = (jnp.einsum('bqhr,bqkr->bqhk', q_lat, c_sel)
                  + jnp.einsum('bqhe,bqke->bqhk', qp, kpe_sel)) * scale
        p = masked_softmax(logits, valid[:, :, None, :]).astype(c_sel.dtype)
        o_lat = jnp.einsum('bqhk,bqkr->bqhr', p, c_sel)
        return jnp.einsum('bqhr,hrv->bqhv', o_lat, w_uv)

    o = sweep_query_blocks(block, S)
    return o.reshape(B, S, DSA_HEADS * DSA_V_DIM) @ w_out


def fox_mixer(h, w_in, b_f, w_out):
    B, S, _ = h.shape
    hd = FOX_HEADS * FOX_DIM
    q, k, v, f_logit = split_last(h @ w_in, (hd, hd, hd, FOX_HEADS))
    q = q.reshape(B, S, FOX_HEADS, FOX_DIM)
    k = k.reshape(B, S, FOX_HEADS, FOX_DIM)
    v = v.reshape(B, S, FOX_HEADS, FOX_DIM)
    log_f = jax.nn.log_sigmoid(f_logit.astype(jnp.float32) + b_f.astype(jnp.float32))
    cum = jnp.cumsum(log_f, axis=1).transpose(0, 2, 1)
    scale = FOX_DIM ** -0.5
    key_pos = jnp.arange(S)

    def block(i):
        q0 = i * Q_BLOCK
        t = q0 + jnp.arange(Q_BLOCK)
        qb = lax.dynamic_slice_in_dim(q, q0, Q_BLOCK, 1)
        cb = lax.dynamic_slice_in_dim(cum, q0, Q_BLOCK, 2)
        logits = (jnp.einsum('bqhd,bshd->bhqs', qb, k).astype(jnp.float32) * scale
                  + (cb[..., None] - cum[:, :, None, :]))
        p = masked_softmax(logits, key_pos[None, :] <= t[:, None]).astype(v.dtype)
        return jnp.einsum('bhqs,bshd->bqhd', p, v)

    o = sweep_query_blocks(block, S)
    return o.reshape(B, S, hd) @ w_out


def compress_blocks(x, pe, w1, b1, w2):
    B, S, G, D = x.shape
    n_cmp = (S - CMP_BLOCK) // CMP_STRIDE + 1
    idx = np.arange(n_cmp)[:, None] * CMP_STRIDE + np.arange(CMP_BLOCK)[None, :]
    blocks = x[:, idx] + pe[None, None, :, None, :]
    blocks = blocks.transpose(0, 1, 3, 2, 4).reshape(B, n_cmp, G, CMP_BLOCK * D)
    return jax.nn.gelu(blocks @ w1 + b1) @ w2


def nsa_mixer(h, positions, w_in, k_pe, k_w1, k_b1, k_w2, v_pe, v_w1, v_b1, v_w2, w_out):
    B, S, _ = h.shape
    G, J, DK, DV = NSA_GROUPS, NSA_HPG, NSA_QK_DIM, NSA_V_DIM
    q, k_c, v_c, k_s, v_s, k_w, v_w, g_logit = split_last(h @ w_in, NSA_IN_SIZES)
    cos, sin = rope_tables(positions, NSA_ROPE)
    q = apply_partial_rope(q.reshape(B, S, NSA_HEADS, DK), cos, sin).reshape(B, S, G, J, DK)
    k_c = apply_partial_rope(k_c.reshape(B, S, G, DK), cos, sin)
    k_s = apply_partial_rope(k_s.reshape(B, S, G, DK), cos, sin)
    k_w = apply_partial_rope(k_w.reshape(B, S, G, DK), cos, sin)
    v_c = v_c.reshape(B, S, G, DV)
    v_s = v_s.reshape(B, S, G, DV)
    v_w = v_w.reshape(B, S, G, DV)
    gates = jax.nn.sigmoid(g_logit.astype(jnp.float32)).astype(h.dtype).reshape(B, S, NSA_N_BRANCH, G, J)
    kc = compress_blocks(k_c, k_pe, k_w1, k_b1, k_w2)
    vc = compress_blocks(v_c, v_pe, v_w1, v_b1, v_w2)
    n_cmp = kc.shape[1]
    cmp_start = np.arange(n_cmp) * CMP_STRIDE
    cmp_last = jnp.asarray(cmp_start + CMP_BLOCK - 1)
    n_slc = S // SLC_BLOCK
    n_sel = min(SLC_TOPK, n_slc)
    slc_start = np.arange(n_slc) * SLC_BLOCK
    overlap = jnp.asarray(((cmp_start[:, None] < slc_start[None, :] + SLC_BLOCK)
                           & (cmp_start[:, None] + CMP_BLOCK > slc_start[None, :])).astype(np.float32))
    ks_blocks = k_s.reshape(B, n_slc, SLC_BLOCK, G, DK).transpose(0, 3, 1, 2, 4)
    vs_blocks = v_s.reshape(B, n_slc, SLC_BLOCK, G, DV).transpose(0, 3, 1, 2, 4)
    blk_ids = jnp.arange(n_slc)
    in_blk = jnp.arange(SLC_BLOCK)
    gather = jax.vmap(jax.vmap(lambda a, ix: a[ix]))
    pad = ((0, 0), (WINDOW, 0), (0, 0), (0, 0))
    k_wp = jnp.pad(k_w, pad)
    v_wp = jnp.pad(v_w, pad)
    win_len = WINDOW + Q_BLOCK
    scale = DK ** -0.5

    def block(i):
        q0 = i * Q_BLOCK
        t = q0 + jnp.arange(Q_BLOCK)
        qb = lax.dynamic_slice_in_dim(q, q0, Q_BLOCK, 1)
        lc = jnp.einsum('bqgjd,bngd->bgjqn', qb, kc) * scale
        pc = masked_softmax(lc, cmp_last[None, :] <= t[:, None])
        o_c = jnp.einsum('bgjqn,bngv->bqgjv', pc.astype(vc.dtype), vc)
        imp = jnp.einsum('bgjqn,nm->bgqm', pc, overlap)
        cur = t // SLC_BLOCK
        forced = ((blk_ids[None, :] == 0) | (blk_ids[None, :] == cur[:, None])
                  | (blk_ids[None, :] == cur[:, None] - 1))
        causal_blk = blk_ids[None, :] * SLC_BLOCK <= t[:, None]
        imp = jnp.where(causal_blk, jnp.where(forced, jnp.inf, imp), -jnp.inf)
        _, sel = lax.top_k(imp, n_sel)
        ks = gather(ks_blocks, sel).reshape(B, G, Q_BLOCK, n_sel * SLC_BLOCK, DK)
        vs = gather(vs_blocks, sel).reshape(B, G, Q_BLOCK, n_sel * SLC_BLOCK, DV)
        pos = (sel[..., None] * SLC_BLOCK + in_blk).reshape(B, G, Q_BLOCK, n_sel * SLC_BLOCK)
        ls = jnp.einsum('bqgjd,bgqkd->bgjqk', qb, ks) * scale
        ps = masked_softmax(ls, (pos <= t[None, None, :, None])[:, :, None]).astype(vs.dtype)
        o_s = jnp.einsum('bgjqk,bgqkv->bqgjv', ps, vs)
        kw = lax.dynamic_slice_in_dim(k_wp, q0, win_len, 1)
        vw = lax.dynamic_slice_in_dim(v_wp, q0, win_len, 1)
        s_pos = q0 - WINDOW + jnp.arange(win_len)
        wmask = ((s_pos[None, :] >= 0) & (s_pos[None, :] <= t[:, None])
                 & (s_pos[None, :] > t[:, None] - WINDOW))
        lw = jnp.einsum('bqgjd,bsgd->bgjqs', qb, kw) * scale
        pw = masked_softmax(lw, wmask).astype(vw.dtype)
        o_w = jnp.einsum('bgjqs,bsgv->bqgjv', pw, vw)
        gb = lax.dynamic_slice_in_dim(gates, q0, Q_BLOCK, 1)
        return gb[:, :, 0, ..., None] * o_c + gb[:, :, 1, ..., None] * o_s + gb[:, :, 2, ..., None] * o_w

    o = sweep_query_blocks(block, S)
    return o.reshape(B, S, NSA_HEADS * DV) @ w_out


def conv_glu_ffn(h, w_up, conv_w, conv_b, w_down):
    S = h.shape[1]
    u = h @ w_up
    up = jnp.pad(u, ((0, 0), (CONV_WIDTH - 1, 0), (0, 0)))
    y = conv_b
    for j in range(CONV_WIDTH):
        y = y + conv_w[j] * up[:, j:j + S]
    gate, val = jnp.split(y, 2, axis=-1)
    return (jax.nn.silu(gate) * val) @ w_down


def _normal(key, shape, scale):
    return jax.random.normal(key, shape, jnp.float32) * scale


def _gain(key, n):
    return 1.0 + 0.05 * jax.random.normal(key, (n,), jnp.float32)


def _dsa_params(key, p):
    ks = jax.random.split(key, 9)
    return {
        p + 'dsa_w_in': _normal(ks[0], (D_MODEL, DSA_IN), D_MODEL ** -0.5),
        p + 'dsa_q_norm': _gain(ks[1], DSA_Q_RANK),
        p + 'dsa_kv_norm': _gain(ks[2], DSA_KV_RANK),
        p + 'dsa_idx_ln_g': _gain(ks[3], IDX_DIM),
        p + 'dsa_idx_ln_b': _normal(ks[4], (IDX_DIM,), 0.02),
        p + 'dsa_w_qup': _normal(ks[5], (DSA_Q_RANK, DSA_QUP), DSA_Q_RANK ** -0.5),
        p + 'dsa_w_uk': _normal(ks[6], (DSA_HEADS, DSA_KV_RANK, DSA_NOPE), DSA_KV_RANK ** -0.5),
        p + 'dsa_w_uv': _normal(ks[7], (DSA_HEADS, DSA_KV_RANK, DSA_V_DIM), DSA_KV_RANK ** -0.5),
        p + 'dsa_w_out': _normal(ks[8], (DSA_HEADS * DSA_V_DIM, D_MODEL), (DSA_HEADS * DSA_V_DIM) ** -0.5),
    }


def _fox_params(key, p):
    ks = jax.random.split(key, 3)
    return {
        p + 'fox_w_in': _normal(ks[0], (D_MODEL, FOX_IN), D_MODEL ** -0.5),
        p + 'fox_b_f': jax.random.uniform(ks[1], (FOX_HEADS,), jnp.float32, 1.0, 5.0),
        p + 'fox_w_out': _normal(ks[2], (FOX_HEADS * FOX_DIM, D_MODEL), (FOX_HEADS * FOX_DIM) ** -0.5),
    }


def _nsa_params(key, p):
    ks = jax.random.split(key, 10)
    return {
        p + 'nsa_w_in': _normal(ks[0], (D_MODEL, NSA_IN), D_MODEL ** -0.5),
        p + 'nsa_k_pe': _normal(ks[1], (CMP_BLOCK, NSA_QK_DIM), 0.1),
        p + 'nsa_k_w1': _normal(ks[2], (CMP_BLOCK * NSA_QK_DIM, CMP_HIDDEN), (CMP_BLOCK * NSA_QK_DIM) ** -0.5),
        p + 'nsa_k_b1': _normal(ks[3], (CMP_HIDDEN,), 0.01),
        p + 'nsa_k_w2': _normal(ks[4], (CMP_HIDDEN, NSA_QK_DIM), CMP_HIDDEN ** -0.5),
        p + 'nsa_v_pe': _normal(ks[5], (CMP_BLOCK, NSA_V_DIM), 0.1),
        p + 'nsa_v_w1': _normal(ks[6], (CMP_BLOCK * NSA_V_DIM, CMP_HIDDEN), (CMP_BLOCK * NSA_V_DIM) ** -0.5),
        p + 'nsa_v_b1': _normal(ks[7], (CMP_HIDDEN,), 0.01),
        p + 'nsa_v_w2': _normal(ks[8], (CMP_HIDDEN, NSA_V_DIM), CMP_HIDDEN ** -0.5),
        p + 'nsa_w_out': _normal(ks[9], (NSA_HEADS * NSA_V_DIM, D_MODEL), (NSA_HEADS * NSA_V_DIM) ** -0.5),
    }


def _ffn_params(key, p):
    ks = jax.random.split(key, 5)
    return {
        p + 'ffn_norm': _gain(ks[0], D_MODEL),
        p + 'ffn_up': _normal(ks[1], (D_MODEL, 2 * D_FF), D_MODEL ** -0.5),
        p + 'ffn_conv_w': _normal(ks[2], (CONV_WIDTH, 2 * D_FF), CONV_WIDTH ** -0.5),
        p + 'ffn_conv_b': _normal(ks[3], (2 * D_FF,), 0.01),
        p + 'ffn_down': _normal(ks[4], (D_FF, D_MODEL), D_FF ** -0.5),
    }


def setup_inputs(seed: int = 0) -> dict:
    key = jax.random.key(seed)
    keys = jax.random.split(key, DEPTH + 3)
    mixer_init = {'dsa': _dsa_params, 'fox': _fox_params, 'nsa': _nsa_params}
    params = {}
    params['x'] = jax.random.normal(keys[0], (BATCH, SEQ, D_MODEL), jnp.float32)
    offset = jax.random.randint(keys[1], (BATCH, 1), 0, 1024, dtype=jnp.int32)
    params['positions'] = jnp.arange(SEQ, dtype=jnp.int32)[None, :] + offset
    for i in range(DEPTH):
        k_norm, k_mix, k_ffn = jax.random.split(keys[2 + i], 3)
        p = 'l%d_' % i
        params[p + 'attn_norm'] = _gain(k_norm, D_MODEL)
        params.update(mixer_init[MIXER_ORDER[i % N_MIXERS]](k_mix, p))
        params.update(_ffn_params(k_ffn, p))
    params['final_norm'] = _gain(keys[DEPTH + 2], D_MODEL)
    return params


def reference(x, positions,
              l0_attn_norm, l0_dsa_w_in, l0_dsa_q_norm, l0_dsa_kv_norm, l0_dsa_idx_ln_g, l0_dsa_idx_ln_b,
              l0_dsa_w_qup, l0_dsa_w_uk, l0_dsa_w_uv, l0_dsa_w_out,
              l0_ffn_norm, l0_ffn_up, l0_ffn_conv_w, l0_ffn_conv_b, l0_ffn_down,
              l1_attn_norm, l1_fox_w_in, l1_fox_b_f, l1_fox_w_out,
              l1_ffn_norm, l1_ffn_up, l1_ffn_conv_w, l1_ffn_conv_b, l1_ffn_down,
              l2_attn_norm, l2_nsa_w_in, l2_nsa_k_pe, l2_nsa_k_w1, l2_nsa_k_b1, l2_nsa_k_w2,
              l2_nsa_v_pe, l2_nsa_v_w1, l2_nsa_v_b1, l2_nsa_v_w2, l2_nsa_w_out,
              l2_ffn_norm, l2_ffn_up, l2_ffn_conv_w, l2_ffn_conv_b, l2_ffn_down,
              l3_attn_norm, l3_dsa_w_in, l3_dsa_q_norm, l3_dsa_kv_norm, l3_dsa_idx_ln_g, l3_dsa_idx_ln_b,
              l3_dsa_w_qup, l3_dsa_w_uk, l3_dsa_w_uv, l3_dsa_w_out,
              l3_ffn_norm, l3_ffn_up, l3_ffn_conv_w, l3_ffn_conv_b, l3_ffn_down,
              final_norm):
    mixer_fns = {'dsa': dsa_mixer, 'fox': fox_mixer, 'nsa': nsa_mixer}
    layer_args = (
        (l0_attn_norm,
         (positions, l0_dsa_w_in, l0_dsa_q_norm, l0_dsa_kv_norm, l0_dsa_idx_ln_g, l0_dsa_idx_ln_b,
          l0_dsa_w_qup, l0_dsa_w_uk, l0_dsa_w_uv, l0_dsa_w_out),
         l0_ffn_norm, (l0_ffn_up, l0_ffn_conv_w, l0_ffn_conv_b, l0_ffn_down)),
        (l1_attn_norm,
         (l1_fox_w_in, l1_fox_b_f, l1_fox_w_out),
         l1_ffn_norm, (l1_ffn_up, l1_ffn_conv_w, l1_ffn_conv_b, l1_ffn_down)),
        (l2_attn_norm,
         (positions, l2_nsa_w_in, l2_nsa_k_pe, l2_nsa_k_w1, l2_nsa_k_b1, l2_nsa_k_w2,
          l2_nsa_v_pe, l2_nsa_v_w1, l2_nsa_v_b1, l2_nsa_v_w2, l2_nsa_w_out),
         l2_ffn_norm, (l2_ffn_up, l2_ffn_conv_w, l2_ffn_conv_b, l2_ffn_down)),
        (l3_attn_norm,
         (positions, l3_dsa_w_in, l3_dsa_q_norm, l3_dsa_kv_norm, l3_dsa_idx_ln_g, l3_dsa_idx_ln_b,
          l3_dsa_w_qup, l3_dsa_w_uk, l3_dsa_w_uv, l3_dsa_w_out),
         l3_ffn_norm, (l3_ffn_up, l3_ffn_conv_w, l3_ffn_conv_b, l3_ffn_down)),
    )
    h = x
    for i in range(DEPTH):
        attn_norm, mix_args, ffn_norm, ffn_args = layer_args[i]
        mixer = mixer_fns[MIXER_ORDER[i % N_MIXERS]]
        h = h + mixer(rms_norm(h, attn_norm), *mix_args)
        h = h + conv_glu_ffn(rms_norm(h, ffn_norm), *ffn_args)
    return rms_norm(h, final_norm)
```

```python
import functools

import jax
import jax.numpy as jnp
import numpy as np
from jax import lax
from jax.experimental import pallas as pl
from jax.experimental.pallas import tpu as pltpu

F32 = jnp.float32
BF16 = jnp.bfloat16

ROPE_THETA = 500000.0
NORM_EPS = 1e-6
Q_BLOCK = 128

DSA_HEADS = 32
DSA_Q_RANK = 512
DSA_KV_RANK = 256
DSA_QK_DIM = 192
DSA_ROPE = 48
DSA_NOPE = DSA_QK_DIM - DSA_ROPE
DSA_V_DIM = 128
IDX_HEADS = 16
IDX_DIM = 128
IDX_ROPE = 32
DSA_TOPK_MAX = 256
DSA_IN = DSA_Q_RANK + DSA_KV_RANK + DSA_ROPE + IDX_DIM + IDX_HEADS

FOX_HEADS = 16
FOX_DIM = 128

NSA_HEADS = 48
NSA_GROUPS = 4
NSA_HPG = NSA_HEADS // NSA_GROUPS
NSA_QK_DIM = 192
NSA_ROPE = 48
NSA_V_DIM = 128
CMP_BLOCK = 32
CMP_STRIDE = 16
CMP_HIDDEN = 256
SLC_BLOCK = 64
SLC_TOPK = 16
WINDOW = 512
NSA_KD = NSA_GROUPS * NSA_QK_DIM
NSA_VD = NSA_GROUPS * NSA_V_DIM

CONV_WIDTH = 3

LANES = 128
BF16_SUBLANES = 16
VMEM_LIMIT = 56 * 1024 * 1024

MASK_NEG = -1e30
BISECT_ITERS = 32


def _params(*sem):
    return pltpu.CompilerParams(dimension_semantics=sem, vmem_limit_bytes=VMEM_LIMIT)


def _nt(a, b):
    return lax.dot_general(a, b, (((1,), (1,)), ((), ())), preferred_element_type=F32)


def _rms(x, g):
    return x * lax.rsqrt(jnp.mean(x * x, axis=-1, keepdims=True) + NORM_EPS) * g


def _rope_first_vreg(x0, cf, sa, sb, half):
    return (x0 * cf + pltpu.roll(x0, LANES - half, 1) * sa + pltpu.roll(x0, half, 1) * sb)


def _mm_body(*refs, has_gain, has_res):
    x_ref, w_ref = refs[0], refs[1]
    k = 2
    g_ref = r_ref = None
    if has_gain:
        g_ref = refs[k]
        k += 1
    if has_res:
        r_ref = refs[k]
        k += 1
    o_ref = refs[k]
    if has_gain:
        xn_ref = refs[k + 1]

        @pl.when(pl.program_id(1) == 0)
        def _():
            xn_ref[...] = _rms(x_ref[...].astype(F32), g_ref[...]).astype(BF16)

        a = xn_ref[...]
    else:
        a = x_ref[...].astype(BF16)
    acc = jnp.dot(a, w_ref[...], preferred_element_type=F32)
    if has_res:
        acc = acc + r_ref[...]
    o_ref[...] = acc.astype(o_ref.dtype)


def _mm(x, w, *, gain=None, res=None, out_dtype=F32, x_col_block=0, tm=512, tn=512):
    m = x.shape[0]
    k, n = w.shape
    tm = min(tm, m)
    tn = min(tn, n)
    assert m % tm == 0 and n % tn == 0 and x.shape[1] % k == 0
    in_specs = [pl.BlockSpec((tm, k), lambda i, j: (i, x_col_block)),
                pl.BlockSpec((k, tn), lambda i, j: (0, j))]
    args = [x, w]
    scratch = []
    if gain is not None:
        in_specs.append(pl.BlockSpec((1, k), lambda i, j: (0, 0)))
        args.append(gain.reshape(1, k).astype(F32))
        scratch.append(pltpu.VMEM((tm, k), BF16))
    if res is not None:
        in_specs.append(pl.BlockSpec((tm, tn), lambda i, j: (i, j)))
        args.append(res)
    return pl.pallas_call(
        functools.partial(_mm_body, has_gain=gain is not None, has_res=res is not None),
        grid=(m // tm, n // tn),
        in_specs=in_specs,
        out_specs=pl.BlockSpec((tm, tn), lambda i, j: (i, j)),
        out_shape=jax.ShapeDtypeStruct((m, n), out_dtype),
        scratch_shapes=scratch,
        compiler_params=_params("arbitrary", "arbitrary"),
        name="mm",
    )(*args)


FFN_HALO = BF16_SUBLANES


def _ffn_body(h_ref, hp_ref, g_ref, wg_ref, wv_ref, cwg_ref, cwv_ref, cbg_ref, cbv_ref, wd_ref,
              o_ref, xn_ref, ug_ref, uv_ref, acc_ref, *, tm, seq):
    i = pl.program_id(0)
    j = pl.program_id(1)

    @pl.when(j == 0)
    def _():
        xn_ref[FFN_HALO:, :] = _rms(h_ref[...], g_ref[...]).astype(BF16)
        prev = _rms(hp_ref[...], g_ref[...])
        seq_start = (i * tm) % seq == 0
        xn_ref[:FFN_HALO, :] = jnp.where(seq_start, 0.0, prev).astype(BF16)
        acc_ref[...] = jnp.zeros_like(acc_ref)

    xn = xn_ref[...]
    ug_ref[...] = jnp.dot(xn, wg_ref[...], preferred_element_type=F32)
    uv_ref[...] = jnp.dot(xn, wv_ref[...], preferred_element_type=F32)

    def conv(u_ref, cw_ref, cb_ref):
        y = cb_ref[...]
        for t in range(CONV_WIDTH):
            off = FFN_HALO - (CONV_WIDTH - 1) + t
            y = y + cw_ref[t:t + 1, :] * u_ref[off:off + tm, :]
        return y

    yg = conv(ug_ref, cwg_ref, cbg_ref)
    yv = conv(uv_ref, cwv_ref, cbv_ref)
    a = (jax.nn.silu(yg) * yv).astype(BF16)
    acc_ref[...] += jnp.dot(a, wd_ref[...], preferred_element_type=F32)

    @pl.when(j == pl.num_programs(1) - 1)
    def _():
        o_ref[...] = h_ref[...] + acc_ref[...]


def _ffn(h, seq, gain, w_up, conv_w, conv_b, w_down, *, tm=512, tf=512):
    t, d = h.shape
    dff = w_down.shape[0]
    tm = min(tm, seq)
    assert seq % tm == 0 and dff % tf == 0 and tm % FFN_HALO == 0
    nf = dff // tf
    hb = tm // FFN_HALO
    conv_b2 = conv_b.reshape(1, 2 * dff)
    return pl.pallas_call(
        functools.partial(_ffn_body, tm=tm, seq=seq),
        grid=(t // tm, nf),
        in_specs=[
            pl.BlockSpec((tm, d), lambda i, j: (i, 0)),
            pl.BlockSpec((FFN_HALO, d), lambda i, j: (jnp.maximum(i * hb - 1, 0), 0)),
            pl.BlockSpec((1, d), lambda i, j: (0, 0)),
            pl.BlockSpec((d, tf), lambda i, j: (0, j)),
            pl.BlockSpec((d, tf), lambda i, j: (0, nf + j)),
            pl.BlockSpec((CONV_WIDTH, tf), lambda i, j: (0, j)),
            pl.BlockSpec((CONV_WIDTH, tf), lambda i, j: (0, nf + j)),
            pl.BlockSpec((1, tf), lambda i, j: (0, j)),
            pl.BlockSpec((1, tf), lambda i, j: (0, nf + j)),
            pl.BlockSpec((tf, d), lambda i, j: (j, 0)),
        ],
        out_specs=pl.BlockSpec((tm, d), lambda i, j: (i, 0)),
        out_shape=jax.ShapeDtypeStruct((t, d), F32),
        scratch_shapes=[
            pltpu.VMEM((tm + FFN_HALO, d), BF16),
            pltpu.VMEM((tm + FFN_HALO, tf), F32),
            pltpu.VMEM((tm + FFN_HALO, tf), F32),
            pltpu.VMEM((tm, d), F32),
        ],
        compiler_params=_params("arbitrary", "arbitrary"),
        name="ffn",
    )(h, h, gain.reshape(1, d), w_up, w_up, conv_w, conv_w, conv_b2, conv_b2, w_down)


def _rope_cos_sin(positions, rot):
    inv = ROPE_THETA ** (-jnp.arange(0, rot, 2, dtype=F32) / rot)
    ang = positions.astype(F32)[..., None] * inv
    return jnp.cos(ang), jnp.sin(ang)


def _rope_lane_tables(positions, rot):
    c, s = _rope_cos_sin(positions, rot)
    half = rot // 2
    shp = c.shape[:-1]
    cf = jnp.concatenate([c, c, jnp.ones(shp + (LANES - rot,), F32)], -1)
    sa = jnp.concatenate([-s, jnp.zeros(shp + (LANES - half,), F32)], -1)
    sb = jnp.concatenate([jnp.zeros(shp + (half,), F32), s, jnp.zeros(shp + (LANES - rot,), F32)], -1)
    return jnp.concatenate([cf, sa, sb], -1)


def _rope_glue(x, cos, sin):
    half = cos.shape[-1]
    x1, x2 = x[..., :half], x[..., half:2 * half]
    return jnp.concatenate([x1 * cos - x2 * sin, x1 * sin + x2 * cos, x[..., 2 * half:]], -1)


DSA_QX = 384
DSA_HEAD_GROUP = 8


def _dsa_body(q_ref, qi_ref, wi_ref, tab_ref, kidx_ref, kext_ref, wk_ref, wuv_ref, o_ref,
              qx_ref, qis_ref, wb_ref, sc_ref, m_ref, l_ref, acc_ref, *, kc, n_keep):
    i = pl.program_id(1)
    q0 = i * Q_BLOCK
    nc = (q0 + Q_BLOCK + kc - 1) // kc
    scale = DSA_QK_DIM ** -0.5
    tab = tab_ref[0]
    cf, sa, sb = tab[:, 0:LANES], tab[:, LANES:2 * LANES], tab[:, 2 * LANES:3 * LANES]
    cfi, sai, sbi = tab[:, 3 * LANES:4 * LANES], tab[:, 4 * LANES:5 * LANES], tab[:, 5 * LANES:6 * LANES]

    for h in range(DSA_HEADS):
        xh = q_ref[0, :, h * DSA_QK_DIM:(h + 1) * DSA_QK_DIM].astype(F32)
        r0 = _rope_first_vreg(xh[:, :LANES], cf, sa, sb, DSA_ROPE // 2)
        qh = jnp.concatenate([r0, xh[:, LANES:]], axis=1).astype(BF16)
        qx = jnp.dot(qh, wk_ref[h], preferred_element_type=F32) * scale
        qx_ref[h * Q_BLOCK:(h + 1) * Q_BLOCK, :] = qx.astype(BF16)

    wi = wi_ref[0]
    for h in range(IDX_HEADS):
        xi = qi_ref[0, :, h * IDX_DIM:(h + 1) * IDX_DIM].astype(F32)
        qis_ref[h * Q_BLOCK:(h + 1) * Q_BLOCK, :] = _rope_first_vreg(xi, cfi, sai, sbi, IDX_ROPE // 2).astype(BF16)
        wb_ref[h * Q_BLOCK:(h + 1) * Q_BLOCK, :] = jnp.broadcast_to(wi[:, h:h + 1], (Q_BLOCK, LANES))

    t_col = q0 + lax.broadcasted_iota(jnp.int32, (Q_BLOCK, kc), 0)
    lane = lax.broadcasted_iota(jnp.int32, (Q_BLOCK, kc), 1)

    def score_chunk(c, carry):
        lo, hi = carry
        k0 = pl.multiple_of(c * kc, kc)
        d = _nt(qis_ref[...], kidx_ref[0, pl.ds(k0, kc), :])
        d = jnp.maximum(d, 0.0) * jnp.tile(wb_ref[...], (1, kc // LANES))
        sc = d.reshape(IDX_HEADS, Q_BLOCK, kc).sum(axis=0) + 0.0
        causal = (k0 + lane) <= t_col
        sc_ref[c] = jnp.where(causal, sc, -jnp.inf)
        lo = jnp.minimum(lo, jnp.where(causal, sc, jnp.inf).min(axis=1, keepdims=True))
        hi = jnp.maximum(hi, jnp.where(causal, sc, -jnp.inf).max(axis=1, keepdims=True))
        return lo, hi

    lo, hi = lax.fori_loop(0, nc, score_chunk,
                           (jnp.full((Q_BLOCK, 1), jnp.inf, F32), jnp.full((Q_BLOCK, 1), -jnp.inf, F32)))

    def bisect(_, carry):
        lo, hi = carry
        mid = lo + (hi - lo) * 0.5

        def count_chunk(c, cnt):
            return cnt + jnp.where(sc_ref[c] >= mid, 1.0, 0.0).sum(axis=1, keepdims=True)

        cnt = lax.fori_loop(0, nc, count_chunk, jnp.zeros((Q_BLOCK, 1), F32))
        ge = cnt >= n_keep
        return jnp.where(ge, mid, lo), jnp.where(ge, hi, mid)

    lo, hi = lax.fori_loop(0, BISECT_ITERS, bisect, (lo, hi))

    def bias_chunk(c, _):
        sc_ref[c] = jnp.where(sc_ref[c] >= lo, 0.0, MASK_NEG)
        return 0

    lax.fori_loop(0, nc, bias_chunk, 0)

    m_ref[...] = jnp.full_like(m_ref, MASK_NEG)
    l_ref[...] = jnp.zeros_like(l_ref)
    acc_ref[...] = jnp.zeros_like(acc_ref)
    rows = DSA_HEAD_GROUP * Q_BLOCK

    def attn_chunk(c, _):
        k0 = pl.multiple_of(c * kc, kc)
        kx = kext_ref[0, pl.ds(k0, kc), :]
        bias = sc_ref[c]
        for g in range(DSA_HEADS // DSA_HEAD_GROUP):
            r = slice(g * rows, (g + 1) * rows)
            s = _nt(qx_ref[r, :], kx)
            s = (s.reshape(DSA_HEAD_GROUP, Q_BLOCK, kc) + bias[None]).reshape(rows, kc)
            m_prev = m_ref[r, :]
            m_new = jnp.maximum(m_prev, s.max(axis=1, keepdims=True))
            alpha = jnp.exp(m_prev - m_new)
            p = jnp.exp(s - m_new)
            l_ref[r, :] = alpha * l_ref[r, :] + p.sum(axis=1, keepdims=True)
            acc_ref[r, :] = alpha * acc_ref[r, :] + jnp.dot(
                p.astype(BF16), kx[:, :DSA_KV_RANK], preferred_element_type=F32)
            m_ref[r, :] = m_new
        return 0

    lax.fori_loop(0, nc, attn_chunk, 0)

    for h in range(DSA_HEADS):
        r = slice(h * Q_BLOCK, (h + 1) * Q_BLOCK)
        o_lat = (acc_ref[r, :] / l_ref[r, :]).astype(BF16)
        o_ref[0, :, h * DSA_V_DIM:(h + 1) * DSA_V_DIM] = jnp.dot(
            o_lat, wuv_ref[h], preferred_element_type=F32).astype(o_ref.dtype)


def _dsa_attention(qall, wi, tabs, kidx, kext, wk, wuv, *, kc=512):
    b, s, _ = qall.shape
    kc = min(kc, s)
    assert s % kc == 0 and kc % Q_BLOCK == 0
    n_keep = min(DSA_TOPK_MAX, s // 4)
    nq = DSA_HEADS * DSA_QK_DIM
    ni = IDX_HEADS * IDX_DIM
    assert nq % ni == 0
    return pl.pallas_call(
        functools.partial(_dsa_body, kc=kc, n_keep=float(n_keep)),
        grid=(b, s // Q_BLOCK),
        in_specs=[
            pl.BlockSpec((1, Q_BLOCK, nq), lambda bb, i: (bb, i, 0)),
            pl.BlockSpec((1, Q_BLOCK, ni), lambda bb, i: (bb, i, nq // ni)),
            pl.BlockSpec((1, Q_BLOCK, IDX_HEADS), lambda bb, i: (bb, i, 0)),
            pl.BlockSpec((1, Q_BLOCK, 6 * LANES), lambda bb, i: (bb, i, 0)),
            pl.BlockSpec((1, s, IDX_DIM), lambda bb, i: (bb, 0, 0)),
            pl.BlockSpec((1, s, DSA_QX), lambda bb, i: (bb, 0, 0)),
            pl.BlockSpec((DSA_HEADS, DSA_QK_DIM, DSA_QX), lambda bb, i: (0, 0, 0)),
            pl.BlockSpec((DSA_HEADS, DSA_KV_RANK, DSA_V_DIM), lambda bb, i: (0, 0, 0)),
        ],
        out_specs=pl.BlockSpec((1, Q_BLOCK, DSA_HEADS * DSA_V_DIM), lambda bb, i: (bb, i, 0)),
        out_shape=jax.ShapeDtypeStruct((b, s, DSA_HEADS * DSA_V_DIM), BF16),
        scratch_shapes=[
            pltpu.VMEM((DSA_HEADS * Q_BLOCK, DSA_QX), BF16),
            pltpu.VMEM((IDX_HEADS * Q_BLOCK, IDX_DIM), BF16),
            pltpu.VMEM((IDX_HEADS * Q_BLOCK, LANES), F32),
            pltpu.VMEM((s // kc, Q_BLOCK, kc), F32),
            pltpu.VMEM((DSA_HEADS * Q_BLOCK, 1), F32),
            pltpu.VMEM((DSA_HEADS * Q_BLOCK, 1), F32),
            pltpu.VMEM((DSA_HEADS * Q_BLOCK, DSA_KV_RANK), F32),
        ],
        compiler_params=_params("arbitrary", "arbitrary"),
        name="dsa_attn",
    )(qall, qall, wi, tabs, kidx, kext, wk, wuv)


def _dsa_mixer(h2, bsz, seq, positions, attn_norm, w_in, q_norm, kv_norm, idx_ln_g, idx_ln_b,
               w_qup, w_uk, w_uv, w_out):
    t = h2.shape[0]
    w_in_p = jnp.pad(w_in, ((0, 0), (0, 1024 - DSA_IN))).astype(BF16)
    proj = _mm(h2, w_in_p, gain=attn_norm, tn=1024)
    o0 = DSA_Q_RANK
    ckv = proj[:, o0:o0 + DSA_KV_RANK]
    o1 = o0 + DSA_KV_RANK
    k_pe = proj[:, o1:o1 + DSA_ROPE]
    o2 = o1 + DSA_ROPE
    k_idx = proj[:, o2:o2 + IDX_DIM]
    o3 = o2 + IDX_DIM
    w_idx = proj[:, o3:o3 + IDX_HEADS] * (IDX_HEADS ** -0.5 * IDX_DIM ** -0.5)

    qall = _mm(proj, w_qup.astype(BF16), gain=q_norm, out_dtype=BF16, tn=2048)

    pos = positions.reshape(t)
    cos, sin = _rope_cos_sin(pos, DSA_ROPE)
    icos, isin = _rope_cos_sin(pos, IDX_ROPE)
    c_n = ckv * lax.rsqrt(jnp.mean(ckv * ckv, -1, keepdims=True) + NORM_EPS) * kv_norm
    kext = jnp.concatenate(
        [c_n, _rope_glue(k_pe, cos, sin), jnp.zeros((t, DSA_QX - DSA_KV_RANK - DSA_ROPE), F32)], -1)
    mu = jnp.mean(k_idx, -1, keepdims=True)
    var = jnp.mean(jnp.square(k_idx - mu), -1, keepdims=True)
    k_ln = (k_idx - mu) * lax.rsqrt(var + NORM_EPS) * idx_ln_g + idx_ln_b
    kidx = _rope_glue(k_ln, icos, isin)
    tabs = jnp.concatenate([_rope_lane_tables(pos, DSA_ROPE), _rope_lane_tables(pos, IDX_ROPE)], -1)

    wk = jnp.zeros((DSA_HEADS, DSA_QK_DIM, DSA_QX), F32)
    wk = wk.at[:, DSA_ROPE:, :DSA_KV_RANK].set(jnp.swapaxes(w_uk, 1, 2))
    eye = jnp.eye(DSA_ROPE, dtype=F32)
    wk = wk.at[:, :DSA_ROPE, DSA_KV_RANK:DSA_KV_RANK + DSA_ROPE].set(jnp.broadcast_to(eye, (DSA_HEADS,) + eye.shape))

    o = _dsa_attention(
        qall.reshape(bsz, seq, -1), w_idx.reshape(bsz, seq, IDX_HEADS), tabs.reshape(bsz, seq, -1),
        kidx.astype(BF16).reshape(bsz, seq, IDX_DIM), kext.astype(BF16).reshape(bsz, seq, DSA_QX),
        wk.astype(BF16), w_uv.astype(BF16))
    return _mm(o.reshape(t, -1), w_out.astype(BF16), res=h2)


def _fox_body(q_ref, k_ref, v_ref, cum_ref, o_ref, m_ref, l_ref, acc_ref, *, tq, kc):
    i = pl.program_id(2)
    q0 = i * tq
    nc = (q0 + tq) // kc
    scale = FOX_DIM ** -0.5
    q = q_ref[0]
    m_ref[...] = jnp.full_like(m_ref, MASK_NEG)
    l_ref[...] = jnp.zeros_like(l_ref)
    acc_ref[...] = jnp.zeros_like(acc_ref)
    t_col = q0 + lax.broadcasted_iota(jnp.int32, (tq, kc), 0)
    lane = lax.broadcasted_iota(jnp.int32, (tq, kc), 1)

    def chunk(c, _):
        k0 = pl.multiple_of(c * kc, kc)
        s = _nt(q, k_ref[0, pl.ds(k0, kc), :]) * scale - cum_ref[0, 0, c]
        s = jnp.where((k0 + lane) <= t_col, s, MASK_NEG)
        m_prev = m_ref[...]
        m_new = jnp.maximum(m_prev, s.max(axis=1, keepdims=True))
        alpha = jnp.exp(m_prev - m_new)
        p = jnp.exp(s - m_new)
        l_ref[...] = alpha * l_ref[...] + p.sum(axis=1, keepdims=True)
        acc_ref[...] = alpha * acc_ref[...] + jnp.dot(
            p.astype(BF16), v_ref[0, pl.ds(k0, kc), :], preferred_element_type=F32)
        m_ref[...] = m_new
        return 0

    lax.fori_loop(0, nc, chunk, 0)
    o_ref[0] = (acc_ref[...] / l_ref[...]).astype(o_ref.dtype)


def _fox_attention(qkv, cum, *, tq=512, kc=512):
    b, s, _ = qkv.shape
    tq = min(tq, s)
    kc = min(kc, tq)
    assert s % tq == 0 and tq % kc == 0
    hh = FOX_HEADS
    return pl.pallas_call(
        functools.partial(_fox_body, tq=tq, kc=kc),
        grid=(b, hh, s // tq),
        in_specs=[
            pl.BlockSpec((1, tq, FOX_DIM), lambda bb, h, i: (bb, i, h)),
            pl.BlockSpec((1, s, FOX_DIM), lambda bb, h, i: (bb, 0, hh + h)),
            pl.BlockSpec((1, s, FOX_DIM), lambda bb, h, i: (bb, 0, 2 * hh + h)),
            pl.BlockSpec((1, 1, s // kc, 1, kc), lambda bb, h, i: (bb, h, 0, 0, 0)),
        ],
        out_specs=pl.BlockSpec((1, tq, FOX_DIM), lambda bb, h, i: (bb, i, h)),
        out_shape=jax.ShapeDtypeStruct((b, s, hh * FOX_DIM), BF16),
        scratch_shapes=[pltpu.VMEM((tq, 1), F32), pltpu.VMEM((tq, 1), F32), pltpu.VMEM((tq, FOX_DIM), F32)],
        compiler_params=_params("arbitrary", "arbitrary", "arbitrary"),
        name="fox_attn",
    )(qkv, qkv, qkv, cum)


def _fox_mixer(h2, bsz, seq, attn_norm, w_in, b_f, w_out, *, kc=512):
    t = h2.shape[0]
    hd = FOX_HEADS * FOX_DIM
    kc = min(kc, seq)
    qkv = _mm(h2, w_in[:, :3 * hd].astype(BF16), gain=attn_norm, out_dtype=BF16, tn=1024)
    w_f = jnp.pad(w_in[:, 3 * hd:], ((0, 0), (0, LANES - FOX_HEADS))).astype(BF16)
    f_logit = _mm(h2, w_f, gain=attn_norm)[:, :FOX_HEADS]
    log_f = jax.nn.log_sigmoid(f_logit + b_f)
    cum = jnp.cumsum(log_f.reshape(bsz, seq, FOX_HEADS), axis=1).transpose(0, 2, 1)
    cum = cum.reshape(bsz, FOX_HEADS, seq // kc, 1, kc)
    o = _fox_attention(qkv.reshape(bsz, seq, 3 * hd), cum, kc=kc)
    return _mm(o.reshape(t, hd), w_out.astype(BF16), res=h2)


def _cmp_body(x_ref, pe_ref, w1_ref, b1_ref, w2_ref, o_ref, *, half):
    x = x_ref[0, 0]
    w_lo = w1_ref[:half, :]
    w_hi = w1_ref[half:, :]
    a = jnp.dot(x, w_lo, preferred_element_type=F32)
    bnext = jnp.dot(x, w_hi, preferred_element_type=F32)
    n = x.shape[0]
    bnext = pltpu.roll(bnext, n - 1, 0)
    pe = pe_ref[...]
    pe_b = (jnp.dot(pe[:, :half], w_lo, preferred_element_type=F32)
            + jnp.dot(pe[:, half:], w_hi, preferred_element_type=F32))[0:1, :]
    hid = jax.nn.gelu(a + bnext + pe_b + b1_ref[...])
    o_ref[0, 0] = jnp.dot(hid.astype(BF16), w2_ref[...], preferred_element_type=F32).astype(o_ref.dtype)


def _compress(x, pe, w1, b1, w2):
    b, g, n, kd = x.shape
    dout = w2.shape[1]
    pe8 = jnp.broadcast_to(pe.reshape(1, 2 * kd), (8, 2 * kd)).astype(BF16)
    return pl.pallas_call(
        functools.partial(_cmp_body, half=kd),
        grid=(b, g),
        in_specs=[
            pl.BlockSpec((1, 1, n, kd), lambda bb, gg: (bb, gg, 0, 0)),
            pl.BlockSpec((8, 2 * kd), lambda bb, gg: (0, 0)),
            pl.BlockSpec((2 * kd, CMP_HIDDEN), lambda bb, gg: (0, 0)),
            pl.BlockSpec((1, CMP_HIDDEN), lambda bb, gg: (0, 0)),
            pl.BlockSpec((CMP_HIDDEN, dout), lambda bb, gg: (0, 0)),
        ],
        out_specs=pl.BlockSpec((1, 1, n, dout), lambda bb, gg: (bb, gg, 0, 0)),
        out_shape=jax.ShapeDtypeStruct((b, g, n, dout), BF16),
        compiler_params=_params("arbitrary", "arbitrary"),
        name="nsa_compress",
    )(x, pe8, w1.astype(BF16), b1.reshape(1, CMP_HIDDEN), w2.astype(BF16))


def _nsa_body(q_ref, tab_ref, gate_ref, kc_ref, vc_ref, ks_ref, vs_ref, kw_ref, vw_ref, ovt_ref, exp_ref,
              o_ref, qs_ref, m_ref, l_ref, acc_ref, *, kc, n_cmp, n_slc, n_sel):
    i = pl.program_id(2)
    q0 = i * Q_BLOCK
    nc = (q0 + Q_BLOCK + kc - 1) // kc
    jh = NSA_HPG
    rows = jh * Q_BLOCK
    scale = NSA_QK_DIM ** -0.5
    tab = tab_ref[0]
    cf, sa, sb = tab[:, 0:LANES], tab[:, LANES:2 * LANES], tab[:, 2 * LANES:3 * LANES]

    for j in range(jh):
        xj = q_ref[0, :, j * NSA_QK_DIM:(j + 1) * NSA_QK_DIM].astype(F32)
        r0 = _rope_first_vreg(xj[:, :LANES], cf, sa, sb, NSA_ROPE // 2)
        qs_ref[j * Q_BLOCK:(j + 1) * Q_BLOCK, :] = (jnp.concatenate([r0, xj[:, LANES:]], axis=1) * scale).astype(BF16)
    qs = qs_ref[...]

    def softmax_rows(z, mask):
        z = jnp.where(mask, z, MASK_NEG)
        e = jnp.where(mask, jnp.exp(z - z.max(axis=-1, keepdims=True)), 0.0)
        den = e.sum(axis=-1, keepdims=True)
        return e / jnp.where(den > 0, den, 1.0)

    t_c = q0 + lax.broadcasted_iota(jnp.int32, (Q_BLOCK, n_cmp), 0)
    n_id = lax.broadcasted_iota(jnp.int32, (Q_BLOCK, n_cmp), 1)
    cmask = (n_id * CMP_STRIDE + (CMP_BLOCK - 1)) <= t_c
    lc = _nt(qs, kc_ref[0, 0]).reshape(jh, Q_BLOCK, n_cmp)
    pc = softmax_rows(lc, cmask[None])
    o_c = jnp.dot(pc.reshape(rows, n_cmp).astype(BF16), vc_ref[0, 0], preferred_element_type=F32)

    pcs = pc.sum(axis=0)
    p_hi = pcs.astype(BF16)
    r1 = pcs - p_hi.astype(F32)
    p_mid = r1.astype(BF16)
    p_lo = (r1 - p_mid.astype(F32)).astype(BF16)
    ovt = ovt_ref[...]
    imp = _nt(ovt, p_hi) + _nt(ovt, p_mid) + _nt(ovt, p_lo)

    blk = lax.broadcasted_iota(jnp.int32, (n_slc, Q_BLOCK), 0)
    t_l = q0 + lax.broadcasted_iota(jnp.int32, (n_slc, Q_BLOCK), 1)
    cur = lax.shift_right_logical(t_l, int(np.log2(SLC_BLOCK)))
    causal_blk = blk * SLC_BLOCK <= t_l
    forced = (blk == 0) | (blk == cur) | (blk == cur - 1)
    val = jnp.where(causal_blk, jnp.where(forced, jnp.inf, imp), -jnp.inf)
    rank = jnp.zeros((n_slc, Q_BLOCK), F32)
    for mp in range(n_slc):
        vrow = val[mp:mp + 1, :]
        before = (vrow > val) | ((vrow == val) & (blk > mp))
        rank = rank + jnp.where(before, 1.0, 0.0)
    sel = jnp.where((rank < n_sel) & causal_blk, 1.0, 0.0)
    sel_q = jnp.concatenate([sel, jnp.zeros((LANES - n_slc, Q_BLOCK), F32)], axis=0).T.astype(BF16)

    m_ref[...] = jnp.full_like(m_ref, MASK_NEG)
    l_ref[...] = jnp.zeros_like(l_ref)
    acc_ref[...] = jnp.zeros_like(acc_ref)
    t_k = q0 + lax.broadcasted_iota(jnp.int32, (Q_BLOCK, kc), 0)
    lane = lax.broadcasted_iota(jnp.int32, (Q_BLOCK, kc), 1)

    def sel_chunk(c, _):
        k0 = pl.multiple_of(c * kc, kc)
        hit = jnp.dot(sel_q, exp_ref[c], preferred_element_type=F32)
        bias = jnp.where((hit > 0.5) & ((k0 + lane) <= t_k), 0.0, MASK_NEG)
        s = _nt(qs, ks_ref[0, 0, pl.ds(k0, kc), :])
        s = (s.reshape(jh, Q_BLOCK, kc) + bias[None]).reshape(rows, kc)
        m_prev = m_ref[...]
        m_new = jnp.maximum(m_prev, s.max(axis=1, keepdims=True))
        alpha = jnp.exp(m_prev - m_new)
        p = jnp.exp(s - m_new)
        l_ref[...] = alpha * l_ref[...] + p.sum(axis=1, keepdims=True)
        acc_ref[...] = alpha * acc_ref[...] + jnp.dot(
            p.astype(BF16), vs_ref[0, 0, pl.ds(k0, kc), :], preferred_element_type=F32)
        m_ref[...] = m_new
        return 0

    lax.fori_loop(0, nc, sel_chunk, 0)
    o_s = acc_ref[...] / l_ref[...]

    wl = WINDOW + Q_BLOCK
    kstart = pl.multiple_of(q0, Q_BLOCK)
    t_w = q0 + lax.broadcasted_iota(jnp.int32, (Q_BLOCK, wl), 0)
    s_pos = q0 - WINDOW + lax.broadcasted_iota(jnp.int32, (Q_BLOCK, wl), 1)
    wmask = (s_pos >= 0) & (s_pos <= t_w) & (s_pos > t_w - WINDOW)
    lw = _nt(qs, kw_ref[0, 0, pl.ds(kstart, wl), :]).reshape(jh, Q_BLOCK, wl)
    pw = softmax_rows(lw, wmask[None])
    o_w = jnp.dot(pw.reshape(rows, wl).astype(BF16), vw_ref[0, 0, pl.ds(kstart, wl), :],
                  preferred_element_type=F32)

    g = gate_ref[0, 0]
    for j in range(jh):
        r = slice(j * Q_BLOCK, (j + 1) * Q_BLOCK)
        out = (g[:, j:j + 1] * o_c[r, :] + g[:, jh + j:jh + j + 1] * o_s[r, :]
               + g[:, 2 * jh + j:2 * jh + j + 1] * o_w[r, :])
        o_ref[0, :, j * NSA_V_DIM:(j + 1) * NSA_V_DIM] = out.astype(o_ref.dtype)


def _nsa_attention(q, tabs, gates, kcmp, vcmp, ks, vs, kw, vw, *, kc=512):
    b, s, _ = q.shape
    g = NSA_GROUPS
    kc = min(kc, s)
    assert s % kc == 0 and kc % SLC_BLOCK == 0
    n_cmp = kcmp.shape[2]
    n_slc = s // SLC_BLOCK
    assert n_slc <= LANES
    n_sel = min(SLC_TOPK, n_slc)
    cmp_start = np.arange(n_cmp) * CMP_STRIDE
    slc_start = np.arange(n_slc) * SLC_BLOCK
    ovt = ((cmp_start[None, :] < slc_start[:, None] + SLC_BLOCK)
           & (cmp_start[None, :] + CMP_BLOCK > slc_start[:, None])).astype(np.float32)
    expand = np.zeros((s // kc, LANES, kc), np.float32)
    key_blk = np.arange(s) // SLC_BLOCK
    expand[np.arange(s) // kc, key_blk, np.arange(s) % kc] = 1.0
    qw = NSA_HPG * NSA_QK_DIM
    ow = NSA_HPG * NSA_V_DIM
    wl = s + WINDOW
    return pl.pallas_call(
        functools.partial(_nsa_body, kc=kc, n_cmp=n_cmp, n_slc=n_slc, n_sel=float(n_sel)),
        grid=(b, g, s // Q_BLOCK),
        in_specs=[
            pl.BlockSpec((1, Q_BLOCK, qw), lambda bb, gg, i: (bb, i, gg)),
            pl.BlockSpec((1, Q_BLOCK, 3 * LANES), lambda bb, gg, i: (bb, i, 0)),
            pl.BlockSpec((1, 1, Q_BLOCK, 3 * NSA_HPG), lambda bb, gg, i: (bb, gg, i, 0)),
            pl.BlockSpec((1, 1, n_cmp, NSA_QK_DIM), lambda bb, gg, i: (bb, gg, 0, 0)),
            pl.BlockSpec((1, 1, n_cmp, NSA_V_DIM), lambda bb, gg, i: (bb, gg, 0, 0)),
            pl.BlockSpec((1, 1, s, NSA_QK_DIM), lambda bb, gg, i: (bb, gg, 0, 0)),
            pl.BlockSpec((1, 1, s, NSA_V_DIM), lambda bb, gg, i: (bb, gg, 0, 0)),
            pl.BlockSpec((1, 1, wl, NSA_QK_DIM), lambda bb, gg, i: (bb, gg, 0, 0)),
            pl.BlockSpec((1, 1, wl, NSA_V_DIM), lambda bb, gg, i: (bb, gg, 0, 0)),
            pl.BlockSpec((n_slc, n_cmp), lambda bb, gg, i: (0, 0)),
            pl.BlockSpec((s // kc, LANES, kc), lambda bb, gg, i: (0, 0, 0)),
        ],
        out_specs=pl.BlockSpec((1, Q_BLOCK, ow), lambda bb, gg, i: (bb, i, gg)),
        out_shape=jax.ShapeDtypeStruct((b, s, NSA_HEADS * NSA_V_DIM), BF16),
        scratch_shapes=[
            pltpu.VMEM((NSA_HPG * Q_BLOCK, NSA_QK_DIM), BF16),
            pltpu.VMEM((NSA_HPG * Q_BLOCK, 1), F32),
            pltpu.VMEM((NSA_HPG * Q_BLOCK, 1), F32),
            pltpu.VMEM((NSA_HPG * Q_BLOCK, NSA_V_DIM), F32),
        ],
        compiler_params=_params("arbitrary", "arbitrary", "arbitrary"),
        name="nsa_attn",
    )(q, tabs, gates, kcmp, vcmp, ks, vs, kw, vw, jnp.asarray(ovt, BF16), jnp.asarray(expand, BF16))


def _nsa_mixer(h2, bsz, seq, positions, attn_norm, w_in, k_pe, k_w1, k_b1, k_w2, v_pe, v_w1, v_b1, v_w2, w_out):
    t = h2.shape[0]
    g, jh, dk, dv = NSA_GROUPS, NSA_HPG, NSA_QK_DIM, NSA_V_DIM
    nq = NSA_HEADS * dk
    n_rest = w_in.shape[1] - nq
    n_rest_p = -(-n_rest // 512) * 512
    q = _mm(h2, w_in[:, :nq].astype(BF16), gain=attn_norm, out_dtype=BF16, tm=1024, tn=1024)
    w_rest = jnp.pad(w_in[:, nq:], ((0, 0), (0, n_rest_p - n_rest))).astype(BF16)
    rest = _mm(h2, w_rest, gain=attn_norm, tn=512)
    offs = np.cumsum([0, NSA_KD, NSA_VD, NSA_KD, NSA_VD, NSA_KD, NSA_VD, 3 * NSA_HEADS])
    k_c, v_c, k_s, v_s, k_w, v_w, g_logit = [rest[:, int(a):int(b)] for a, b in zip(offs[:-1], offs[1:])]

    pos = positions.reshape(t)
    cos, sin = _rope_cos_sin(pos, NSA_ROPE)

    def rope_k(k):
        return _rope_glue(k.reshape(t, g, dk), cos[:, None, :], sin[:, None, :])

    def per_group(x, d):
        return x.reshape(bsz, seq, g, d).transpose(0, 2, 1, 3).astype(BF16)

    def chunks(x, d):
        x = x.reshape(bsz, seq // CMP_STRIDE, CMP_STRIDE, g, d).transpose(0, 3, 1, 2, 4)
        return x.reshape(bsz, g, seq // CMP_STRIDE, CMP_STRIDE * d).astype(BF16)

    kcmp = _compress(chunks(rope_k(k_c), dk), k_pe, k_w1, k_b1, k_w2)
    vcmp = _compress(chunks(v_c.reshape(t, g, dv), dv), v_pe, v_w1, v_b1, v_w2)
    ks = per_group(rope_k(k_s), dk)
    vs = per_group(v_s.reshape(t, g, dv), dv)
    pad = ((0, 0), (0, 0), (WINDOW, 0), (0, 0))
    kw = jnp.pad(per_group(rope_k(k_w), dk), pad)
    vw = jnp.pad(per_group(v_w.reshape(t, g, dv), dv), pad)
    gates = jax.nn.sigmoid(g_logit).reshape(bsz, seq, 3, g, jh).transpose(0, 3, 1, 2, 4).reshape(bsz, g, seq, 3 * jh)
    tabs = _rope_lane_tables(pos, NSA_ROPE).reshape(bsz, seq, 3 * LANES)

    o = _nsa_attention(q.reshape(bsz, seq, nq), tabs, gates, kcmp, vcmp, ks, vs, kw, vw)
    return _mm(o.reshape(t, NSA_HEADS * dv), w_out.astype(BF16), res=h2)


def _norm_body(x_ref, g_ref, o_ref):
    o_ref[...] = _rms(x_ref[...], g_ref[...])


def _final_norm(h2, gain, *, tm=512):
    t, d = h2.shape
    tm = min(tm, t)
    return pl.pallas_call(
        _norm_body,
        grid=(t // tm,),
        in_specs=[pl.BlockSpec((tm, d), lambda i: (i, 0)), pl.BlockSpec((1, d), lambda i: (0, 0))],
        out_specs=pl.BlockSpec((tm, d), lambda i: (i, 0)),
        out_shape=jax.ShapeDtypeStruct((t, d), F32),
        compiler_params=_params("arbitrary"),
        name="final_norm",
    )(h2, gain.reshape(1, d))


def _ffn_layer(h2, seq, ffn_norm, w_up, conv_w, conv_b, w_down):
    return _ffn(h2, seq, ffn_norm, w_up.astype(BF16), conv_w, conv_b, w_down.astype(BF16))


def kernel(x, positions, l0_attn_norm, l0_dsa_w_in, l0_dsa_q_norm, l0_dsa_kv_norm, l0_dsa_idx_ln_g, l0_dsa_idx_ln_b, l0_dsa_w_qup, l0_dsa_w_uk, l0_dsa_w_uv, l0_dsa_w_out, l0_ffn_norm, l0_ffn_up, l0_ffn_conv_w, l0_ffn_conv_b, l0_ffn_down, l1_attn_norm, l1_fox_w_in, l1_fox_b_f, l1_fox_w_out, l1_ffn_norm, l1_ffn_up, l1_ffn_conv_w, l1_ffn_conv_b, l1_ffn_down, l2_attn_norm, l2_nsa_w_in, l2_nsa_k_pe, l2_nsa_k_w1, l2_nsa_k_b1, l2_nsa_k_w2, l2_nsa_v_pe, l2_nsa_v_w1, l2_nsa_v_b1, l2_nsa_v_w2, l2_nsa_w_out, l2_ffn_norm, l2_ffn_up, l2_ffn_conv_w, l2_ffn_conv_b, l2_ffn_down, l3_attn_norm, l3_dsa_w_in, l3_dsa_q_norm, l3_dsa_kv_norm, l3_dsa_idx_ln_g, l3_dsa_idx_ln_b, l3_dsa_w_qup, l3_dsa_w_uk, l3_dsa_w_uv, l3_dsa_w_out, l3_ffn_norm, l3_ffn_up, l3_ffn_conv_w, l3_ffn_conv_b, l3_ffn_down, final_norm):
    bsz, seq, d = x.shape
    h = x.reshape(bsz * seq, d)
    h = _dsa_mixer(h, bsz, seq, positions, l0_attn_norm, l0_dsa_w_in, l0_dsa_q_norm, l0_dsa_kv_norm,
                   l0_dsa_idx_ln_g, l0_dsa_idx_ln_b, l0_dsa_w_qup, l0_dsa_w_uk, l0_dsa_w_uv, l0_dsa_w_out)
    h = _ffn_layer(h, seq, l0_ffn_norm, l0_ffn_up, l0_ffn_conv_w, l0_ffn_conv_b, l0_ffn_down)
    h = _fox_mixer(h, bsz, seq, l1_attn_norm, l1_fox_w_in, l1_fox_b_f, l1_fox_w_out)
    h = _ffn_layer(h, seq, l1_ffn_norm, l1_ffn_up, l1_ffn_conv_w, l1_ffn_conv_b, l1_ffn_down)
    h = _nsa_mixer(h, bsz, seq, positions, l2_attn_norm, l2_nsa_w_in, l2_nsa_k_pe, l2_nsa_k_w1, l2_nsa_k_b1,
                   l2_nsa_k_w2, l2_nsa_v_pe, l2_nsa_v_w1, l2_nsa_v_b1, l2_nsa_v_w2, l2_nsa_w_out)
    h = _ffn_layer(h, seq, l2_ffn_norm, l2_ffn_up, l2_ffn_conv_w, l2_ffn_conv_b, l2_ffn_down)
    h = _dsa_mixer(h, bsz, seq, positions, l3_attn_norm, l3_dsa_w_in, l3_dsa_q_norm, l3_dsa_kv_norm,
                   l3_dsa_idx_ln_g, l3_dsa_idx_ln_b, l3_dsa_w_qup, l3_dsa_w_uk, l3_dsa_w_uv, l3_dsa_w_out)
    h = _ffn_layer(h, seq, l3_ffn_norm, l3_ffn_up, l3_ffn_conv_w, l3_ffn_conv_b, l3_ffn_down)
    return _final_norm(h, final_norm).reshape(bsz, seq, d)
```

```python
import functools

import jax
import jax.numpy as jnp
import numpy as np
from jax import lax
from jax.experimental import pallas as pl
from jax.experimental.pallas import tpu as pltpu

F32 = jnp.float32
BF16 = jnp.bfloat16

ROPE_THETA = 500000.0
NORM_EPS = 1e-6
Q_BLOCK = 128

DSA_HEADS = 32
DSA_Q_RANK = 512
DSA_KV_RANK = 256
DSA_QK_DIM = 192
DSA_ROPE = 48
DSA_NOPE = DSA_QK_DIM - DSA_ROPE
DSA_V_DIM = 128
IDX_HEADS = 16
IDX_DIM = 128
IDX_ROPE = 32
DSA_TOPK_MAX = 256
DSA_IN = DSA_Q_RANK + DSA_KV_RANK + DSA_ROPE + IDX_DIM + IDX_HEADS

FOX_HEADS = 16
FOX_DIM = 128

NSA_HEADS = 48
NSA_GROUPS = 4
NSA_HPG = NSA_HEADS // NSA_GROUPS
NSA_QK_DIM = 192
NSA_ROPE = 48
NSA_V_DIM = 128
CMP_BLOCK = 32
CMP_STRIDE = 16
CMP_HIDDEN = 256
SLC_BLOCK = 64
SLC_TOPK = 16
WINDOW = 512
NSA_KD = NSA_GROUPS * NSA_QK_DIM
NSA_VD = NSA_GROUPS * NSA_V_DIM

CONV_WIDTH = 3

LANES = 128
SUBLANES = 8
BF16_SUBLANES = 16
MXU_DIM = 256
VMEM_LIMIT = 56 * 1024 * 1024

MASK_NEG = -1e30
BISECT_ITERS = 32


def _params(*sem):
    return pltpu.CompilerParams(dimension_semantics=sem, vmem_limit_bytes=VMEM_LIMIT)


def _nt(a, b):
    return lax.dot_general(a, b, (((1,), (1,)), ((), ())), preferred_element_type=F32)


def _tn(a, b):
    return lax.dot_general(a, b, (((0,), (0,)), ((), ())), preferred_element_type=F32)


def _rms(x, g):
    return x * lax.rsqrt(jnp.mean(x * x, axis=-1, keepdims=True) + NORM_EPS) * g


def _rope_first_vreg(x0, cf, sa, sb, half):
    return (x0 * cf + pltpu.roll(x0, LANES - half, 1) * sa + pltpu.roll(x0, half, 1) * sb)


def _split3(x):
    hi = x.astype(BF16)
    r1 = x - hi.astype(F32)
    mid = r1.astype(BF16)
    lo = (r1 - mid.astype(F32)).astype(BF16)
    return hi, mid, lo


def _split3_bits(x):
    def trunc(v):
        bits = lax.bitcast_convert_type(v, jnp.uint32) & jnp.uint32(0xFFFF0000)
        return lax.bitcast_convert_type(bits, F32)

    hi = trunc(x)
    r1 = x - hi
    mid = trunc(r1)
    lo = r1 - mid
    return hi.astype(BF16), mid.astype(BF16), lo.astype(BF16)


def _online_softmax_step(s, pv_lhs, m_ref, l_ref, acc_ref, cols):
    m_prev = m_ref[:, cols]
    m_new = jnp.maximum(m_prev, s.max(axis=0, keepdims=True))
    alpha = jnp.exp(m_prev - m_new)
    p = jnp.exp(s - m_new)
    l_ref[:, cols] = alpha * l_ref[:, cols] + p.sum(axis=0, keepdims=True)
    acc_ref[:, cols] = alpha * acc_ref[:, cols] + jnp.dot(pv_lhs, p.astype(BF16), preferred_element_type=F32)
    m_ref[:, cols] = m_new


def _softmax_cols(z):
    mx = z.max(axis=0, keepdims=True)
    e = jnp.exp(z - mx)
    den = e.sum(axis=0, keepdims=True)
    return e * jnp.where(mx > 0.5 * MASK_NEG, 1.0 / den, 0.0)


def _mm_body(*refs, has_gain, has_res):
    x_ref, w_ref = refs[0], refs[1]
    k = 2
    g_ref = r_ref = None
    if has_gain:
        g_ref = refs[k]
        k += 1
    if has_res:
        r_ref = refs[k]
        k += 1
    o_ref = refs[k]
    if has_gain:
        xn_ref = refs[k + 1]

        @pl.when(pl.program_id(1) == 0)
        def _():
            xn_ref[...] = _rms(x_ref[...].astype(F32), g_ref[...]).astype(BF16)

        a = xn_ref[...]
    else:
        a = x_ref[...].astype(BF16)
    acc = jnp.dot(a, w_ref[...], preferred_element_type=F32)
    if has_res:
        acc = acc + r_ref[...]
    o_ref[...] = acc.astype(o_ref.dtype)


def _mm(x, w, *, gain=None, res=None, out_dtype=F32, x_col_block=0, tm=512, tn=512):
    m = x.shape[0]
    k, n = w.shape
    tm = min(tm, m)
    tn = min(tn, n)
    assert m % tm == 0 and n % tn == 0 and x.shape[1] % k == 0
    in_specs = [pl.BlockSpec((tm, k), lambda i, j: (i, x_col_block)),
                pl.BlockSpec((k, tn), lambda i, j: (0, j))]
    args = [x, w]
    scratch = []
    if gain is not None:
        in_specs.append(pl.BlockSpec((1, k), lambda i, j: (0, 0)))
        args.append(gain.reshape(1, k).astype(F32))
        scratch.append(pltpu.VMEM((tm, k), BF16))
    if res is not None:
        in_specs.append(pl.BlockSpec((tm, tn), lambda i, j: (i, j)))
        args.append(res)
    return pl.pallas_call(
        functools.partial(_mm_body, has_gain=gain is not None, has_res=res is not None),
        grid=(m // tm, n // tn),
        in_specs=in_specs,
        out_specs=pl.BlockSpec((tm, tn), lambda i, j: (i, j)),
        out_shape=jax.ShapeDtypeStruct((m, n), out_dtype),
        scratch_shapes=scratch,
        compiler_params=_params("arbitrary", "arbitrary"),
        name="mm",
    )(*args)


FFN_HALO = BF16_SUBLANES


def _ffn_body(h_ref, hp_ref, g_ref, wg_ref, wv_ref, cwg_ref, cwv_ref, cbg_ref, cbv_ref, wd_ref,
              o_ref, xn_ref, ug_ref, uv_ref, acc_ref, *, tm, seq):
    i = pl.program_id(0)
    j = pl.program_id(1)

    @pl.when(j == 0)
    def _():
        xn_ref[FFN_HALO:, :] = _rms(h_ref[...], g_ref[...]).astype(BF16)
        prev = _rms(hp_ref[...], g_ref[...])
        seq_start = (i * tm) % seq == 0
        xn_ref[:FFN_HALO, :] = jnp.where(seq_start, 0.0, prev).astype(BF16)
        acc_ref[...] = jnp.zeros_like(acc_ref)

    xn = xn_ref[...]
    ug_ref[...] = jnp.dot(xn, wg_ref[...], preferred_element_type=F32)
    uv_ref[...] = jnp.dot(xn, wv_ref[...], preferred_element_type=F32)

    def conv(u_ref, cw_ref, cb_ref):
        y = cb_ref[...]
        for t in range(CONV_WIDTH):
            off = FFN_HALO - (CONV_WIDTH - 1) + t
            y = y + cw_ref[t:t + 1, :] * u_ref[off:off + tm, :]
        return y

    yg = conv(ug_ref, cwg_ref, cbg_ref)
    yv = conv(uv_ref, cwv_ref, cbv_ref)
    a = (jax.nn.silu(yg) * yv).astype(BF16)
    acc_ref[...] += jnp.dot(a, wd_ref[...], preferred_element_type=F32)

    @pl.when(j == pl.num_programs(1) - 1)
    def _():
        o_ref[...] = h_ref[...] + acc_ref[...]


def _ffn(h, seq, gain, w_up, conv_w, conv_b, w_down, *, tm=512, tf=512):
    t, d = h.shape
    dff = w_down.shape[0]
    tm = min(tm, seq)
    assert seq % tm == 0 and dff % tf == 0 and tm % FFN_HALO == 0
    nf = dff // tf
    hb = tm // FFN_HALO
    conv_b2 = conv_b.reshape(1, 2 * dff)
    return pl.pallas_call(
        functools.partial(_ffn_body, tm=tm, seq=seq),
        grid=(t // tm, nf),
        in_specs=[
            pl.BlockSpec((tm, d), lambda i, j: (i, 0)),
            pl.BlockSpec((FFN_HALO, d), lambda i, j: (jnp.maximum(i * hb - 1, 0), 0)),
            pl.BlockSpec((1, d), lambda i, j: (0, 0)),
            pl.BlockSpec((d, tf), lambda i, j: (0, j)),
            pl.BlockSpec((d, tf), lambda i, j: (0, nf + j)),
            pl.BlockSpec((CONV_WIDTH, tf), lambda i, j: (0, j)),
            pl.BlockSpec((CONV_WIDTH, tf), lambda i, j: (0, nf + j)),
            pl.BlockSpec((1, tf), lambda i, j: (0, j)),
            pl.BlockSpec((1, tf), lambda i, j: (0, nf + j)),
            pl.BlockSpec((tf, d), lambda i, j: (j, 0)),
        ],
        out_specs=pl.BlockSpec((tm, d), lambda i, j: (i, 0)),
        out_shape=jax.ShapeDtypeStruct((t, d), F32),
        scratch_shapes=[
            pltpu.VMEM((tm + FFN_HALO, d), BF16),
            pltpu.VMEM((tm + FFN_HALO, tf), F32),
            pltpu.VMEM((tm + FFN_HALO, tf), F32),
            pltpu.VMEM((tm, d), F32),
        ],
        compiler_params=_params("arbitrary", "arbitrary"),
        name="ffn",
    )(h, h, gain.reshape(1, d), w_up, w_up, conv_w, conv_w, conv_b2, conv_b2, w_down)


def _rope_cos_sin(positions, rot):
    inv = ROPE_THETA ** (-jnp.arange(0, rot, 2, dtype=F32) / rot)
    ang = positions.astype(F32)[..., None] * inv
    return jnp.cos(ang), jnp.sin(ang)


def _rope_lane_tables(positions, rot):
    c, s = _rope_cos_sin(positions, rot)
    half = rot // 2
    shp = c.shape[:-1]
    cf = jnp.concatenate([c, c, jnp.ones(shp + (LANES - rot,), F32)], -1)
    sa = jnp.concatenate([-s, jnp.zeros(shp + (LANES - half,), F32)], -1)
    sb = jnp.concatenate([jnp.zeros(shp + (half,), F32), s, jnp.zeros(shp + (LANES - rot,), F32)], -1)
    return jnp.concatenate([cf, sa, sb], -1)


def _rope_glue(x, cos, sin):
    half = cos.shape[-1]
    x1, x2 = x[..., :half], x[..., half:2 * half]
    return jnp.concatenate([x1 * cos - x2 * sin, x1 * sin + x2 * cos, x[..., 2 * half:]], -1)


def _chunks_t(x, kc):
    s, d = x.shape[-2:]
    lead = x.shape[:-2]
    return jnp.swapaxes(x.reshape(lead + (s // kc, kc, d)), -1, -2)


DSA_QX = 384
DSA_QPAD = MXU_DIM
DSA_HEAD_GROUP = 8


def _dsa_body(q_ref, qi_ref, wi_ref, tab_ref, kidx_ref, kext_ref, ct_ref, wk_ref, wuv_ref, o_ref,
              qx_ref, qis_ref, sc_ref, m_ref, l_ref, acc_ref, *, kc, n_keep):
    i = pl.program_id(1)
    q0 = i * Q_BLOCK
    nc = (q0 + Q_BLOCK + kc - 1) // kc
    scale = DSA_QK_DIM ** -0.5
    tab = tab_ref[0]
    cf, sa, sb = tab[:, 0:LANES], tab[:, LANES:2 * LANES], tab[:, 2 * LANES:3 * LANES]
    cfi, sai, sbi = tab[:, 3 * LANES:4 * LANES], tab[:, 4 * LANES:5 * LANES], tab[:, 5 * LANES:6 * LANES]

    for h in range(DSA_HEADS):
        xh = q_ref[0, :, h * DSA_QPAD:(h + 1) * DSA_QPAD].astype(F32)
        r0 = _rope_first_vreg(xh[:, :LANES], cf, sa, sb, DSA_ROPE // 2)
        qh = jnp.concatenate([r0, xh[:, LANES:]], axis=1).astype(BF16)
        qx = jnp.dot(qh, wk_ref[h], preferred_element_type=F32) * scale
        qx_ref[h * Q_BLOCK:(h + 1) * Q_BLOCK, :] = qx.astype(BF16)
    for h in range(IDX_HEADS):
        xi = qi_ref[0, :, h * IDX_DIM:(h + 1) * IDX_DIM].astype(F32)
        qis_ref[h * Q_BLOCK:(h + 1) * Q_BLOCK, :] = _rope_first_vreg(xi, cfi, sai, sbi, IDX_ROPE // 2).astype(BF16)

    wl = wi_ref[0, 0]
    key_i = lax.broadcasted_iota(jnp.int32, (kc, Q_BLOCK), 0)
    t_l = q0 + lax.broadcasted_iota(jnp.int32, (kc, Q_BLOCK), 1)

    def score_chunk(c, carry):
        lo, hi = carry
        k0 = pl.multiple_of(c * kc, kc)
        d = _nt(kidx_ref[0, pl.ds(k0, kc), :], qis_ref[...])
        d = jnp.maximum(d, 0.0) * wl
        sc = d[:, 0:Q_BLOCK]
        for h in range(1, IDX_HEADS):
            sc = sc + d[:, h * Q_BLOCK:(h + 1) * Q_BLOCK]
        sc = sc + 0.0
        causal = (k0 + key_i) <= t_l
        sc_ref[c] = jnp.where(causal, sc, -jnp.inf)
        lo = jnp.minimum(lo, jnp.where(causal, sc, jnp.inf).min(axis=0, keepdims=True))
        hi = jnp.maximum(hi, jnp.where(causal, sc, -jnp.inf).max(axis=0, keepdims=True))
        return lo, hi

    lo, hi = lax.fori_loop(0, nc, score_chunk,
                           (jnp.full((1, Q_BLOCK), jnp.inf, F32), jnp.full((1, Q_BLOCK), -jnp.inf, F32)))

    def bisect(_, carry):
        lo, hi = carry
        mid = lo + (hi - lo) * 0.5

        def count_chunk(c, cnt):
            ge = jnp.where(sc_ref[c] >= mid, 1.0, 0.0)
            return cnt + ge.reshape(kc // SUBLANES, SUBLANES, Q_BLOCK).sum(axis=0)

        cnt = lax.fori_loop(0, nc, count_chunk, jnp.zeros((SUBLANES, Q_BLOCK), F32))
        ge = cnt.sum(axis=0, keepdims=True) >= n_keep
        return jnp.where(ge, mid, lo), jnp.where(ge, hi, mid)

    lo, hi = lax.fori_loop(0, BISECT_ITERS, bisect, (lo, hi))

    def bias_chunk(c, _):
        sc_ref[c] = jnp.where(sc_ref[c] >= lo, 0.0, MASK_NEG)
        return 0

    lax.fori_loop(0, nc, bias_chunk, 0)

    m_ref[...] = jnp.full_like(m_ref, MASK_NEG)
    l_ref[...] = jnp.zeros_like(l_ref)
    acc_ref[...] = jnp.zeros_like(acc_ref)
    gl = DSA_HEAD_GROUP * Q_BLOCK

    def attn_chunk(c, _):
        k0 = pl.multiple_of(c * kc, kc)
        kx = kext_ref[0, pl.ds(k0, kc), :]
        ct = ct_ref[0, c]
        bias = jnp.tile(sc_ref[c], (1, DSA_HEAD_GROUP))
        for g in range(DSA_HEADS // DSA_HEAD_GROUP):
            cols = slice(g * gl, (g + 1) * gl)
            s = _nt(kx, qx_ref[cols, :]) + bias
            _online_softmax_step(s, ct, m_ref, l_ref, acc_ref, cols)
        return 0

    lax.fori_loop(0, nc, attn_chunk, 0)

    for h in range(DSA_HEADS):
        cols = slice(h * Q_BLOCK, (h + 1) * Q_BLOCK)
        o_lat = (acc_ref[:, cols] / l_ref[:, cols]).astype(BF16)
        o_ref[0, :, h * DSA_V_DIM:(h + 1) * DSA_V_DIM] = _tn(o_lat, wuv_ref[h]).astype(o_ref.dtype)


def _dsa_attention(qall, wi_l, tabs, kidx, kext, ct, wk, wuv, *, kc):
    b, s, _ = qall.shape
    assert s % kc == 0 and kc % Q_BLOCK == 0
    n_keep = min(DSA_TOPK_MAX, s // 4)
    nq = DSA_HEADS * DSA_QPAD
    ni = IDX_HEADS * IDX_DIM
    assert nq % ni == 0
    nrow = DSA_HEADS * Q_BLOCK
    return pl.pallas_call(
        functools.partial(_dsa_body, kc=kc, n_keep=float(n_keep)),
        grid=(b, s // Q_BLOCK),
        in_specs=[
            pl.BlockSpec((1, Q_BLOCK, nq), lambda bb, i: (bb, i, 0)),
            pl.BlockSpec((1, Q_BLOCK, ni), lambda bb, i: (bb, i, nq // ni)),
            pl.BlockSpec((1, 1, 1, IDX_HEADS * Q_BLOCK), lambda bb, i: (bb, i, 0, 0)),
            pl.BlockSpec((1, Q_BLOCK, 6 * LANES), lambda bb, i: (bb, i, 0)),
            pl.BlockSpec((1, s, IDX_DIM), lambda bb, i: (bb, 0, 0)),
            pl.BlockSpec((1, s, DSA_QX), lambda bb, i: (bb, 0, 0)),
            pl.BlockSpec((1, s // kc, DSA_KV_RANK, kc), lambda bb, i: (bb, 0, 0, 0)),
            pl.BlockSpec((DSA_HEADS, DSA_QPAD, DSA_QX), lambda bb, i: (0, 0, 0)),
            pl.BlockSpec((DSA_HEADS, DSA_KV_RANK, DSA_V_DIM), lambda bb, i: (0, 0, 0)),
        ],
        out_specs=pl.BlockSpec((1, Q_BLOCK, DSA_HEADS * DSA_V_DIM), lambda bb, i: (bb, i, 0)),
        out_shape=jax.ShapeDtypeStruct((b, s, DSA_HEADS * DSA_V_DIM), BF16),
        scratch_shapes=[
            pltpu.VMEM((nrow, DSA_QX), BF16),
            pltpu.VMEM((IDX_HEADS * Q_BLOCK, IDX_DIM), BF16),
            pltpu.VMEM((s // kc, kc, Q_BLOCK), F32),
            pltpu.VMEM((1, nrow), F32),
            pltpu.VMEM((1, nrow), F32),
            pltpu.VMEM((DSA_KV_RANK, nrow), F32),
        ],
        compiler_params=_params("arbitrary", "arbitrary"),
        name="dsa_attn",
    )(qall, qall, wi_l, tabs, kidx, kext, ct, wk, wuv)


def _dsa_mixer(h2, bsz, seq, positions, attn_norm, w_in, q_norm, kv_norm, idx_ln_g, idx_ln_b,
               w_qup, w_uk, w_uv, w_out, *, kc=512):
    t = h2.shape[0]
    kc = min(kc, seq)
    w_in_p = jnp.pad(w_in, ((0, 0), (0, 1024 - DSA_IN))).astype(BF16)
    proj = _mm(h2, w_in_p, gain=attn_norm, tn=1024)
    o0 = DSA_Q_RANK
    ckv = proj[:, o0:o0 + DSA_KV_RANK]
    o1 = o0 + DSA_KV_RANK
    k_pe = proj[:, o1:o1 + DSA_ROPE]
    o2 = o1 + DSA_ROPE
    k_idx = proj[:, o2:o2 + IDX_DIM]
    o3 = o2 + IDX_DIM
    w_idx = proj[:, o3:o3 + IDX_HEADS] * (IDX_HEADS ** -0.5 * IDX_DIM ** -0.5)

    nq = DSA_HEADS * DSA_QK_DIM
    w_q = w_qup[:, :nq].reshape(DSA_Q_RANK, DSA_HEADS, DSA_QK_DIM)
    w_q = jnp.pad(w_q, ((0, 0), (0, 0), (0, DSA_QPAD - DSA_QK_DIM))).reshape(DSA_Q_RANK, DSA_HEADS * DSA_QPAD)
    w_qp = jnp.concatenate([w_q, w_qup[:, nq:]], axis=1).astype(BF16)
    qall = _mm(proj, w_qp, gain=q_norm, out_dtype=BF16, tn=2048)

    pos = positions.reshape(t)
    cos, sin = _rope_cos_sin(pos, DSA_ROPE)
    icos, isin = _rope_cos_sin(pos, IDX_ROPE)
    c_n = ckv * lax.rsqrt(jnp.mean(ckv * ckv, -1, keepdims=True) + NORM_EPS) * kv_norm
    kext = jnp.concatenate(
        [c_n, _rope_glue(k_pe, cos, sin), jnp.zeros((t, DSA_QX - DSA_KV_RANK - DSA_ROPE), F32)], -1)
    mu = jnp.mean(k_idx, -1, keepdims=True)
    var = jnp.mean(jnp.square(k_idx - mu), -1, keepdims=True)
    k_ln = (k_idx - mu) * lax.rsqrt(var + NORM_EPS) * idx_ln_g + idx_ln_b
    kidx = _rope_glue(k_ln, icos, isin)
    tabs = jnp.concatenate([_rope_lane_tables(pos, DSA_ROPE), _rope_lane_tables(pos, IDX_ROPE)], -1)
    ct = _chunks_t(c_n.astype(BF16).reshape(bsz, seq, DSA_KV_RANK), kc)
    wi_l = w_idx.reshape(bsz, seq // Q_BLOCK, Q_BLOCK, IDX_HEADS).transpose(0, 1, 3, 2)
    wi_l = wi_l.reshape(bsz, seq // Q_BLOCK, 1, IDX_HEADS * Q_BLOCK)

    wk = jnp.zeros((DSA_HEADS, DSA_QPAD, DSA_QX), F32)
    wk = wk.at[:, DSA_ROPE:DSA_QK_DIM, :DSA_KV_RANK].set(jnp.swapaxes(w_uk, 1, 2))
    eye = jnp.eye(DSA_ROPE, dtype=F32)
    wk = wk.at[:, :DSA_ROPE, DSA_KV_RANK:DSA_KV_RANK + DSA_ROPE].set(jnp.broadcast_to(eye, (DSA_HEADS,) + eye.shape))

    o = _dsa_attention(
        qall.reshape(bsz, seq, -1), wi_l, tabs.reshape(bsz, seq, -1),
        kidx.astype(BF16).reshape(bsz, seq, IDX_DIM), kext.astype(BF16).reshape(bsz, seq, DSA_QX), ct,
        wk.astype(BF16), w_uv.astype(BF16), kc=kc)
    return _mm(o.reshape(t, -1), w_out.astype(BF16), res=h2)


FOX_KX = MXU_DIM


def _fox_body(q_ref, kx_ref, vt_ref, o_ref, qx_ref, m_ref, l_ref, acc_ref, *, tq, kc):
    i = pl.program_id(2)
    q0 = i * tq
    n_full = q0 // kc
    n_diag = tq // kc
    lane = lax.broadcasted_iota(jnp.int32, (tq, FOX_KX - FOX_DIM), 1)
    qx_ref[:, :FOX_DIM] = q_ref[0]
    qx_ref[:, FOX_DIM:] = jnp.where(lane < 3, 1.0, 0.0).astype(BF16)
    qx = qx_ref[...]
    m_ref[...] = jnp.full_like(m_ref, MASK_NEG)
    l_ref[...] = jnp.zeros_like(l_ref)
    acc_ref[...] = jnp.zeros_like(acc_ref)
    key_i = lax.broadcasted_iota(jnp.int32, (kc, tq), 0)
    t_l = q0 + lax.broadcasted_iota(jnp.int32, (kc, tq), 1)
    cols = slice(0, tq)

    def step(c, masked):
        k0 = pl.multiple_of(c * kc, kc)
        s = _nt(kx_ref[0, 0, pl.ds(k0, kc), :], qx)
        if masked:
            s = jnp.where((k0 + key_i) <= t_l, s, MASK_NEG)
        _online_softmax_step(s, vt_ref[0, 0, c], m_ref, l_ref, acc_ref, cols)

    def full_chunk(c, _):
        step(c, False)
        return 0

    def diag_chunk(c, _):
        step(c, True)
        return 0

    lax.fori_loop(0, n_full, full_chunk, 0)
    lax.fori_loop(n_full, n_full + n_diag, diag_chunk, 0)
    o_ref[0] = (acc_ref[...] / l_ref[...]).T.astype(o_ref.dtype)


def _fox_attention(qkv, kx, vt, *, tq, kc):
    b, s, _ = qkv.shape
    assert s % tq == 0 and tq % kc == 0
    hh = FOX_HEADS
    return pl.pallas_call(
        functools.partial(_fox_body, tq=tq, kc=kc),
        grid=(b, hh, s // tq),
        in_specs=[
            pl.BlockSpec((1, tq, FOX_DIM), lambda bb, h, i: (bb, i, h)),
            pl.BlockSpec((1, 1, s, FOX_KX), lambda bb, h, i: (bb, h, 0, 0)),
            pl.BlockSpec((1, 1, s // kc, FOX_DIM, kc), lambda bb, h, i: (bb, h, 0, 0, 0)),
        ],
        out_specs=pl.BlockSpec((1, tq, FOX_DIM), lambda bb, h, i: (bb, i, h)),
        out_shape=jax.ShapeDtypeStruct((b, s, hh * FOX_DIM), BF16),
        scratch_shapes=[pltpu.VMEM((tq, FOX_KX), BF16), pltpu.VMEM((1, tq), F32), pltpu.VMEM((1, tq), F32),
                        pltpu.VMEM((FOX_DIM, tq), F32)],
        compiler_params=_params("arbitrary", "arbitrary", "arbitrary"),
        name="fox_attn",
    )(qkv, kx, vt)


def _fox_mixer(h2, bsz, seq, attn_norm, w_in, b_f, w_out, *, tq=512, kc=512):
    t = h2.shape[0]
    hd = FOX_HEADS * FOX_DIM
    tq = min(tq, seq)
    kc = min(kc, tq)
    scale = FOX_DIM ** -0.5
    w_qkv = jnp.concatenate([w_in[:, :hd] * scale, w_in[:, hd:3 * hd]], axis=1).astype(BF16)
    qkv = _mm(h2, w_qkv, gain=attn_norm, out_dtype=BF16, tn=1024)
    w_f = jnp.pad(w_in[:, 3 * hd:], ((0, 0), (0, LANES - FOX_HEADS))).astype(BF16)
    f_logit = _mm(h2, w_f, gain=attn_norm)[:, :FOX_HEADS]
    log_f = jax.nn.log_sigmoid(f_logit + b_f)
    cum = jnp.cumsum(log_f.reshape(bsz, seq, FOX_HEADS), axis=1)
    c_hi, c_mid, c_lo = _split3_bits(-cum)
    qkv4 = qkv.reshape(bsz, seq, 3, FOX_HEADS, FOX_DIM)
    kx = jnp.concatenate(
        [qkv4[:, :, 1], jnp.stack([c_hi, c_mid, c_lo], -1),
         jnp.zeros((bsz, seq, FOX_HEADS, FOX_KX - FOX_DIM - 3), BF16)], -1).transpose(0, 2, 1, 3)
    vt = _chunks_t(qkv4[:, :, 2].transpose(0, 2, 1, 3), kc)
    o = _fox_attention(qkv.reshape(bsz, seq, 3 * hd), kx, vt, tq=tq, kc=kc)
    return _mm(o.reshape(t, hd), w_out.astype(BF16), res=h2)


def _cmp_body(x_ref, pe_ref, w1_ref, b1_ref, w2_ref, o_ref, *, half):
    x = x_ref[0, 0]
    w_lo = w1_ref[:half, :]
    w_hi = w1_ref[half:, :]
    a = jnp.dot(x, w_lo, preferred_element_type=F32)
    bnext = jnp.dot(x, w_hi, preferred_element_type=F32)
    n = x.shape[0]
    bnext = pltpu.roll(bnext, n - 1, 0)
    pe = pe_ref[...]
    pe_b = (jnp.dot(pe[:, :half], w_lo, preferred_element_type=F32)
            + jnp.dot(pe[:, half:], w_hi, preferred_element_type=F32))[0:1, :]
    hid = jax.nn.gelu(a + bnext + pe_b + b1_ref[...])
    o_ref[0, 0] = jnp.dot(hid.astype(BF16), w2_ref[...], preferred_element_type=F32).astype(o_ref.dtype)


def _compress(x, pe, w1, b1, w2):
    b, g, n, kd = x.shape
    dout = w2.shape[1]
    pe8 = jnp.broadcast_to(pe.reshape(1, 2 * kd), (8, 2 * kd)).astype(BF16)
    return pl.pallas_call(
        functools.partial(_cmp_body, half=kd),
        grid=(b, g),
        in_specs=[
            pl.BlockSpec((1, 1, n, kd), lambda bb, gg: (bb, gg, 0, 0)),
            pl.BlockSpec((8, 2 * kd), lambda bb, gg: (0, 0)),
            pl.BlockSpec((2 * kd, CMP_HIDDEN), lambda bb, gg: (0, 0)),
            pl.BlockSpec((1, CMP_HIDDEN), lambda bb, gg: (0, 0)),
            pl.BlockSpec((CMP_HIDDEN, dout), lambda bb, gg: (0, 0)),
        ],
        out_specs=pl.BlockSpec((1, 1, n, dout), lambda bb, gg: (bb, gg, 0, 0)),
        out_shape=jax.ShapeDtypeStruct((b, g, n, dout), BF16),
        compiler_params=_params("arbitrary", "arbitrary"),
        name="nsa_compress",
    )(x, pe8, w1.astype(BF16), b1.reshape(1, CMP_HIDDEN), w2.astype(BF16))


def _nsa_body(q_ref, tab_ref, gate_ref, kc_ref, vct_ref, ks_ref, vst_ref, kw_ref, vwt_ref, ov_ref, exp_ref,
              o_ref, qs_ref, m_ref, l_ref, acc_ref, *, kc, n_cmp, n_slc, n_sel):
    i = pl.program_id(2)
    q0 = i * Q_BLOCK
    nc = (q0 + Q_BLOCK + kc - 1) // kc
    jh = NSA_HPG
    scale = NSA_QK_DIM ** -0.5
    tab = tab_ref[0]
    cf, sa, sb = tab[:, 0:LANES], tab[:, LANES:2 * LANES], tab[:, 2 * LANES:3 * LANES]

    for j in range(jh):
        xj = q_ref[0, :, j * NSA_QK_DIM:(j + 1) * NSA_QK_DIM].astype(F32)
        r0 = _rope_first_vreg(xj[:, :LANES], cf, sa, sb, NSA_ROPE // 2)
        qs_ref[j * Q_BLOCK:(j + 1) * Q_BLOCK, :] = (jnp.concatenate([r0, xj[:, LANES:]], axis=1) * scale).astype(BF16)
    qs = qs_ref[...]

    n_id = lax.broadcasted_iota(jnp.int32, (n_cmp, Q_BLOCK), 0)
    t_c = q0 + lax.broadcasted_iota(jnp.int32, (n_cmp, Q_BLOCK), 1)
    cbias = jnp.where((n_id * CMP_STRIDE + (CMP_BLOCK - 1)) <= t_c, 0.0, MASK_NEG)
    pc = _softmax_cols(_nt(kc_ref[0, 0], qs) + jnp.tile(cbias, (1, jh)))
    o_c = jnp.dot(vct_ref[0, 0], pc.astype(BF16), preferred_element_type=F32)

    pcs = pc[:, 0:Q_BLOCK]
    for j in range(1, jh):
        pcs = pcs + pc[:, j * Q_BLOCK:(j + 1) * Q_BLOCK]
    ov = ov_ref[...]
    imp = sum(jnp.dot(ov, term, preferred_element_type=F32) for term in _split3(pcs))

    blk = lax.broadcasted_iota(jnp.int32, (n_slc, Q_BLOCK), 0)
    t_b = q0 + lax.broadcasted_iota(jnp.int32, (n_slc, Q_BLOCK), 1)
    cur = lax.shift_right_logical(t_b, int(np.log2(SLC_BLOCK)))
    causal_blk = blk * SLC_BLOCK <= t_b
    forced = (blk == 0) | (blk == cur) | (blk == cur - 1)
    val = jnp.where(causal_blk, jnp.where(forced, jnp.inf, imp), -jnp.inf)
    rank = jnp.zeros((n_slc, Q_BLOCK), F32)
    for mp in range(n_slc):
        vrow = val[mp:mp + 1, :]
        before = (vrow > val) | ((vrow == val) & (blk > mp))
        rank = rank + jnp.where(before, 1.0, 0.0)
    sel = jnp.where((rank < n_sel) & causal_blk, 1.0, 0.0)
    sel_p = jnp.concatenate([sel, jnp.zeros((LANES - n_slc, Q_BLOCK), F32)], axis=0).astype(BF16)

    m_ref[...] = jnp.full_like(m_ref, MASK_NEG)
    l_ref[...] = jnp.zeros_like(l_ref)
    acc_ref[...] = jnp.zeros_like(acc_ref)
    key_i = lax.broadcasted_iota(jnp.int32, (kc, Q_BLOCK), 0)
    t_k = q0 + lax.broadcasted_iota(jnp.int32, (kc, Q_BLOCK), 1)
    cols = slice(0, jh * Q_BLOCK)

    def sel_chunk(c, _):
        k0 = pl.multiple_of(c * kc, kc)
        hit = jnp.dot(exp_ref[pl.ds(k0, kc), :], sel_p, preferred_element_type=F32)
        bias = jnp.where((hit > 0.5) & ((k0 + key_i) <= t_k), 0.0, MASK_NEG)
        s = _nt(ks_ref[0, 0, pl.ds(k0, kc), :], qs) + jnp.tile(bias, (1, jh))
        _online_softmax_step(s, vst_ref[0, 0, c], m_ref, l_ref, acc_ref, cols)
        return 0

    lax.fori_loop(0, nc, sel_chunk, 0)
    o_s = acc_ref[...] / l_ref[...]

    wl = WINDOW + Q_BLOCK
    kstart = pl.multiple_of(q0, Q_BLOCK)
    s_pos = q0 - WINDOW + lax.broadcasted_iota(jnp.int32, (wl, Q_BLOCK), 0)
    t_w = q0 + lax.broadcasted_iota(jnp.int32, (wl, Q_BLOCK), 1)
    wbias = jnp.where((s_pos >= 0) & (s_pos <= t_w) & (s_pos > t_w - WINDOW), 0.0, MASK_NEG)
    pw = _softmax_cols(_nt(kw_ref[0, 0, pl.ds(kstart, wl), :], qs) + jnp.tile(wbias, (1, jh))).astype(BF16)
    o_w = jnp.dot(vwt_ref[0, 0, i], pw[0:Q_BLOCK, :], preferred_element_type=F32)
    for c in range(1, wl // Q_BLOCK):
        o_w = o_w + jnp.dot(vwt_ref[0, 0, i + c], pw[c * Q_BLOCK:(c + 1) * Q_BLOCK, :],
                            preferred_element_type=F32)

    g = gate_ref[0, 0]
    for j in range(jh):
        cj = slice(j * Q_BLOCK, (j + 1) * Q_BLOCK)
        out = (g[j:j + 1, :] * o_c[:, cj] + g[jh + j:jh + j + 1, :] * o_s[:, cj]
               + g[2 * jh + j:2 * jh + j + 1, :] * o_w[:, cj])
        o_ref[0, :, j * NSA_V_DIM:(j + 1) * NSA_V_DIM] = out.T.astype(o_ref.dtype)


def _nsa_attention(q, tabs, gates, kcmp, vcmp_t, ks, vs_t, kw, vw_t, *, kc):
    b, s, _ = q.shape
    g = NSA_GROUPS
    assert s % kc == 0 and kc % SLC_BLOCK == 0
    n_cmp = kcmp.shape[2]
    n_slc = s // SLC_BLOCK
    assert n_slc <= LANES
    n_sel = min(SLC_TOPK, n_slc)
    cmp_start = np.arange(n_cmp) * CMP_STRIDE
    slc_start = np.arange(n_slc) * SLC_BLOCK
    ov = ((cmp_start[None, :] < slc_start[:, None] + SLC_BLOCK)
          & (cmp_start[None, :] + CMP_BLOCK > slc_start[:, None])).astype(np.float32)
    expand = (np.arange(s)[:, None] // SLC_BLOCK == np.arange(LANES)[None, :]).astype(np.float32)
    qw = NSA_HPG * NSA_QK_DIM
    ow = NSA_HPG * NSA_V_DIM
    nrow = NSA_HPG * Q_BLOCK
    wl = s + WINDOW
    return pl.pallas_call(
        functools.partial(_nsa_body, kc=kc, n_cmp=n_cmp, n_slc=n_slc, n_sel=float(n_sel)),
        grid=(b, g, s // Q_BLOCK),
        in_specs=[
            pl.BlockSpec((1, Q_BLOCK, qw), lambda bb, gg, i: (bb, i, gg)),
            pl.BlockSpec((1, Q_BLOCK, 3 * LANES), lambda bb, gg, i: (bb, i, 0)),
            pl.BlockSpec((1, 1, 3 * NSA_HPG, Q_BLOCK), lambda bb, gg, i: (bb, gg, 0, i)),
            pl.BlockSpec((1, 1, n_cmp, NSA_QK_DIM), lambda bb, gg, i: (bb, gg, 0, 0)),
            pl.BlockSpec((1, 1, NSA_V_DIM, n_cmp), lambda bb, gg, i: (bb, gg, 0, 0)),
            pl.BlockSpec((1, 1, s, NSA_QK_DIM), lambda bb, gg, i: (bb, gg, 0, 0)),
            pl.BlockSpec((1, 1, s // kc, NSA_V_DIM, kc), lambda bb, gg, i: (bb, gg, 0, 0, 0)),
            pl.BlockSpec((1, 1, wl, NSA_QK_DIM), lambda bb, gg, i: (bb, gg, 0, 0)),
            pl.BlockSpec((1, 1, wl // Q_BLOCK, NSA_V_DIM, Q_BLOCK), lambda bb, gg, i: (bb, gg, 0, 0, 0)),
            pl.BlockSpec((n_slc, n_cmp), lambda bb, gg, i: (0, 0)),
            pl.BlockSpec((s, LANES), lambda bb, gg, i: (0, 0)),
        ],
        out_specs=pl.BlockSpec((1, Q_BLOCK, ow), lambda bb, gg, i: (bb, i, gg)),
        out_shape=jax.ShapeDtypeStruct((b, s, NSA_HEADS * NSA_V_DIM), BF16),
        scratch_shapes=[
            pltpu.VMEM((nrow, NSA_QK_DIM), BF16),
            pltpu.VMEM((1, nrow), F32),
            pltpu.VMEM((1, nrow), F32),
            pltpu.VMEM((NSA_V_DIM, nrow), F32),
        ],
        compiler_params=_params("arbitrary", "arbitrary", "arbitrary"),
        name="nsa_attn",
    )(q, tabs, gates, kcmp, vcmp_t, ks, vs_t, kw, vw_t, jnp.asarray(ov, BF16), jnp.asarray(expand, BF16))


def _nsa_mixer(h2, bsz, seq, positions, attn_norm, w_in, k_pe, k_w1, k_b1, k_w2, v_pe, v_w1, v_b1, v_w2, w_out,
               *, kc=512):
    t = h2.shape[0]
    kc = min(kc, seq)
    g, jh, dk, dv = NSA_GROUPS, NSA_HPG, NSA_QK_DIM, NSA_V_DIM
    nq = NSA_HEADS * dk
    n_rest = w_in.shape[1] - nq
    n_rest_p = -(-n_rest // 512) * 512
    q = _mm(h2, w_in[:, :nq].astype(BF16), gain=attn_norm, out_dtype=BF16, tm=1024, tn=1024)
    w_rest = jnp.pad(w_in[:, nq:], ((0, 0), (0, n_rest_p - n_rest))).astype(BF16)
    rest = _mm(h2, w_rest, gain=attn_norm, tn=512)
    offs = np.cumsum([0, NSA_KD, NSA_VD, NSA_KD, NSA_VD, NSA_KD, NSA_VD, 3 * NSA_HEADS])
    k_c, v_c, k_s, v_s, k_w, v_w, g_logit = [rest[:, int(a):int(b)] for a, b in zip(offs[:-1], offs[1:])]

    pos = positions.reshape(t)
    cos, sin = _rope_cos_sin(pos, NSA_ROPE)

    def rope_k(k):
        return _rope_glue(k.reshape(t, g, dk), cos[:, None, :], sin[:, None, :])

    def per_group(x, d):
        return x.reshape(bsz, seq, g, d).transpose(0, 2, 1, 3).astype(BF16)

    def chunks(x, d):
        x = x.reshape(bsz, seq // CMP_STRIDE, CMP_STRIDE, g, d).transpose(0, 3, 1, 2, 4)
        return x.reshape(bsz, g, seq // CMP_STRIDE, CMP_STRIDE * d).astype(BF16)

    kcmp = _compress(chunks(rope_k(k_c), dk), k_pe, k_w1, k_b1, k_w2)
    vcmp_t = jnp.swapaxes(_compress(chunks(v_c.reshape(t, g, dv), dv), v_pe, v_w1, v_b1, v_w2), -1, -2)
    ks = per_group(rope_k(k_s), dk)
    vs_t = _chunks_t(per_group(v_s.reshape(t, g, dv), dv), kc)
    pad = ((0, 0), (0, 0), (WINDOW, 0), (0, 0))
    kw = jnp.pad(per_group(rope_k(k_w), dk), pad)
    vw_t = _chunks_t(jnp.pad(per_group(v_w.reshape(t, g, dv), dv), pad), Q_BLOCK)
    gates = jax.nn.sigmoid(g_logit).reshape(bsz, seq, 3, g, jh).transpose(0, 3, 2, 4, 1).reshape(bsz, g, 3 * jh, seq)
    tabs = _rope_lane_tables(pos, NSA_ROPE).reshape(bsz, seq, 3 * LANES)

    o = _nsa_attention(q.reshape(bsz, seq, nq), tabs, gates, kcmp, vcmp_t, ks, vs_t, kw, vw_t, kc=kc)
    return _mm(o.reshape(t, NSA_HEADS * dv), w_out.astype(BF16), res=h2)


def _norm_body(x_ref, g_ref, o_ref):
    o_ref[...] = _rms(x_ref[...], g_ref[...])


def _final_norm(h2, gain, *, tm=512):
    t, d = h2.shape
    tm = min(tm, t)
    return pl.pallas_call(
        _norm_body,
        grid=(t // tm,),
        in_specs=[pl.BlockSpec((tm, d), lambda i: (i, 0)), pl.BlockSpec((1, d), lambda i: (0, 0))],
        out_specs=pl.BlockSpec((tm, d), lambda i: (i, 0)),
        out_shape=jax.ShapeDtypeStruct((t, d), F32),
        compiler_params=_params("arbitrary"),
        name="final_norm",
    )(h2, gain.reshape(1, d))


def _ffn_layer(h2, seq, ffn_norm, w_up, conv_w, conv_b, w_down):
    return _ffn(h2, seq, ffn_norm, w_up.astype(BF16), conv_w, conv_b, w_down.astype(BF16))


def kernel(x, positions, l0_attn_norm, l0_dsa_w_in, l0_dsa_q_norm, l0_dsa_kv_norm, l0_dsa_idx_ln_g, l0_dsa_idx_ln_b, l0_dsa_w_qup, l0_dsa_w_uk, l0_dsa_w_uv, l0_dsa_w_out, l0_ffn_norm, l0_ffn_up, l0_ffn_conv_w, l0_ffn_conv_b, l0_ffn_down, l1_attn_norm, l1_fox_w_in, l1_fox_b_f, l1_fox_w_out, l1_ffn_norm, l1_ffn_up, l1_ffn_conv_w, l1_ffn_conv_b, l1_ffn_down, l2_attn_norm, l2_nsa_w_in, l2_nsa_k_pe, l2_nsa_k_w1, l2_nsa_k_b1, l2_nsa_k_w2, l2_nsa_v_pe, l2_nsa_v_w1, l2_nsa_v_b1, l2_nsa_v_w2, l2_nsa_w_out, l2_ffn_norm, l2_ffn_up, l2_ffn_conv_w, l2_ffn_conv_b, l2_ffn_down, l3_attn_norm, l3_dsa_w_in, l3_dsa_q_norm, l3_dsa_kv_norm, l3_dsa_idx_ln_g, l3_dsa_idx_ln_b, l3_dsa_w_qup, l3_dsa_w_uk, l3_dsa_w_uv, l3_dsa_w_out, l3_ffn_norm, l3_ffn_up, l3_ffn_conv_w, l3_ffn_conv_b, l3_ffn_down, final_norm):
    bsz, seq, d = x.shape
    h = x.reshape(bsz * seq, d)
    h = _dsa_mixer(h, bsz, seq, positions, l0_attn_norm, l0_dsa_w_in, l0_dsa_q_norm, l0_dsa_kv_norm,
                   l0_dsa_idx_ln_g, l0_dsa_idx_ln_b, l0_dsa_w_qup, l0_dsa_w_uk, l0_dsa_w_uv, l0_dsa_w_out)
    h = _ffn_layer(h, seq, l0_ffn_norm, l0_ffn_up, l0_ffn_conv_w, l0_ffn_conv_b, l0_ffn_down)
    h = _fox_mixer(h, bsz, seq, l1_attn_norm, l1_fox_w_in, l1_fox_b_f, l1_fox_w_out)
    h = _ffn_layer(h, seq, l1_ffn_norm, l1_ffn_up, l1_ffn_conv_w, l1_ffn_conv_b, l1_ffn_down)
    h = _nsa_mixer(h, bsz, seq, positions, l2_attn_norm, l2_nsa_w_in, l2_nsa_k_pe, l2_nsa_k_w1, l2_nsa_k_b1,
                   l2_nsa_k_w2, l2_nsa_v_pe, l2_nsa_v_w1, l2_nsa_v_b1, l2_nsa_v_w2, l2_nsa_w_out)
    h = _ffn_layer(h, seq, l2_ffn_norm, l2_ffn_up, l2_ffn_conv_w, l2_ffn_conv_b, l2_ffn_down)
    h = _dsa_mixer(h, bsz, seq, positions, l3_attn_norm, l3_dsa_w_in, l3_dsa_q_norm, l3_dsa_kv_norm,
                   l3_dsa_idx_ln_g, l3_dsa_idx_ln_b, l3_dsa_w_qup, l3_dsa_w_uk, l3_dsa_w_uv, l3_dsa_w_out)
    h = _ffn_layer(h, seq, l3_ffn_norm, l3_ffn_up, l3_ffn_conv_w, l3_ffn_conv_b, l3_ffn_down)
    return _final_norm(h, final_norm).reshape(bsz, seq, d)
```

```python
import functools

import jax
import jax.numpy as jnp
import numpy as np
from jax import lax
from jax.experimental import pallas as pl
from jax.experimental.pallas import tpu as pltpu

F32 = jnp.float32
BF16 = jnp.bfloat16

ROPE_THETA = 500000.0
NORM_EPS = 1e-6
Q_BLOCK = 128

DSA_HEADS = 32
DSA_Q_RANK = 512
DSA_KV_RANK = 256
DSA_QK_DIM = 192
DSA_ROPE = 48
DSA_NOPE = DSA_QK_DIM - DSA_ROPE
DSA_V_DIM = 128
IDX_HEADS = 16
IDX_DIM = 128
IDX_ROPE = 32
DSA_TOPK_MAX = 256
DSA_IN = DSA_Q_RANK + DSA_KV_RANK + DSA_ROPE + IDX_DIM + IDX_HEADS

FOX_HEADS = 16
FOX_DIM = 128

NSA_HEADS = 48
NSA_GROUPS = 4
NSA_HPG = NSA_HEADS // NSA_GROUPS
NSA_QK_DIM = 192
NSA_ROPE = 48
NSA_V_DIM = 128
CMP_BLOCK = 32
CMP_STRIDE = 16
CMP_HIDDEN = 256
SLC_BLOCK = 64
SLC_TOPK = 16
WINDOW = 512
NSA_KD = NSA_GROUPS * NSA_QK_DIM
NSA_VD = NSA_GROUPS * NSA_V_DIM

CONV_WIDTH = 3

LANES = 128
SUBLANES = 8
BF16_SUBLANES = 16
MXU_DIM = 256
VMEM_LIMIT = 56 * 1024 * 1024

MASK_NEG = -1e30
LOG2E = 1.4426950408889634
BISECT_ITERS = 32


def _params(*sem):
    return pltpu.CompilerParams(dimension_semantics=sem, vmem_limit_bytes=VMEM_LIMIT)


def _nt(a, b):
    return lax.dot_general(a, b, (((1,), (1,)), ((), ())), preferred_element_type=F32)


def _tn(a, b):
    return lax.dot_general(a, b, (((0,), (0,)), ((), ())), preferred_element_type=F32)


def _rms(x, g):
    return x * lax.rsqrt(jnp.mean(x * x, axis=-1, keepdims=True) + NORM_EPS) * g


def _rope_first_vreg(x0, cf, sa, sb, half):
    return (x0 * cf + pltpu.roll(x0, LANES - half, 1) * sa + pltpu.roll(x0, half, 1) * sb)


def _split3(x):
    hi = x.astype(BF16)
    r1 = x - hi.astype(F32)
    mid = r1.astype(BF16)
    lo = (r1 - mid.astype(F32)).astype(BF16)
    return hi, mid, lo


def _split3_bits(x):
    def trunc(v):
        bits = lax.bitcast_convert_type(v, jnp.uint32) & jnp.uint32(0xFFFF0000)
        return lax.bitcast_convert_type(bits, F32)

    hi = trunc(x)
    r1 = x - hi
    mid = trunc(r1)
    lo = r1 - mid
    return hi.astype(BF16), mid.astype(BF16), lo.astype(BF16)


def _online_softmax_step(s, pv_lhs, m_ref, l_ref, acc_ref, cols):
    m_prev = m_ref[:, cols]
    m_new = jnp.maximum(m_prev, s.max(axis=0, keepdims=True))
    alpha = jnp.exp2(m_prev - m_new)
    p = jnp.exp2(s - m_new)
    l_ref[:, cols] = alpha * l_ref[:, cols] + p.sum(axis=0, keepdims=True)
    acc_ref[:, cols] = alpha * acc_ref[:, cols] + jnp.dot(pv_lhs, p.astype(BF16), preferred_element_type=F32)
    m_ref[:, cols] = m_new


def _softmax_cols(z):
    mx = z.max(axis=0, keepdims=True)
    e = jnp.exp2(z - mx)
    den = e.sum(axis=0, keepdims=True)
    return e * jnp.where(mx > 0.5 * MASK_NEG, 1.0 / den, 0.0)


def _mm_body(*refs, has_gain, has_res):
    x_ref, w_ref = refs[0], refs[1]
    k = 2
    g_ref = r_ref = None
    if has_gain:
        g_ref = refs[k]
        k += 1
    if has_res:
        r_ref = refs[k]
        k += 1
    o_ref = refs[k]
    if has_gain:
        xn_ref = refs[k + 1]

        @pl.when(pl.program_id(1) == 0)
        def _():
            xn_ref[...] = _rms(x_ref[...].astype(F32), g_ref[...]).astype(BF16)

        a = xn_ref[...]
    else:
        a = x_ref[...].astype(BF16)
    acc = jnp.dot(a, w_ref[...], preferred_element_type=F32)
    if has_res:
        acc = acc + r_ref[...]
    o_ref[...] = acc.astype(o_ref.dtype)


def _mm(x, w, *, gain=None, res=None, out_dtype=F32, x_col_block=0, tm=512, tn=512):
    m = x.shape[0]
    k, n = w.shape
    tm = min(tm, m)
    tn = min(tn, n)
    assert m % tm == 0 and n % tn == 0 and x.shape[1] % k == 0
    in_specs = [pl.BlockSpec((tm, k), lambda i, j: (i, x_col_block)),
                pl.BlockSpec((k, tn), lambda i, j: (0, j))]
    args = [x, w]
    scratch = []
    if gain is not None:
        in_specs.append(pl.BlockSpec((1, k), lambda i, j: (0, 0)))
        args.append(gain.reshape(1, k).astype(F32))
        scratch.append(pltpu.VMEM((tm, k), BF16))
    if res is not None:
        in_specs.append(pl.BlockSpec((tm, tn), lambda i, j: (i, j)))
        args.append(res)
    return pl.pallas_call(
        functools.partial(_mm_body, has_gain=gain is not None, has_res=res is not None),
        grid=(m // tm, n // tn),
        in_specs=in_specs,
        out_specs=pl.BlockSpec((tm, tn), lambda i, j: (i, j)),
        out_shape=jax.ShapeDtypeStruct((m, n), out_dtype),
        scratch_shapes=scratch,
        compiler_params=_params("arbitrary", "arbitrary"),
        name="mm",
    )(*args)


FFN_HALO = BF16_SUBLANES


def _ffn_body(h_ref, hp_ref, g_ref, wg_ref, wv_ref, cwg_ref, cwv_ref, cbg_ref, cbv_ref, wd_ref,
              o_ref, xn_ref, ug_ref, uv_ref, acc_ref, *, tm, seq):
    i = pl.program_id(0)
    j = pl.program_id(1)

    @pl.when(j == 0)
    def _():
        xn_ref[FFN_HALO:, :] = _rms(h_ref[...], g_ref[...]).astype(BF16)
        prev = _rms(hp_ref[...], g_ref[...])
        seq_start = (i * tm) % seq == 0
        xn_ref[:FFN_HALO, :] = jnp.where(seq_start, 0.0, prev).astype(BF16)
        acc_ref[...] = jnp.zeros_like(acc_ref)

    xn = xn_ref[...]
    ug_ref[...] = jnp.dot(xn, wg_ref[...], preferred_element_type=F32)
    uv_ref[...] = jnp.dot(xn, wv_ref[...], preferred_element_type=F32)

    def conv(u_ref, cw_ref, cb_ref):
        y = cb_ref[...]
        for t in range(CONV_WIDTH):
            off = FFN_HALO - (CONV_WIDTH - 1) + t
            y = y + cw_ref[t:t + 1, :] * u_ref[off:off + tm, :]
        return y

    yg = conv(ug_ref, cwg_ref, cbg_ref)
    yv = conv(uv_ref, cwv_ref, cbv_ref)
    a = (jax.nn.silu(yg) * yv).astype(BF16)
    acc_ref[...] += jnp.dot(a, wd_ref[...], preferred_element_type=F32)

    @pl.when(j == pl.num_programs(1) - 1)
    def _():
        o_ref[...] = h_ref[...] + acc_ref[...]


def _ffn(h, seq, gain, w_up, conv_w, conv_b, w_down, *, tm=512, tf=512):
    t, d = h.shape
    dff = w_down.shape[0]
    tm = min(tm, seq)
    assert seq % tm == 0 and dff % tf == 0 and tm % FFN_HALO == 0
    nf = dff // tf
    hb = tm // FFN_HALO
    conv_b2 = conv_b.reshape(1, 2 * dff)
    return pl.pallas_call(
        functools.partial(_ffn_body, tm=tm, seq=seq),
        grid=(t // tm, nf),
        in_specs=[
            pl.BlockSpec((tm, d), lambda i, j: (i, 0)),
            pl.BlockSpec((FFN_HALO, d), lambda i, j: (jnp.maximum(i * hb - 1, 0), 0)),
            pl.BlockSpec((1, d), lambda i, j: (0, 0)),
            pl.BlockSpec((d, tf), lambda i, j: (0, j)),
            pl.BlockSpec((d, tf), lambda i, j: (0, nf + j)),
            pl.BlockSpec((CONV_WIDTH, tf), lambda i, j: (0, j)),
            pl.BlockSpec((CONV_WIDTH, tf), lambda i, j: (0, nf + j)),
            pl.BlockSpec((1, tf), lambda i, j: (0, j)),
            pl.BlockSpec((1, tf), lambda i, j: (0, nf + j)),
            pl.BlockSpec((tf, d), lambda i, j: (j, 0)),
        ],
        out_specs=pl.BlockSpec((tm, d), lambda i, j: (i, 0)),
        out_shape=jax.ShapeDtypeStruct((t, d), F32),
        scratch_shapes=[
            pltpu.VMEM((tm + FFN_HALO, d), BF16),
            pltpu.VMEM((tm + FFN_HALO, tf), F32),
            pltpu.VMEM((tm + FFN_HALO, tf), F32),
            pltpu.VMEM((tm, d), F32),
        ],
        compiler_params=_params("arbitrary", "arbitrary"),
        name="ffn",
    )(h, h, gain.reshape(1, d), w_up, w_up, conv_w, conv_w, conv_b2, conv_b2, w_down)


def _rope_cos_sin(positions, rot):
    inv = ROPE_THETA ** (-jnp.arange(0, rot, 2, dtype=F32) / rot)
    ang = positions.astype(F32)[..., None] * inv
    return jnp.cos(ang), jnp.sin(ang)


def _rope_lane_tables(positions, rot):
    c, s = _rope_cos_sin(positions, rot)
    half = rot // 2
    shp = c.shape[:-1]
    cf = jnp.concatenate([c, c, jnp.ones(shp + (LANES - rot,), F32)], -1)
    sa = jnp.concatenate([-s, jnp.zeros(shp + (LANES - half,), F32)], -1)
    sb = jnp.concatenate([jnp.zeros(shp + (half,), F32), s, jnp.zeros(shp + (LANES - rot,), F32)], -1)
    return jnp.concatenate([cf, sa, sb], -1)


def _rope_glue(x, cos, sin):
    half = cos.shape[-1]
    x1, x2 = x[..., :half], x[..., half:2 * half]
    return jnp.concatenate([x1 * cos - x2 * sin, x1 * sin + x2 * cos, x[..., 2 * half:]], -1)


def _chunks_t(x, kc):
    s, d = x.shape[-2:]
    lead = x.shape[:-2]
    return jnp.swapaxes(x.reshape(lead + (s // kc, kc, d)), -1, -2)


DSA_QX = 384
DSA_QPAD = MXU_DIM
DSA_HEAD_GROUP = 8


def _dsa_body(q_ref, qi_ref, wi_ref, tab_ref, kidx_ref, kext_ref, ct_ref, wk_ref, wuv_ref, o_ref,
              qx_ref, qis_ref, sc_ref, m_ref, l_ref, acc_ref, *, kc, n_keep):
    i = pl.program_id(1)
    q0 = i * Q_BLOCK
    nc = (q0 + Q_BLOCK + kc - 1) // kc
    scale = DSA_QK_DIM ** -0.5 * LOG2E
    tab = tab_ref[0]
    cf, sa, sb = tab[:, 0:LANES], tab[:, LANES:2 * LANES], tab[:, 2 * LANES:3 * LANES]
    cfi, sai, sbi = tab[:, 3 * LANES:4 * LANES], tab[:, 4 * LANES:5 * LANES], tab[:, 5 * LANES:6 * LANES]

    for h in range(DSA_HEADS):
        xh = q_ref[0, :, h * DSA_QPAD:(h + 1) * DSA_QPAD].astype(F32)
        r0 = _rope_first_vreg(xh[:, :LANES], cf, sa, sb, DSA_ROPE // 2)
        qh = jnp.concatenate([r0, xh[:, LANES:]], axis=1).astype(BF16)
        qx = jnp.dot(qh, wk_ref[h], preferred_element_type=F32) * scale
        qx_ref[h * Q_BLOCK:(h + 1) * Q_BLOCK, :] = qx.astype(BF16)
    for h in range(IDX_HEADS):
        xi = qi_ref[0, :, h * IDX_DIM:(h + 1) * IDX_DIM].astype(F32)
        qis_ref[h * Q_BLOCK:(h + 1) * Q_BLOCK, :] = _rope_first_vreg(xi, cfi, sai, sbi, IDX_ROPE // 2).astype(BF16)

    wl = wi_ref[0, 0]
    key_i = lax.broadcasted_iota(jnp.int32, (kc, Q_BLOCK), 0)
    t_l = q0 + lax.broadcasted_iota(jnp.int32, (kc, Q_BLOCK), 1)

    def score_chunk(c, carry):
        lo, hi = carry
        k0 = pl.multiple_of(c * kc, kc)
        d = _nt(kidx_ref[0, pl.ds(k0, kc), :], qis_ref[...])
        d = jnp.maximum(d, 0.0) * wl
        sc = d[:, 0:Q_BLOCK]
        for h in range(1, IDX_HEADS):
            sc = sc + d[:, h * Q_BLOCK:(h + 1) * Q_BLOCK]
        sc = sc + 0.0
        causal = (k0 + key_i) <= t_l
        sc_ref[c] = jnp.where(causal, sc, -jnp.inf)
        lo = jnp.minimum(lo, jnp.where(causal, sc, jnp.inf).min(axis=0, keepdims=True))
        hi = jnp.maximum(hi, jnp.where(causal, sc, -jnp.inf).max(axis=0, keepdims=True))
        return lo, hi

    lo, hi = lax.fori_loop(0, nc, score_chunk,
                           (jnp.full((1, Q_BLOCK), jnp.inf, F32), jnp.full((1, Q_BLOCK), -jnp.inf, F32)))

    def bisect_more(carry):
        it, _, _, n_lo = carry
        return (it < BISECT_ITERS) & (jnp.max(n_lo) > n_keep)

    def bisect(carry):
        it, lo, hi, n_lo = carry
        mid = lo + (hi - lo) * 0.5

        def count_chunk(c, cnt):
            ge = jnp.where(sc_ref[c] >= mid, 1.0, 0.0)
            return cnt + ge.reshape(kc // SUBLANES, SUBLANES, Q_BLOCK).sum(axis=0)

        cnt = lax.fori_loop(0, nc, count_chunk, jnp.zeros((SUBLANES, Q_BLOCK), F32)).sum(axis=0, keepdims=True)
        ge = cnt >= n_keep
        return it + 1, jnp.where(ge, mid, lo), jnp.where(ge, hi, mid), jnp.where(ge, cnt, n_lo)

    n_causal = (q0 + 1 + lax.broadcasted_iota(jnp.int32, (1, Q_BLOCK), 1)).astype(F32)
    _, lo, hi, _ = lax.while_loop(bisect_more, bisect, (0, lo, hi, n_causal))

    def bias_chunk(c, _):
        sc_ref[c] = jnp.where(sc_ref[c] >= lo, 0.0, MASK_NEG)
        return 0

    lax.fori_loop(0, nc, bias_chunk, 0)

    m_ref[...] = jnp.full_like(m_ref, MASK_NEG)
    l_ref[...] = jnp.zeros_like(l_ref)
    acc_ref[...] = jnp.zeros_like(acc_ref)
    gl = DSA_HEAD_GROUP * Q_BLOCK

    def attn_chunk(c, _):
        k0 = pl.multiple_of(c * kc, kc)
        kx = kext_ref[0, pl.ds(k0, kc), :]
        ct = ct_ref[0, c]
        bias = jnp.tile(sc_ref[c], (1, DSA_HEAD_GROUP))
        n_slab = DSA_HEADS // DSA_HEAD_GROUP
        s = _nt(kx, qx_ref[0:gl, :]) + bias
        for g in range(n_slab):
            s_next = _nt(kx, qx_ref[(g + 1) * gl:(g + 2) * gl, :]) + bias if g + 1 < n_slab else None
            _online_softmax_step(s, ct, m_ref, l_ref, acc_ref, slice(g * gl, (g + 1) * gl))
            s = s_next
        return 0

    lax.fori_loop(0, nc, attn_chunk, 0)

    for h in range(DSA_HEADS):
        cols = slice(h * Q_BLOCK, (h + 1) * Q_BLOCK)
        o_lat = (acc_ref[:, cols] / l_ref[:, cols]).astype(BF16)
        o_ref[0, :, h * DSA_V_DIM:(h + 1) * DSA_V_DIM] = _tn(o_lat, wuv_ref[h]).astype(o_ref.dtype)


def _dsa_attention(qall, wi_l, tabs, kidx, kext, ct, wk, wuv, *, kc):
    b, s, _ = qall.shape
    assert s % kc == 0 and kc % Q_BLOCK == 0
    n_keep = min(DSA_TOPK_MAX, s // 4)
    nq = DSA_HEADS * DSA_QPAD
    ni = IDX_HEADS * IDX_DIM
    assert nq % ni == 0
    nrow = DSA_HEADS * Q_BLOCK
    return pl.pallas_call(
        functools.partial(_dsa_body, kc=kc, n_keep=float(n_keep)),
        grid=(b, s // Q_BLOCK),
        in_specs=[
            pl.BlockSpec((1, Q_BLOCK, nq), lambda bb, i: (bb, i, 0)),
            pl.BlockSpec((1, Q_BLOCK, ni), lambda bb, i: (bb, i, nq // ni)),
            pl.BlockSpec((1, 1, 1, IDX_HEADS * Q_BLOCK), lambda bb, i: (bb, i, 0, 0)),
            pl.BlockSpec((1, Q_BLOCK, 6 * LANES), lambda bb, i: (bb, i, 0)),
            pl.BlockSpec((1, s, IDX_DIM), lambda bb, i: (bb, 0, 0)),
            pl.BlockSpec((1, s, DSA_QX), lambda bb, i: (bb, 0, 0)),
            pl.BlockSpec((1, s // kc, DSA_KV_RANK, kc), lambda bb, i: (bb, 0, 0, 0)),
            pl.BlockSpec((DSA_HEADS, DSA_QPAD, DSA_QX), lambda bb, i: (0, 0, 0)),
            pl.BlockSpec((DSA_HEADS, DSA_KV_RANK, DSA_V_DIM), lambda bb, i: (0, 0, 0)),
        ],
        out_specs=pl.BlockSpec((1, Q_BLOCK, DSA_HEADS * DSA_V_DIM), lambda bb, i: (bb, i, 0)),
        out_shape=jax.ShapeDtypeStruct((b, s, DSA_HEADS * DSA_V_DIM), BF16),
        scratch_shapes=[
            pltpu.VMEM((nrow, DSA_QX), BF16),
            pltpu.VMEM((IDX_HEADS * Q_BLOCK, IDX_DIM), BF16),
            pltpu.VMEM((s // kc, kc, Q_BLOCK), F32),
            pltpu.VMEM((1, nrow), F32),
            pltpu.VMEM((1, nrow), F32),
            pltpu.VMEM((DSA_KV_RANK, nrow), F32),
        ],
        compiler_params=_params("arbitrary", "arbitrary"),
        name="dsa_attn",
    )(qall, qall, wi_l, tabs, kidx, kext, ct, wk, wuv)


def _dsa_mixer(h2, bsz, seq, positions, attn_norm, w_in, q_norm, kv_norm, idx_ln_g, idx_ln_b,
               w_qup, w_uk, w_uv, w_out, *, kc=512):
    t = h2.shape[0]
    kc = min(kc, seq)
    w_in_p = jnp.pad(w_in, ((0, 0), (0, 1024 - DSA_IN))).astype(BF16)
    proj = _mm(h2, w_in_p, gain=attn_norm, tn=1024)
    o0 = DSA_Q_RANK
    ckv = proj[:, o0:o0 + DSA_KV_RANK]
    o1 = o0 + DSA_KV_RANK
    k_pe = proj[:, o1:o1 + DSA_ROPE]
    o2 = o1 + DSA_ROPE
    k_idx = proj[:, o2:o2 + IDX_DIM]
    o3 = o2 + IDX_DIM
    w_idx = proj[:, o3:o3 + IDX_HEADS] * (IDX_HEADS ** -0.5 * IDX_DIM ** -0.5)

    nq = DSA_HEADS * DSA_QK_DIM
    w_q = w_qup[:, :nq].reshape(DSA_Q_RANK, DSA_HEADS, DSA_QK_DIM)
    w_q = jnp.pad(w_q, ((0, 0), (0, 0), (0, DSA_QPAD - DSA_QK_DIM))).reshape(DSA_Q_RANK, DSA_HEADS * DSA_QPAD)
    w_qp = jnp.concatenate([w_q, w_qup[:, nq:]], axis=1).astype(BF16)
    qall = _mm(proj, w_qp, gain=q_norm, out_dtype=BF16, tn=w_qp.shape[1])

    pos = positions.reshape(t)
    cos, sin = _rope_cos_sin(pos, DSA_ROPE)
    icos, isin = _rope_cos_sin(pos, IDX_ROPE)
    c_n = ckv * lax.rsqrt(jnp.mean(ckv * ckv, -1, keepdims=True) + NORM_EPS) * kv_norm
    kext = jnp.concatenate(
        [c_n, _rope_glue(k_pe, cos, sin), jnp.zeros((t, DSA_QX - DSA_KV_RANK - DSA_ROPE), F32)], -1)
    mu = jnp.mean(k_idx, -1, keepdims=True)
    var = jnp.mean(jnp.square(k_idx - mu), -1, keepdims=True)
    k_ln = (k_idx - mu) * lax.rsqrt(var + NORM_EPS) * idx_ln_g + idx_ln_b
    kidx = _rope_glue(k_ln, icos, isin)
    tabs = jnp.concatenate([_rope_lane_tables(pos, DSA_ROPE), _rope_lane_tables(pos, IDX_ROPE)], -1)
    ct = _chunks_t(c_n.astype(BF16).reshape(bsz, seq, DSA_KV_RANK), kc)
    wi_l = w_idx.reshape(bsz, seq // Q_BLOCK, Q_BLOCK, IDX_HEADS).transpose(0, 1, 3, 2)
    wi_l = wi_l.reshape(bsz, seq // Q_BLOCK, 1, IDX_HEADS * Q_BLOCK)

    wk = jnp.zeros((DSA_HEADS, DSA_QPAD, DSA_QX), F32)
    wk = wk.at[:, DSA_ROPE:DSA_QK_DIM, :DSA_KV_RANK].set(jnp.swapaxes(w_uk, 1, 2))
    eye = jnp.eye(DSA_ROPE, dtype=F32)
    wk = wk.at[:, :DSA_ROPE, DSA_KV_RANK:DSA_KV_RANK + DSA_ROPE].set(jnp.broadcast_to(eye, (DSA_HEADS,) + eye.shape))

    o = _dsa_attention(
        qall.reshape(bsz, seq, -1), wi_l, tabs.reshape(bsz, seq, -1),
        kidx.astype(BF16).reshape(bsz, seq, IDX_DIM), kext.astype(BF16).reshape(bsz, seq, DSA_QX), ct,
        wk.astype(BF16), w_uv.astype(BF16), kc=kc)
    return _mm(o.reshape(t, -1), w_out.astype(BF16), res=h2)


FOX_KX = MXU_DIM


def _fox_body(q_ref, kx_ref, vt_ref, o_ref, qx_ref, m_ref, l_ref, acc_ref, *, tq, kc):
    i = pl.program_id(2)
    q0 = i * tq
    n_full = q0 // kc
    n_diag = tq // kc
    lane = lax.broadcasted_iota(jnp.int32, (tq, FOX_KX - FOX_DIM), 1)
    qx_ref[:, :FOX_DIM] = q_ref[0]
    qx_ref[:, FOX_DIM:] = jnp.where(lane < 3, 1.0, 0.0).astype(BF16)
    qx = qx_ref[...]
    m_ref[...] = jnp.full_like(m_ref, MASK_NEG)
    l_ref[...] = jnp.zeros_like(l_ref)
    acc_ref[...] = jnp.zeros_like(acc_ref)
    key_i = lax.broadcasted_iota(jnp.int32, (kc, tq), 0)
    t_l = q0 + lax.broadcasted_iota(jnp.int32, (kc, tq), 1)
    cols = slice(0, tq)

    def step(c, masked):
        k0 = pl.multiple_of(c * kc, kc)
        s = _nt(kx_ref[0, 0, pl.ds(k0, kc), :], qx)
        if masked:
            s = jnp.where((k0 + key_i) <= t_l, s, MASK_NEG)
        _online_softmax_step(s, vt_ref[0, 0, c], m_ref, l_ref, acc_ref, cols)

    def full_chunk(c, _):
        step(c, False)
        return 0

    def diag_chunk(c, _):
        step(c, True)
        return 0

    lax.fori_loop(0, n_full, full_chunk, 0)
    lax.fori_loop(n_full, n_full + n_diag, diag_chunk, 0)
    o_ref[0] = (acc_ref[...] / l_ref[...]).T.astype(o_ref.dtype)


def _fox_attention(qkv, kx, vt, *, tq, kc):
    b, s, _ = qkv.shape
    assert s % tq == 0 and tq % kc == 0
    hh = FOX_HEADS
    return pl.pallas_call(
        functools.partial(_fox_body, tq=tq, kc=kc),
        grid=(b, hh, s // tq),
        in_specs=[
            pl.BlockSpec((1, tq, FOX_DIM), lambda bb, h, i: (bb, i, h)),
            pl.BlockSpec((1, 1, s, FOX_KX), lambda bb, h, i: (bb, h, 0, 0)),
            pl.BlockSpec((1, 1, s // kc, FOX_DIM, kc), lambda bb, h, i: (bb, h, 0, 0, 0)),
        ],
        out_specs=pl.BlockSpec((1, tq, FOX_DIM), lambda bb, h, i: (bb, i, h)),
        out_shape=jax.ShapeDtypeStruct((b, s, hh * FOX_DIM), BF16),
        scratch_shapes=[pltpu.VMEM((tq, FOX_KX), BF16), pltpu.VMEM((1, tq), F32), pltpu.VMEM((1, tq), F32),
                        pltpu.VMEM((FOX_DIM, tq), F32)],
        compiler_params=_params("arbitrary", "arbitrary", "arbitrary"),
        name="fox_attn",
    )(qkv, kx, vt)


def _fox_mixer(h2, bsz, seq, attn_norm, w_in, b_f, w_out, *, tq=512, kc=512):
    t = h2.shape[0]
    hd = FOX_HEADS * FOX_DIM
    tq = min(tq, seq)
    kc = min(kc, tq)
    scale = FOX_DIM ** -0.5 * LOG2E
    w_qkv = jnp.concatenate([w_in[:, :hd] * scale, w_in[:, hd:3 * hd]], axis=1).astype(BF16)
    qkv = _mm(h2, w_qkv, gain=attn_norm, out_dtype=BF16, tn=1024)
    w_f = jnp.pad(w_in[:, 3 * hd:], ((0, 0), (0, LANES - FOX_HEADS))).astype(BF16)
    f_logit = _mm(h2, w_f, gain=attn_norm)[:, :FOX_HEADS]
    log_f = jax.nn.log_sigmoid(f_logit + b_f)
    cum = jnp.cumsum(log_f.reshape(bsz, seq, FOX_HEADS), axis=1)
    c_hi, c_mid, c_lo = _split3_bits(-cum * LOG2E)
    qkv4 = qkv.reshape(bsz, seq, 3, FOX_HEADS, FOX_DIM)
    kx = jnp.concatenate(
        [qkv4[:, :, 1], jnp.stack([c_hi, c_mid, c_lo], -1),
         jnp.zeros((bsz, seq, FOX_HEADS, FOX_KX - FOX_DIM - 3), BF16)], -1).transpose(0, 2, 1, 3)
    vt = _chunks_t(qkv4[:, :, 2].transpose(0, 2, 1, 3), kc)
    o = _fox_attention(qkv.reshape(bsz, seq, 3 * hd), kx, vt, tq=tq, kc=kc)
    return _mm(o.reshape(t, hd), w_out.astype(BF16), res=h2)


NSA_HEAD_GROUP = 4


def _cmp_body(x_ref, pe_ref, w1_ref, b1_ref, w2_ref, o_ref, *, half):
    x = x_ref[0, 0]
    w_lo = w1_ref[:half, :]
    w_hi = w1_ref[half:, :]
    a = jnp.dot(x, w_lo, preferred_element_type=F32)
    bnext = jnp.dot(x, w_hi, preferred_element_type=F32)
    n = x.shape[0]
    bnext = pltpu.roll(bnext, n - 1, 0)
    pe = pe_ref[...]
    pe_b = (jnp.dot(pe[:, :half], w_lo, preferred_element_type=F32)
            + jnp.dot(pe[:, half:], w_hi, preferred_element_type=F32))[0:1, :]
    hid = jax.nn.gelu(a + bnext + pe_b + b1_ref[...])
    o_ref[0, 0] = jnp.dot(hid.astype(BF16), w2_ref[...], preferred_element_type=F32).astype(o_ref.dtype)


def _compress(x, pe, w1, b1, w2):
    b, g, n, kd = x.shape
    dout = w2.shape[1]
    pe8 = jnp.broadcast_to(pe.reshape(1, 2 * kd), (8, 2 * kd)).astype(BF16)
    return pl.pallas_call(
        functools.partial(_cmp_body, half=kd),
        grid=(b, g),
        in_specs=[
            pl.BlockSpec((1, 1, n, kd), lambda bb, gg: (bb, gg, 0, 0)),
            pl.BlockSpec((8, 2 * kd), lambda bb, gg: (0, 0)),
            pl.BlockSpec((2 * kd, CMP_HIDDEN), lambda bb, gg: (0, 0)),
            pl.BlockSpec((1, CMP_HIDDEN), lambda bb, gg: (0, 0)),
            pl.BlockSpec((CMP_HIDDEN, dout), lambda bb, gg: (0, 0)),
        ],
        out_specs=pl.BlockSpec((1, 1, n, dout), lambda bb, gg: (bb, gg, 0, 0)),
        out_shape=jax.ShapeDtypeStruct((b, g, n, dout), BF16),
        compiler_params=_params("arbitrary", "arbitrary"),
        name="nsa_compress",
    )(x, pe8, w1.astype(BF16), b1.reshape(1, CMP_HIDDEN), w2.astype(BF16))


def _nsa_body(q_ref, tab_ref, gate_ref, kc_ref, vct_ref, ks_ref, vst_ref, kw_ref, vwt_ref, ov_ref, exp_ref,
              o_ref, qs_ref, m_ref, l_ref, acc_ref, *, kc, n_cmp, n_slc, n_sel):
    i = pl.program_id(2)
    q0 = i * Q_BLOCK
    nc = (q0 + Q_BLOCK + kc - 1) // kc
    jh = NSA_HPG
    scale = NSA_QK_DIM ** -0.5 * LOG2E
    tab = tab_ref[0]
    cf, sa, sb = tab[:, 0:LANES], tab[:, LANES:2 * LANES], tab[:, 2 * LANES:3 * LANES]

    for j in range(jh):
        xj = q_ref[0, :, j * NSA_QK_DIM:(j + 1) * NSA_QK_DIM].astype(F32)
        r0 = _rope_first_vreg(xj[:, :LANES], cf, sa, sb, NSA_ROPE // 2)
        qs_ref[j * Q_BLOCK:(j + 1) * Q_BLOCK, :] = (jnp.concatenate([r0, xj[:, LANES:]], axis=1) * scale).astype(BF16)
    qs = qs_ref[...]

    n_id = lax.broadcasted_iota(jnp.int32, (n_cmp, Q_BLOCK), 0)
    t_c = q0 + lax.broadcasted_iota(jnp.int32, (n_cmp, Q_BLOCK), 1)
    cbias = jnp.where((n_id * CMP_STRIDE + (CMP_BLOCK - 1)) <= t_c, 0.0, MASK_NEG)
    pc = _softmax_cols(_nt(kc_ref[0, 0], qs) + jnp.tile(cbias, (1, jh)))
    o_c = jnp.dot(vct_ref[0, 0], pc.astype(BF16), preferred_element_type=F32)

    pcs = pc[:, 0:Q_BLOCK]
    for j in range(1, jh):
        pcs = pcs + pc[:, j * Q_BLOCK:(j + 1) * Q_BLOCK]
    ov = ov_ref[...]
    imp = sum(jnp.dot(ov, term, preferred_element_type=F32) for term in _split3(pcs))

    blk = lax.broadcasted_iota(jnp.int32, (n_slc, Q_BLOCK), 0)
    t_b = q0 + lax.broadcasted_iota(jnp.int32, (n_slc, Q_BLOCK), 1)
    cur = lax.shift_right_logical(t_b, int(np.log2(SLC_BLOCK)))
    causal_blk = blk * SLC_BLOCK <= t_b
    forced = (blk == 0) | (blk == cur) | (blk == cur - 1)
    val = jnp.where(causal_blk, jnp.where(forced, jnp.inf, imp), -jnp.inf)
    rank = jnp.zeros((n_slc, Q_BLOCK), F32)
    for mp in range(n_slc):
        vrow = val[mp:mp + 1, :]
        before = (vrow > val) | ((vrow == val) & (blk > mp))
        rank = rank + jnp.where(before, 1.0, 0.0)
    sel = jnp.where((rank < n_sel) & causal_blk, 1.0, 0.0)
    sel_p = jnp.concatenate([sel, jnp.zeros((LANES - n_slc, Q_BLOCK), F32)], axis=0).astype(BF16)

    m_ref[...] = jnp.full_like(m_ref, MASK_NEG)
    l_ref[...] = jnp.zeros_like(l_ref)
    acc_ref[...] = jnp.zeros_like(acc_ref)
    key_i = lax.broadcasted_iota(jnp.int32, (kc, Q_BLOCK), 0)
    t_k = q0 + lax.broadcasted_iota(jnp.int32, (kc, Q_BLOCK), 1)
    gl = NSA_HEAD_GROUP * Q_BLOCK

    def sel_chunk(c, _):
        k0 = pl.multiple_of(c * kc, kc)
        hit = jnp.dot(exp_ref[pl.ds(k0, kc), :], sel_p, preferred_element_type=F32)
        bias = jnp.where((hit > 0.5) & ((k0 + key_i) <= t_k), 0.0, MASK_NEG)
        bias = jnp.tile(bias, (1, NSA_HEAD_GROUP))
        ksc = ks_ref[0, 0, pl.ds(k0, kc), :]
        vt = vst_ref[0, 0, c]
        n_slab = jh // NSA_HEAD_GROUP
        s = _nt(ksc, qs_ref[0:gl, :]) + bias
        for hg in range(n_slab):
            s_next = _nt(ksc, qs_ref[(hg + 1) * gl:(hg + 2) * gl, :]) + bias if hg + 1 < n_slab else None
            _online_softmax_step(s, vt, m_ref, l_ref, acc_ref, slice(hg * gl, (hg + 1) * gl))
            s = s_next
        return 0

    lax.fori_loop(0, nc, sel_chunk, 0)
    o_s = acc_ref[...] / l_ref[...]

    wl = WINDOW + Q_BLOCK
    kstart = pl.multiple_of(q0, Q_BLOCK)
    s_pos = q0 - WINDOW + lax.broadcasted_iota(jnp.int32, (wl, Q_BLOCK), 0)
    t_w = q0 + lax.broadcasted_iota(jnp.int32, (wl, Q_BLOCK), 1)
    wbias = jnp.where((s_pos >= 0) & (s_pos <= t_w) & (s_pos > t_w - WINDOW), 0.0, MASK_NEG)
    pw = _softmax_cols(_nt(kw_ref[0, 0, pl.ds(kstart, wl), :], qs) + jnp.tile(wbias, (1, jh))).astype(BF16)
    o_w = jnp.dot(vwt_ref[0, 0, i], pw[0:Q_BLOCK, :], preferred_element_type=F32)
    for c in range(1, wl // Q_BLOCK):
        o_w = o_w + jnp.dot(vwt_ref[0, 0, i + c], pw[c * Q_BLOCK:(c + 1) * Q_BLOCK, :],
                            preferred_element_type=F32)

    g = gate_ref[0, 0]
    for j in range(jh):
        cj = slice(j * Q_BLOCK, (j + 1) * Q_BLOCK)
        out = (g[j:j + 1, :] * o_c[:, cj] + g[jh + j:jh + j + 1, :] * o_s[:, cj]
               + g[2 * jh + j:2 * jh + j + 1, :] * o_w[:, cj])
        o_ref[0, :, j * NSA_V_DIM:(j + 1) * NSA_V_DIM] = out.T.astype(o_ref.dtype)


def _nsa_attention(q, tabs, gates, kcmp, vcmp_t, ks, vs_t, kw, vw_t, *, kc):
    b, s, _ = q.shape
    g = NSA_GROUPS
    assert s % kc == 0 and kc % SLC_BLOCK == 0
    n_cmp = kcmp.shape[2]
    n_slc = s // SLC_BLOCK
    assert n_slc <= LANES
    n_sel = min(SLC_TOPK, n_slc)
    cmp_start = np.arange(n_cmp) * CMP_STRIDE
    slc_start = np.arange(n_slc) * SLC_BLOCK
    ov = ((cmp_start[None, :] < slc_start[:, None] + SLC_BLOCK)
          & (cmp_start[None, :] + CMP_BLOCK > slc_start[:, None])).astype(np.float32)
    expand = (np.arange(s)[:, None] // SLC_BLOCK == np.arange(LANES)[None, :]).astype(np.float32)
    qw = NSA_HPG * NSA_QK_DIM
    ow = NSA_HPG * NSA_V_DIM
    nrow = NSA_HPG * Q_BLOCK
    wl = s + WINDOW
    return pl.pallas_call(
        functools.partial(_nsa_body, kc=kc, n_cmp=n_cmp, n_slc=n_slc, n_sel=float(n_sel)),
        grid=(b, g, s // Q_BLOCK),
        in_specs=[
            pl.BlockSpec((1, Q_BLOCK, qw), lambda bb, gg, i: (bb, i, gg)),
            pl.BlockSpec((1, Q_BLOCK, 3 * LANES), lambda bb, gg, i: (bb, i, 0)),
            pl.BlockSpec((1, 1, 3 * NSA_HPG, Q_BLOCK), lambda bb, gg, i: (bb, gg, 0, i)),
            pl.BlockSpec((1, 1, n_cmp, NSA_QK_DIM), lambda bb, gg, i: (bb, gg, 0, 0)),
            pl.BlockSpec((1, 1, NSA_V_DIM, n_cmp), lambda bb, gg, i: (bb, gg, 0, 0)),
            pl.BlockSpec((1, 1, s, NSA_QK_DIM), lambda bb, gg, i: (bb, gg, 0, 0)),
            pl.BlockSpec((1, 1, s // kc, NSA_V_DIM, kc), lambda bb, gg, i: (bb, gg, 0, 0, 0)),
            pl.BlockSpec((1, 1, wl, NSA_QK_DIM), lambda bb, gg, i: (bb, gg, 0, 0)),
            pl.BlockSpec((1, 1, wl // Q_BLOCK, NSA_V_DIM, Q_BLOCK), lambda bb, gg, i: (bb, gg, 0, 0, 0)),
            pl.BlockSpec((n_slc, n_cmp), lambda bb, gg, i: (0, 0)),
            pl.BlockSpec((s, LANES), lambda bb, gg, i: (0, 0)),
        ],
        out_specs=pl.BlockSpec((1, Q_BLOCK, ow), lambda bb, gg, i: (bb, i, gg)),
        out_shape=jax.ShapeDtypeStruct((b, s, NSA_HEADS * NSA_V_DIM), BF16),
        scratch_shapes=[
            pltpu.VMEM((nrow, NSA_QK_DIM), BF16),
            pltpu.VMEM((1, nrow), F32),
            pltpu.VMEM((1, nrow), F32),
            pltpu.VMEM((NSA_V_DIM, nrow), F32),
        ],
        compiler_params=_params("arbitrary", "arbitrary", "arbitrary"),
        name="nsa_attn",
    )(q, tabs, gates, kcmp, vcmp_t, ks, vs_t, kw, vw_t, jnp.asarray(ov, BF16), jnp.asarray(expand, BF16))


def _nsa_mixer(h2, bsz, seq, positions, attn_norm, w_in, k_pe, k_w1, k_b1, k_w2, v_pe, v_w1, v_b1, v_w2, w_out,
               *, kc=512):
    t = h2.shape[0]
    kc = min(kc, seq)
    g, jh, dk, dv = NSA_GROUPS, NSA_HPG, NSA_QK_DIM, NSA_V_DIM
    nq = NSA_HEADS * dk
    n_rest = w_in.shape[1] - nq
    n_rest_p = -(-n_rest // 512) * 512
    q = _mm(h2, w_in[:, :nq].astype(BF16), gain=attn_norm, out_dtype=BF16, tm=1024, tn=1024)
    w_rest = jnp.pad(w_in[:, nq:], ((0, 0), (0, n_rest_p - n_rest))).astype(BF16)
    rest = _mm(h2, w_rest, gain=attn_norm, tn=512)
    offs = np.cumsum([0, NSA_KD, NSA_VD, NSA_KD, NSA_VD, NSA_KD, NSA_VD, 3 * NSA_HEADS])
    k_c, v_c, k_s, v_s, k_w, v_w, g_logit = [rest[:, int(a):int(b)] for a, b in zip(offs[:-1], offs[1:])]

    pos = positions.reshape(t)
    cos, sin = _rope_cos_sin(pos, NSA_ROPE)

    def rope_k(k):
        return _rope_glue(k.reshape(t, g, dk), cos[:, None, :], sin[:, None, :])

    def per_group(x, d):
        return x.reshape(bsz, seq, g, d).transpose(0, 2, 1, 3).astype(BF16)

    def chunks(x, d):
        x = x.reshape(bsz, seq // CMP_STRIDE, CMP_STRIDE, g, d).transpose(0, 3, 1, 2, 4)
        return x.reshape(bsz, g, seq // CMP_STRIDE, CMP_STRIDE * d).astype(BF16)

    kcmp = _compress(chunks(rope_k(k_c), dk), k_pe, k_w1, k_b1, k_w2)
    vcmp_t = jnp.swapaxes(_compress(chunks(v_c.reshape(t, g, dv), dv), v_pe, v_w1, v_b1, v_w2), -1, -2)
    ks = per_group(rope_k(k_s), dk)
    vs_t = _chunks_t(per_group(v_s.reshape(t, g, dv), dv), kc)
    pad = ((0, 0), (0, 0), (WINDOW, 0), (0, 0))
    kw = jnp.pad(per_group(rope_k(k_w), dk), pad)
    vw_t = _chunks_t(jnp.pad(per_group(v_w.reshape(t, g, dv), dv), pad), Q_BLOCK)
    gates = jax.nn.sigmoid(g_logit).reshape(bsz, seq, 3, g, jh).transpose(0, 3, 2, 4, 1).reshape(bsz, g, 3 * jh, seq)
    tabs = _rope_lane_tables(pos, NSA_ROPE).reshape(bsz, seq, 3 * LANES)

    o = _nsa_attention(q.reshape(bsz, seq, nq), tabs, gates, kcmp, vcmp_t, ks, vs_t, kw, vw_t, kc=kc)
    return _mm(o.reshape(t, NSA_HEADS * dv), w_out.astype(BF16), res=h2)


def _norm_body(x_ref, g_ref, o_ref):
    o_ref[...] = _rms(x_ref[...], g_ref[...])


def _final_norm(h2, gain, *, tm=512):
    t, d = h2.shape
    tm = min(tm, t)
    return pl.pallas_call(
        _norm_body,
        grid=(t // tm,),
        in_specs=[pl.BlockSpec((tm, d), lambda i: (i, 0)), pl.BlockSpec((1, d), lambda i: (0, 0))],
        out_specs=pl.BlockSpec((tm, d), lambda i: (i, 0)),
        out_shape=jax.ShapeDtypeStruct((t, d), F32),
        compiler_params=_params("arbitrary"),
        name="final_norm",
    )(h2, gain.reshape(1, d))


def _ffn_layer(h2, seq, ffn_norm, w_up, conv_w, conv_b, w_down):
    return _ffn(h2, seq, ffn_norm, w_up.astype(BF16), conv_w, conv_b, w_down.astype(BF16))


def kernel(x, positions, l0_attn_norm, l0_dsa_w_in, l0_dsa_q_norm, l0_dsa_kv_norm, l0_dsa_idx_ln_g, l0_dsa_idx_ln_b, l0_dsa_w_qup, l0_dsa_w_uk, l0_dsa_w_uv, l0_dsa_w_out, l0_ffn_norm, l0_ffn_up, l0_ffn_conv_w, l0_ffn_conv_b, l0_ffn_down, l1_attn_norm, l1_fox_w_in, l1_fox_b_f, l1_fox_w_out, l1_ffn_norm, l1_ffn_up, l1_ffn_conv_w, l1_ffn_conv_b, l1_ffn_down, l2_attn_norm, l2_nsa_w_in, l2_nsa_k_pe, l2_nsa_k_w1, l2_nsa_k_b1, l2_nsa_k_w2, l2_nsa_v_pe, l2_nsa_v_w1, l2_nsa_v_b1, l2_nsa_v_w2, l2_nsa_w_out, l2_ffn_norm, l2_ffn_up, l2_ffn_conv_w, l2_ffn_conv_b, l2_ffn_down, l3_attn_norm, l3_dsa_w_in, l3_dsa_q_norm, l3_dsa_kv_norm, l3_dsa_idx_ln_g, l3_dsa_idx_ln_b, l3_dsa_w_qup, l3_dsa_w_uk, l3_dsa_w_uv, l3_dsa_w_out, l3_ffn_norm, l3_ffn_up, l3_ffn_conv_w, l3_ffn_conv_b, l3_ffn_down, final_norm):
    bsz, seq, d = x.shape
    h = x.reshape(bsz * seq, d)
    h = _dsa_mixer(h, bsz, seq, positions, l0_attn_norm, l0_dsa_w_in, l0_dsa_q_norm, l0_dsa_kv_norm,
                   l0_dsa_idx_ln_g, l0_dsa_idx_ln_b, l0_dsa_w_qup, l0_dsa_w_uk, l0_dsa_w_uv, l0_dsa_w_out)
    h = _ffn_layer(h, seq, l0_ffn_norm, l0_ffn_up, l0_ffn_conv_w, l0_ffn_conv_b, l0_ffn_down)
    h = _fox_mixer(h, bsz, seq, l1_attn_norm, l1_fox_w_in, l1_fox_b_f, l1_fox_w_out)
    h = _ffn_layer(h, seq, l1_ffn_norm, l1_ffn_up, l1_ffn_conv_w, l1_ffn_conv_b, l1_ffn_down)
    h = _nsa_mixer(h, bsz, seq, positions, l2_attn_norm, l2_nsa_w_in, l2_nsa_k_pe, l2_nsa_k_w1, l2_nsa_k_b1,
                   l2_nsa_k_w2, l2_nsa_v_pe, l2_nsa_v_w1, l2_nsa_v_b1, l2_nsa_v_w2, l2_nsa_w_out)
    h = _ffn_layer(h, seq, l2_ffn_norm, l2_ffn_up, l2_ffn_conv_w, l2_ffn_conv_b, l2_ffn_down)
    h = _dsa_mixer(h, bsz, seq, positions, l3_attn_norm, l3_dsa_w_in, l3_dsa_q_norm, l3_dsa_kv_norm,
                   l3_dsa_idx_ln_g, l3_dsa_idx_ln_b, l3_dsa_w_qup, l3_dsa_w_uk, l3_dsa_w_uv, l3_dsa_w_out)
    h = _ffn_layer(h, seq, l3_ffn_norm, l3_ffn_up, l3_ffn_conv_w, l3_ffn_conv_b, l3_ffn_down)
    return _final_norm(h, final_norm).reshape(bsz, seq, d)
```

```python
import functools

import jax
import jax.numpy as jnp
import numpy as np
from jax import lax
from jax.experimental import pallas as pl
from jax.experimental.pallas import tpu as pltpu

F32 = jnp.float32
BF16 = jnp.bfloat16

ROPE_THETA = 500000.0
NORM_EPS = 1e-6
Q_BLOCK = 128

DSA_HEADS = 32
DSA_Q_RANK = 512
DSA_KV_RANK = 256
DSA_QK_DIM = 192
DSA_ROPE = 48
DSA_NOPE = DSA_QK_DIM - DSA_ROPE
DSA_V_DIM = 128
IDX_HEADS = 16
IDX_DIM = 128
IDX_ROPE = 32
DSA_TOPK_MAX = 256
DSA_IN = DSA_Q_RANK + DSA_KV_RANK + DSA_ROPE + IDX_DIM + IDX_HEADS

FOX_HEADS = 16
FOX_DIM = 128

NSA_HEADS = 48
NSA_GROUPS = 4
NSA_HPG = NSA_HEADS // NSA_GROUPS
NSA_QK_DIM = 192
NSA_ROPE = 48
NSA_V_DIM = 128
CMP_BLOCK = 32
CMP_STRIDE = 16
CMP_HIDDEN = 256
SLC_BLOCK = 64
SLC_TOPK = 16
WINDOW = 512
NSA_KD = NSA_GROUPS * NSA_QK_DIM
NSA_VD = NSA_GROUPS * NSA_V_DIM

CONV_WIDTH = 3

LANES = 128
SUBLANES = 8
BF16_SUBLANES = 16
MXU_DIM = 256
VMEM_LIMIT = 56 * 1024 * 1024

MASK_NEG = -1e30
LOG2E = 1.4426950408889634
BISECT_ITERS = 32


def _params(*sem):
    return pltpu.CompilerParams(dimension_semantics=sem, vmem_limit_bytes=VMEM_LIMIT)


def _nt(a, b):
    return lax.dot_general(a, b, (((1,), (1,)), ((), ())), preferred_element_type=F32)


def _tn(a, b):
    return lax.dot_general(a, b, (((0,), (0,)), ((), ())), preferred_element_type=F32)


def _rms(x, g):
    return x * lax.rsqrt(jnp.mean(x * x, axis=-1, keepdims=True) + NORM_EPS) * g


def _rope_first_vreg(x0, cf, sa, sb, half):
    return (x0 * cf + pltpu.roll(x0, LANES - half, 1) * sa + pltpu.roll(x0, half, 1) * sb)


def _split3(x):
    hi = x.astype(BF16)
    r1 = x - hi.astype(F32)
    mid = r1.astype(BF16)
    lo = (r1 - mid.astype(F32)).astype(BF16)
    return hi, mid, lo


def _split3_bits(x):
    def trunc(v):
        bits = lax.bitcast_convert_type(v, jnp.uint32) & jnp.uint32(0xFFFF0000)
        return lax.bitcast_convert_type(bits, F32)

    hi = trunc(x)
    r1 = x - hi
    mid = trunc(r1)
    lo = r1 - mid
    return hi.astype(BF16), mid.astype(BF16), lo.astype(BF16)


def _online_softmax_step(s, pv_lhs, m_ref, l_ref, acc_ref, cols):
    m_prev = m_ref[:, cols]
    m_new = jnp.maximum(m_prev, s.max(axis=0, keepdims=True))
    alpha = jnp.exp2(m_prev - m_new)
    p = jnp.exp2(s - m_new)
    l_ref[:, cols] = alpha * l_ref[:, cols] + p.sum(axis=0, keepdims=True)
    acc_ref[:, cols] = alpha * acc_ref[:, cols] + jnp.dot(pv_lhs, p.astype(BF16), preferred_element_type=F32)
    m_ref[:, cols] = m_new


def _softmax_cols(z):
    mx = z.max(axis=0, keepdims=True)
    e = jnp.exp2(z - mx)
    den = e.sum(axis=0, keepdims=True)
    return e * jnp.where(mx > 0.5 * MASK_NEG, 1.0 / den, 0.0)


def _mm_body(*refs, has_gain, has_res):
    x_ref, w_ref = refs[0], refs[1]
    k = 2
    g_ref = r_ref = None
    if has_gain:
        g_ref = refs[k]
        k += 1
    if has_res:
        r_ref = refs[k]
        k += 1
    o_ref = refs[k]
    if has_gain:
        xn_ref = refs[k + 1]

        @pl.when(pl.program_id(1) == 0)
        def _():
            xn_ref[...] = _rms(x_ref[...].astype(F32), g_ref[...]).astype(BF16)

        a = xn_ref[...]
    else:
        a = x_ref[...].astype(BF16)
    acc = jnp.dot(a, w_ref[...], preferred_element_type=F32)
    if has_res:
        acc = acc + r_ref[...]
    o_ref[...] = acc.astype(o_ref.dtype)


def _mm(x, w, *, gain=None, res=None, out_dtype=F32, x_col_block=0, tm=512, tn=512):
    m = x.shape[0]
    k, n = w.shape
    tm = min(tm, m)
    tn = min(tn, n)
    assert m % tm == 0 and n % tn == 0 and x.shape[1] % k == 0
    in_specs = [pl.BlockSpec((tm, k), lambda i, j: (i, x_col_block)),
                pl.BlockSpec((k, tn), lambda i, j: (0, j))]
    args = [x, w]
    scratch = []
    if gain is not None:
        in_specs.append(pl.BlockSpec((1, k), lambda i, j: (0, 0)))
        args.append(gain.reshape(1, k).astype(F32))
        scratch.append(pltpu.VMEM((tm, k), BF16))
    if res is not None:
        in_specs.append(pl.BlockSpec((tm, tn), lambda i, j: (i, j)))
        args.append(res)
    return pl.pallas_call(
        functools.partial(_mm_body, has_gain=gain is not None, has_res=res is not None),
        grid=(m // tm, n // tn),
        in_specs=in_specs,
        out_specs=pl.BlockSpec((tm, tn), lambda i, j: (i, j)),
        out_shape=jax.ShapeDtypeStruct((m, n), out_dtype),
        scratch_shapes=scratch,
        compiler_params=_params("arbitrary", "arbitrary"),
        name="mm",
    )(*args)


FFN_HALO = BF16_SUBLANES


def _ffn_body(h_ref, hp_ref, g_ref, wg_ref, wv_ref, cwg_ref, cwv_ref, cbg_ref, cbv_ref, wd_ref,
              o_ref, xn_ref, ug_ref, uv_ref, acc_ref, *, tm, seq):
    i = pl.program_id(0)
    j = pl.program_id(1)

    @pl.when(j == 0)
    def _():
        xn_ref[FFN_HALO:, :] = _rms(h_ref[...], g_ref[...]).astype(BF16)
        prev = _rms(hp_ref[...], g_ref[...])
        seq_start = (i * tm) % seq == 0
        xn_ref[:FFN_HALO, :] = jnp.where(seq_start, 0.0, prev).astype(BF16)
        acc_ref[...] = jnp.zeros_like(acc_ref)

    xn = xn_ref[...]
    ug_ref[...] = jnp.dot(xn, wg_ref[...], preferred_element_type=F32)
    uv_ref[...] = jnp.dot(xn, wv_ref[...], preferred_element_type=F32)

    def conv(u_ref, cw_ref, cb_ref):
        y = cb_ref[...]
        for t in range(CONV_WIDTH):
            off = FFN_HALO - (CONV_WIDTH - 1) + t
            y = y + cw_ref[t:t + 1, :] * u_ref[off:off + tm, :]
        return y

    yg = conv(ug_ref, cwg_ref, cbg_ref)
    yv = conv(uv_ref, cwv_ref, cbv_ref)
    a = (jax.nn.silu(yg) * yv).astype(BF16)
    acc_ref[...] += jnp.dot(a, wd_ref[...], preferred_element_type=F32)

    @pl.when(j == pl.num_programs(1) - 1)
    def _():
        o_ref[...] = h_ref[...] + acc_ref[...]


def _ffn(h, seq, gain, w_up, conv_w, conv_b, w_down, *, tm=1024, tf=512):
    t, d = h.shape
    dff = w_down.shape[0]
    tm = min(tm, seq)
    assert seq % tm == 0 and dff % tf == 0 and tm % FFN_HALO == 0
    nf = dff // tf
    hb = tm // FFN_HALO
    conv_b2 = conv_b.reshape(1, 2 * dff)
    once = pl.Buffered(1)
    return pl.pallas_call(
        functools.partial(_ffn_body, tm=tm, seq=seq),
        grid=(t // tm, nf),
        in_specs=[
            pl.BlockSpec((tm, d), lambda i, j: (i, 0), pipeline_mode=once),
            pl.BlockSpec((FFN_HALO, d), lambda i, j: (jnp.maximum(i * hb - 1, 0), 0)),
            pl.BlockSpec((1, d), lambda i, j: (0, 0)),
            pl.BlockSpec((d, tf), lambda i, j: (0, j)),
            pl.BlockSpec((d, tf), lambda i, j: (0, nf + j)),
            pl.BlockSpec((CONV_WIDTH, tf), lambda i, j: (0, j)),
            pl.BlockSpec((CONV_WIDTH, tf), lambda i, j: (0, nf + j)),
            pl.BlockSpec((1, tf), lambda i, j: (0, j)),
            pl.BlockSpec((1, tf), lambda i, j: (0, nf + j)),
            pl.BlockSpec((tf, d), lambda i, j: (j, 0)),
        ],
        out_specs=pl.BlockSpec((tm, d), lambda i, j: (i, 0), pipeline_mode=once),
        out_shape=jax.ShapeDtypeStruct((t, d), F32),
        scratch_shapes=[
            pltpu.VMEM((tm + FFN_HALO, d), BF16),
            pltpu.VMEM((tm + FFN_HALO, tf), F32),
            pltpu.VMEM((tm + FFN_HALO, tf), F32),
            pltpu.VMEM((tm, d), F32),
        ],
        compiler_params=_params("arbitrary", "arbitrary"),
        name="ffn",
    )(h, h, gain.reshape(1, d), w_up, w_up, conv_w, conv_w, conv_b2, conv_b2, w_down)


def _rope_cos_sin(positions, rot):
    inv = ROPE_THETA ** (-jnp.arange(0, rot, 2, dtype=F32) / rot)
    ang = positions.astype(F32)[..., None] * inv
    return jnp.cos(ang), jnp.sin(ang)


def _rope_lane_tables(positions, rot):
    c, s = _rope_cos_sin(positions, rot)
    half = rot // 2
    shp = c.shape[:-1]
    cf = jnp.concatenate([c, c, jnp.ones(shp + (LANES - rot,), F32)], -1)
    sa = jnp.concatenate([-s, jnp.zeros(shp + (LANES - half,), F32)], -1)
    sb = jnp.concatenate([jnp.zeros(shp + (half,), F32), s, jnp.zeros(shp + (LANES - rot,), F32)], -1)
    return jnp.concatenate([cf, sa, sb], -1)


def _rope_glue(x, cos, sin):
    half = cos.shape[-1]
    x1, x2 = x[..., :half], x[..., half:2 * half]
    return jnp.concatenate([x1 * cos - x2 * sin, x1 * sin + x2 * cos, x[..., 2 * half:]], -1)


def _chunks_t(x, kc):
    s, d = x.shape[-2:]
    lead = x.shape[:-2]
    return jnp.swapaxes(x.reshape(lead + (s // kc, kc, d)), -1, -2)


DSA_QX = 384
DSA_QPAD = MXU_DIM
DSA_HEAD_GROUP = 8
DSA_CNT_ROWS = 4 * SUBLANES


def _dsa_body(q_ref, qi_ref, wi_ref, tab_ref, kidx_ref, kext_ref, ct_ref, wk_ref, wuv_ref, o_ref,
              qx_ref, qis_ref, sc_ref, m_ref, l_ref, acc_ref, *, kc, n_keep):
    i = pl.program_id(1)
    q0 = i * Q_BLOCK
    nc = (q0 + Q_BLOCK + kc - 1) // kc
    scale = DSA_QK_DIM ** -0.5 * LOG2E
    tab = tab_ref[0]
    cf, sa, sb = tab[:, 0:LANES], tab[:, LANES:2 * LANES], tab[:, 2 * LANES:3 * LANES]
    cfi, sai, sbi = tab[:, 3 * LANES:4 * LANES], tab[:, 4 * LANES:5 * LANES], tab[:, 5 * LANES:6 * LANES]

    for h in range(DSA_HEADS):
        xh = q_ref[0, :, h * DSA_QPAD:(h + 1) * DSA_QPAD].astype(F32)
        r0 = _rope_first_vreg(xh[:, :LANES], cf, sa, sb, DSA_ROPE // 2)
        qh = jnp.concatenate([r0, xh[:, LANES:]], axis=1).astype(BF16)
        qx = jnp.dot(qh, wk_ref[h], preferred_element_type=F32) * scale
        qx_ref[h * Q_BLOCK:(h + 1) * Q_BLOCK, :] = qx.astype(BF16)
    for h in range(IDX_HEADS):
        xi = qi_ref[0, :, h * IDX_DIM:(h + 1) * IDX_DIM].astype(F32)
        qis_ref[h * Q_BLOCK:(h + 1) * Q_BLOCK, :] = _rope_first_vreg(xi, cfi, sai, sbi, IDX_ROPE // 2).astype(BF16)

    wl = wi_ref[0, 0]
    key_i = lax.broadcasted_iota(jnp.int32, (kc, Q_BLOCK), 0)
    t_l = q0 + lax.broadcasted_iota(jnp.int32, (kc, Q_BLOCK), 1)

    def score_chunk(c, carry):
        lo, hi = carry
        k0 = pl.multiple_of(c * kc, kc)
        d = _nt(kidx_ref[0, pl.ds(k0, kc), :], qis_ref[...])
        d = jnp.maximum(d, 0.0) * wl
        sc = d[:, 0:Q_BLOCK]
        for h in range(1, IDX_HEADS):
            sc = sc + d[:, h * Q_BLOCK:(h + 1) * Q_BLOCK]
        sc = sc + 0.0
        causal = (k0 + key_i) <= t_l
        sc_ref[c] = jnp.where(causal, sc, -jnp.inf)
        lo = jnp.minimum(lo, jnp.where(causal, sc, jnp.inf).min(axis=0, keepdims=True))
        hi = jnp.maximum(hi, jnp.where(causal, sc, -jnp.inf).max(axis=0, keepdims=True))
        return lo, hi

    lo, hi = lax.fori_loop(0, nc, score_chunk,
                           (jnp.full((1, Q_BLOCK), jnp.inf, F32), jnp.full((1, Q_BLOCK), -jnp.inf, F32)))

    def bisect_more(carry):
        it, _, _, n_lo = carry
        return (it < BISECT_ITERS) & (jnp.max(n_lo) > n_keep)

    def bisect(carry):
        it, lo, hi, n_lo = carry
        mid = lo + (hi - lo) * 0.5

        def count_chunk(c, cnt):
            ge = jnp.where(sc_ref[c] >= mid, 1.0, 0.0)
            return cnt + ge.reshape(kc // DSA_CNT_ROWS, DSA_CNT_ROWS, Q_BLOCK).sum(axis=0)

        cnt = lax.fori_loop(0, nc, count_chunk, jnp.zeros((DSA_CNT_ROWS, Q_BLOCK), F32)).sum(axis=0, keepdims=True)
        ge = cnt >= n_keep
        return it + 1, jnp.where(ge, mid, lo), jnp.where(ge, hi, mid), jnp.where(ge, cnt, n_lo)

    n_causal = (q0 + 1 + lax.broadcasted_iota(jnp.int32, (1, Q_BLOCK), 1)).astype(F32)
    _, lo, hi, _ = lax.while_loop(bisect_more, bisect, (0, lo, hi, n_causal))

    def bias_chunk(c, _):
        sc_ref[c] = jnp.where(sc_ref[c] >= lo, 0.0, MASK_NEG)
        return 0

    lax.fori_loop(0, nc, bias_chunk, 0)

    m_ref[...] = jnp.full_like(m_ref, MASK_NEG)
    l_ref[...] = jnp.zeros_like(l_ref)
    acc_ref[...] = jnp.zeros_like(acc_ref)
    gl = DSA_HEAD_GROUP * Q_BLOCK

    def attn_chunk(c, _):
        k0 = pl.multiple_of(c * kc, kc)
        kx = kext_ref[0, pl.ds(k0, kc), :]
        ct = ct_ref[0, c]
        bias = jnp.tile(sc_ref[c], (1, DSA_HEAD_GROUP))
        n_slab = DSA_HEADS // DSA_HEAD_GROUP
        s = _nt(kx, qx_ref[0:gl, :]) + bias
        for g in range(n_slab):
            s_next = _nt(kx, qx_ref[(g + 1) * gl:(g + 2) * gl, :]) + bias if g + 1 < n_slab else None
            _online_softmax_step(s, ct, m_ref, l_ref, acc_ref, slice(g * gl, (g + 1) * gl))
            s = s_next
        return 0

    lax.fori_loop(0, nc, attn_chunk, 0)

    for h in range(DSA_HEADS):
        cols = slice(h * Q_BLOCK, (h + 1) * Q_BLOCK)
        o_lat = (acc_ref[:, cols] / l_ref[:, cols]).astype(BF16)
        o_ref[0, :, h * DSA_V_DIM:(h + 1) * DSA_V_DIM] = _tn(o_lat, wuv_ref[h]).astype(o_ref.dtype)


def _dsa_attention(qall, wi_l, tabs, kidx, kext, ct, wk, wuv, *, kc):
    b, s, _ = qall.shape
    assert s % kc == 0 and kc % Q_BLOCK == 0
    n_keep = min(DSA_TOPK_MAX, s // 4)
    nq = DSA_HEADS * DSA_QPAD
    ni = IDX_HEADS * IDX_DIM
    assert nq % ni == 0
    nrow = DSA_HEADS * Q_BLOCK
    return pl.pallas_call(
        functools.partial(_dsa_body, kc=kc, n_keep=float(n_keep)),
        grid=(b, s // Q_BLOCK),
        in_specs=[
            pl.BlockSpec((1, Q_BLOCK, nq), lambda bb, i: (bb, i, 0)),
            pl.BlockSpec((1, Q_BLOCK, ni), lambda bb, i: (bb, i, nq // ni)),
            pl.BlockSpec((1, 1, 1, IDX_HEADS * Q_BLOCK), lambda bb, i: (bb, i, 0, 0)),
            pl.BlockSpec((1, Q_BLOCK, 6 * LANES), lambda bb, i: (bb, i, 0)),
            pl.BlockSpec((1, s, IDX_DIM), lambda bb, i: (bb, 0, 0)),
            pl.BlockSpec((1, s, DSA_QX), lambda bb, i: (bb, 0, 0)),
            pl.BlockSpec((1, s // kc, DSA_KV_RANK, kc), lambda bb, i: (bb, 0, 0, 0)),
            pl.BlockSpec((DSA_HEADS, DSA_QPAD, DSA_QX), lambda bb, i: (0, 0, 0)),
            pl.BlockSpec((DSA_HEADS, DSA_KV_RANK, DSA_V_DIM), lambda bb, i: (0, 0, 0)),
        ],
        out_specs=pl.BlockSpec((1, Q_BLOCK, DSA_HEADS * DSA_V_DIM), lambda bb, i: (bb, i, 0)),
        out_shape=jax.ShapeDtypeStruct((b, s, DSA_HEADS * DSA_V_DIM), BF16),
        scratch_shapes=[
            pltpu.VMEM((nrow, DSA_QX), BF16),
            pltpu.VMEM((IDX_HEADS * Q_BLOCK, IDX_DIM), BF16),
            pltpu.VMEM((s // kc, kc, Q_BLOCK), F32),
            pltpu.VMEM((1, nrow), F32),
            pltpu.VMEM((1, nrow), F32),
            pltpu.VMEM((DSA_KV_RANK, nrow), F32),
        ],
        compiler_params=_params("arbitrary", "arbitrary"),
        name="dsa_attn",
    )(qall, qall, wi_l, tabs, kidx, kext, ct, wk, wuv)


def _dsa_mixer(h2, bsz, seq, positions, attn_norm, w_in, q_norm, kv_norm, idx_ln_g, idx_ln_b,
               w_qup, w_uk, w_uv, w_out, *, kc=512):
    t = h2.shape[0]
    kc = min(kc, seq)
    w_in_p = jnp.pad(w_in, ((0, 0), (0, 1024 - DSA_IN))).astype(BF16)
    proj = _mm(h2, w_in_p, gain=attn_norm, tn=1024)
    o0 = DSA_Q_RANK
    ckv = proj[:, o0:o0 + DSA_KV_RANK]
    o1 = o0 + DSA_KV_RANK
    k_pe = proj[:, o1:o1 + DSA_ROPE]
    o2 = o1 + DSA_ROPE
    k_idx = proj[:, o2:o2 + IDX_DIM]
    o3 = o2 + IDX_DIM
    w_idx = proj[:, o3:o3 + IDX_HEADS] * (IDX_HEADS ** -0.5 * IDX_DIM ** -0.5)

    nq = DSA_HEADS * DSA_QK_DIM
    w_q = w_qup[:, :nq].reshape(DSA_Q_RANK, DSA_HEADS, DSA_QK_DIM)
    w_q = jnp.pad(w_q, ((0, 0), (0, 0), (0, DSA_QPAD - DSA_QK_DIM))).reshape(DSA_Q_RANK, DSA_HEADS * DSA_QPAD)
    w_qp = jnp.concatenate([w_q, w_qup[:, nq:]], axis=1).astype(BF16)
    qall = _mm(proj, w_qp, gain=q_norm, out_dtype=BF16, tn=w_qp.shape[1])

    pos = positions.reshape(t)
    cos, sin = _rope_cos_sin(pos, DSA_ROPE)
    icos, isin = _rope_cos_sin(pos, IDX_ROPE)
    c_n = ckv * lax.rsqrt(jnp.mean(ckv * ckv, -1, keepdims=True) + NORM_EPS) * kv_norm
    kext = jnp.concatenate(
        [c_n, _rope_glue(k_pe, cos, sin), jnp.zeros((t, DSA_QX - DSA_KV_RANK - DSA_ROPE), F32)], -1)
    mu = jnp.mean(k_idx, -1, keepdims=True)
    var = jnp.mean(jnp.square(k_idx - mu), -1, keepdims=True)
    k_ln = (k_idx - mu) * lax.rsqrt(var + NORM_EPS) * idx_ln_g + idx_ln_b
    kidx = _rope_glue(k_ln, icos, isin)
    tabs = jnp.concatenate([_rope_lane_tables(pos, DSA_ROPE), _rope_lane_tables(pos, IDX_ROPE)], -1)
    ct = _chunks_t(c_n.astype(BF16).reshape(bsz, seq, DSA_KV_RANK), kc)
    wi_l = w_idx.reshape(bsz, seq // Q_BLOCK, Q_BLOCK, IDX_HEADS).transpose(0, 1, 3, 2)
    wi_l = wi_l.reshape(bsz, seq // Q_BLOCK, 1, IDX_HEADS * Q_BLOCK)

    wk = jnp.zeros((DSA_HEADS, DSA_QPAD, DSA_QX), F32)
    wk = wk.at[:, DSA_ROPE:DSA_QK_DIM, :DSA_KV_RANK].set(jnp.swapaxes(w_uk, 1, 2))
    eye = jnp.eye(DSA_ROPE, dtype=F32)
    wk = wk.at[:, :DSA_ROPE, DSA_KV_RANK:DSA_KV_RANK + DSA_ROPE].set(jnp.broadcast_to(eye, (DSA_HEADS,) + eye.shape))

    o = _dsa_attention(
        qall.reshape(bsz, seq, -1), wi_l, tabs.reshape(bsz, seq, -1),
        kidx.astype(BF16).reshape(bsz, seq, IDX_DIM), kext.astype(BF16).reshape(bsz, seq, DSA_QX), ct,
        wk.astype(BF16), w_uv.astype(BF16), kc=kc)
    return _mm(o.reshape(t, -1), w_out.astype(BF16), res=h2)


FOX_KX = MXU_DIM
FOX_HEAD_PAIR = 2


def _fox_body(q_ref, kx_ref, vt_ref, o_ref, qx_ref, m_ref, l_ref, acc_ref, *, tq, kc):
    i = pl.program_id(2)
    q0 = i * tq
    n_full = q0 // kc
    n_diag = tq // kc
    lane = lax.broadcasted_iota(jnp.int32, (tq, FOX_KX - FOX_DIM), 1)
    for a in range(FOX_HEAD_PAIR):
        qx_ref[a, :, :FOX_DIM] = q_ref[0, :, a * FOX_DIM:(a + 1) * FOX_DIM]
        qx_ref[a, :, FOX_DIM:] = jnp.where(lane < 3, 1.0, 0.0).astype(BF16)
    m_ref[...] = jnp.full_like(m_ref, MASK_NEG)
    l_ref[...] = jnp.zeros_like(l_ref)
    acc_ref[...] = jnp.zeros_like(acc_ref)
    key_i = lax.broadcasted_iota(jnp.int32, (kc, tq), 0)
    t_l = q0 + lax.broadcasted_iota(jnp.int32, (kc, tq), 1)

    def step(c, masked):
        k0 = pl.multiple_of(c * kc, kc)

        def logits(a):
            s = _nt(kx_ref[0, a, pl.ds(k0, kc), :], qx_ref[a])
            return jnp.where((k0 + key_i) <= t_l, s, MASK_NEG) if masked else s

        s = logits(0)
        for a in range(FOX_HEAD_PAIR):
            s_next = logits(a + 1) if a + 1 < FOX_HEAD_PAIR else None
            _online_softmax_step(s, vt_ref[0, a, c], m_ref, l_ref, acc_ref, slice(a * tq, (a + 1) * tq))
            s = s_next

    def full_chunk(c, _):
        step(c, False)
        return 0

    def diag_chunk(c, _):
        step(c, True)
        return 0

    lax.fori_loop(0, n_full, full_chunk, 0)
    lax.fori_loop(n_full, n_full + n_diag, diag_chunk, 0)
    for a in range(FOX_HEAD_PAIR):
        cols = slice(a * tq, (a + 1) * tq)
        o_ref[0, :, a * FOX_DIM:(a + 1) * FOX_DIM] = (acc_ref[:, cols] / l_ref[:, cols]).T.astype(o_ref.dtype)


def _fox_attention(qkv, kx, vt, *, tq, kc):
    b, s, _ = qkv.shape
    hp = FOX_HEAD_PAIR
    assert s % tq == 0 and tq % kc == 0 and FOX_HEADS % hp == 0
    return pl.pallas_call(
        functools.partial(_fox_body, tq=tq, kc=kc),
        grid=(b, FOX_HEADS // hp, s // tq),
        in_specs=[
            pl.BlockSpec((1, tq, hp * FOX_DIM), lambda bb, h, i: (bb, i, h)),
            pl.BlockSpec((1, hp, s, FOX_KX), lambda bb, h, i: (bb, h, 0, 0)),
            pl.BlockSpec((1, hp, s // kc, FOX_DIM, kc), lambda bb, h, i: (bb, h, 0, 0, 0)),
        ],
        out_specs=pl.BlockSpec((1, tq, hp * FOX_DIM), lambda bb, h, i: (bb, i, h)),
        out_shape=jax.ShapeDtypeStruct((b, s, FOX_HEADS * FOX_DIM), BF16),
        scratch_shapes=[pltpu.VMEM((hp, tq, FOX_KX), BF16), pltpu.VMEM((1, hp * tq), F32),
                        pltpu.VMEM((1, hp * tq), F32), pltpu.VMEM((FOX_DIM, hp * tq), F32)],
        compiler_params=_params("arbitrary", "arbitrary", "arbitrary"),
        name="fox_attn",
    )(qkv, kx, vt)


def _fox_mixer(h2, bsz, seq, attn_norm, w_in, b_f, w_out, *, tq=512, kc=512):
    t = h2.shape[0]
    hd = FOX_HEADS * FOX_DIM
    tq = min(tq, seq)
    kc = min(kc, tq)
    scale = FOX_DIM ** -0.5 * LOG2E
    w_qkv = jnp.concatenate([w_in[:, :hd] * scale, w_in[:, hd:3 * hd]], axis=1).astype(BF16)
    qkv = _mm(h2, w_qkv, gain=attn_norm, out_dtype=BF16, tn=1024)
    w_f = jnp.pad(w_in[:, 3 * hd:], ((0, 0), (0, LANES - FOX_HEADS))).astype(BF16)
    f_logit = _mm(h2, w_f, gain=attn_norm)[:, :FOX_HEADS]
    log_f = jax.nn.log_sigmoid(f_logit + b_f)
    cum = jnp.cumsum(log_f.reshape(bsz, seq, FOX_HEADS), axis=1)
    c_hi, c_mid, c_lo = _split3_bits(-cum * LOG2E)
    qkv4 = qkv.reshape(bsz, seq, 3, FOX_HEADS, FOX_DIM)
    kx = jnp.concatenate(
        [qkv4[:, :, 1], jnp.stack([c_hi, c_mid, c_lo], -1),
         jnp.zeros((bsz, seq, FOX_HEADS, FOX_KX - FOX_DIM - 3), BF16)], -1).transpose(0, 2, 1, 3)
    vt = _chunks_t(qkv4[:, :, 2].transpose(0, 2, 1, 3), kc)
    o = _fox_attention(qkv.reshape(bsz, seq, 3 * hd), kx, vt, tq=tq, kc=kc)
    return _mm(o.reshape(t, hd), w_out.astype(BF16), res=h2)


NSA_HEAD_GROUP = 4


def _cmp_body(x_ref, pe_ref, w1_ref, b1_ref, w2_ref, o_ref, *, half):
    x = x_ref[0, 0]
    w_lo = w1_ref[:half, :]
    w_hi = w1_ref[half:, :]
    a = jnp.dot(x, w_lo, preferred_element_type=F32)
    bnext = jnp.dot(x, w_hi, preferred_element_type=F32)
    n = x.shape[0]
    bnext = pltpu.roll(bnext, n - 1, 0)
    pe = pe_ref[...]
    pe_b = (jnp.dot(pe[:, :half], w_lo, preferred_element_type=F32)
            + jnp.dot(pe[:, half:], w_hi, preferred_element_type=F32))[0:1, :]
    hid = jax.nn.gelu(a + bnext + pe_b + b1_ref[...])
    o_ref[0, 0] = jnp.dot(hid.astype(BF16), w2_ref[...], preferred_element_type=F32).astype(o_ref.dtype)


def _compress(x, pe, w1, b1, w2):
    b, g, n, kd = x.shape
    dout = w2.shape[1]
    pe8 = jnp.broadcast_to(pe.reshape(1, 2 * kd), (8, 2 * kd)).astype(BF16)
    return pl.pallas_call(
        functools.partial(_cmp_body, half=kd),
        grid=(b, g),
        in_specs=[
            pl.BlockSpec((1, 1, n, kd), lambda bb, gg: (bb, gg, 0, 0)),
            pl.BlockSpec((8, 2 * kd), lambda bb, gg: (0, 0)),
            pl.BlockSpec((2 * kd, CMP_HIDDEN), lambda bb, gg: (0, 0)),
            pl.BlockSpec((1, CMP_HIDDEN), lambda bb, gg: (0, 0)),
            pl.BlockSpec((CMP_HIDDEN, dout), lambda bb, gg: (0, 0)),
        ],
        out_specs=pl.BlockSpec((1, 1, n, dout), lambda bb, gg: (bb, gg, 0, 0)),
        out_shape=jax.ShapeDtypeStruct((b, g, n, dout), BF16),
        compiler_params=_params("arbitrary", "arbitrary"),
        name="nsa_compress",
    )(x, pe8, w1.astype(BF16), b1.reshape(1, CMP_HIDDEN), w2.astype(BF16))


def _nsa_body(q_ref, tab_ref, gate_ref, kc_ref, vct_ref, ks_ref, vst_ref, kw_ref, vwt_ref, ov_ref, exp_ref,
              o_ref, qs_ref, m_ref, l_ref, acc_ref, *, kc, n_cmp, n_slc, n_sel):
    i = pl.program_id(2)
    q0 = i * Q_BLOCK
    nc = (q0 + Q_BLOCK + kc - 1) // kc
    jh = NSA_HPG
    scale = NSA_QK_DIM ** -0.5 * LOG2E
    tab = tab_ref[0]
    cf, sa, sb = tab[:, 0:LANES], tab[:, LANES:2 * LANES], tab[:, 2 * LANES:3 * LANES]

    for j in range(jh):
        xj = q_ref[0, :, j * NSA_QK_DIM:(j + 1) * NSA_QK_DIM].astype(F32)
        r0 = _rope_first_vreg(xj[:, :LANES], cf, sa, sb, NSA_ROPE // 2)
        qs_ref[j * Q_BLOCK:(j + 1) * Q_BLOCK, :] = (jnp.concatenate([r0, xj[:, LANES:]], axis=1) * scale).astype(BF16)
    qs = qs_ref[...]

    n_id = lax.broadcasted_iota(jnp.int32, (n_cmp, Q_BLOCK), 0)
    t_c = q0 + lax.broadcasted_iota(jnp.int32, (n_cmp, Q_BLOCK), 1)
    cbias = jnp.where((n_id * CMP_STRIDE + (CMP_BLOCK - 1)) <= t_c, 0.0, MASK_NEG)
    pc = _softmax_cols(_nt(kc_ref[0, 0], qs) + jnp.tile(cbias, (1, jh)))
    o_c = jnp.dot(vct_ref[0, 0], pc.astype(BF16), preferred_element_type=F32)

    pcs = pc[:, 0:Q_BLOCK]
    for j in range(1, jh):
        pcs = pcs + pc[:, j * Q_BLOCK:(j + 1) * Q_BLOCK]
    ov = ov_ref[...]
    imp = sum(jnp.dot(ov, term, preferred_element_type=F32) for term in _split3(pcs))

    blk = lax.broadcasted_iota(jnp.int32, (n_slc, Q_BLOCK), 0)
    t_b = q0 + lax.broadcasted_iota(jnp.int32, (n_slc, Q_BLOCK), 1)
    cur = lax.shift_right_logical(t_b, int(np.log2(SLC_BLOCK)))
    causal_blk = blk * SLC_BLOCK <= t_b
    forced = (blk == 0) | (blk == cur) | (blk == cur - 1)
    val = jnp.where(causal_blk, jnp.where(forced, jnp.inf, imp), -jnp.inf)
    rank = jnp.zeros((n_slc, Q_BLOCK), F32)
    for mp in range(n_slc):
        vrow = val[mp:mp + 1, :]
        before = (vrow > val) | ((vrow == val) & (blk > mp))
        rank = rank + jnp.where(before, 1.0, 0.0)
    sel = jnp.where((rank < n_sel) & causal_blk, 1.0, 0.0)
    sel_p = jnp.concatenate([sel, jnp.zeros((LANES - n_slc, Q_BLOCK), F32)], axis=0).astype(BF16)

    m_ref[...] = jnp.full_like(m_ref, MASK_NEG)
    l_ref[...] = jnp.zeros_like(l_ref)
    acc_ref[...] = jnp.zeros_like(acc_ref)
    key_i = lax.broadcasted_iota(jnp.int32, (kc, Q_BLOCK), 0)
    t_k = q0 + lax.broadcasted_iota(jnp.int32, (kc, Q_BLOCK), 1)
    gl = NSA_HEAD_GROUP * Q_BLOCK

    def sel_chunk(c, _):
        k0 = pl.multiple_of(c * kc, kc)
        hit = jnp.dot(exp_ref[pl.ds(k0, kc), :], sel_p, preferred_element_type=F32)
        bias = jnp.where((hit > 0.5) & ((k0 + key_i) <= t_k), 0.0, MASK_NEG)
        bias = jnp.tile(bias, (1, NSA_HEAD_GROUP))
        ksc = ks_ref[0, 0, pl.ds(k0, kc), :]
        vt = vst_ref[0, 0, c]
        n_slab = jh // NSA_HEAD_GROUP
        s = _nt(ksc, qs_ref[0:gl, :]) + bias
        for hg in range(n_slab):
            s_next = _nt(ksc, qs_ref[(hg + 1) * gl:(hg + 2) * gl, :]) + bias if hg + 1 < n_slab else None
            _online_softmax_step(s, vt, m_ref, l_ref, acc_ref, slice(hg * gl, (hg + 1) * gl))
            s = s_next
        return 0

    lax.fori_loop(0, nc, sel_chunk, 0)
    o_s = acc_ref[...] / l_ref[...]

    wl = WINDOW + Q_BLOCK
    kstart = pl.multiple_of(q0, Q_BLOCK)
    s_pos = q0 - WINDOW + lax.broadcasted_iota(jnp.int32, (wl, Q_BLOCK), 0)
    t_w = q0 + lax.broadcasted_iota(jnp.int32, (wl, Q_BLOCK), 1)
    wbias = jnp.where((s_pos >= 0) & (s_pos <= t_w) & (s_pos > t_w - WINDOW), 0.0, MASK_NEG)
    pw = _softmax_cols(_nt(kw_ref[0, 0, pl.ds(kstart, wl), :], qs) + jnp.tile(wbias, (1, jh))).astype(BF16)
    o_w = jnp.dot(vwt_ref[0, 0, i], pw[0:Q_BLOCK, :], preferred_element_type=F32)
    for c in range(1, wl // Q_BLOCK):
        o_w = o_w + jnp.dot(vwt_ref[0, 0, i + c], pw[c * Q_BLOCK:(c + 1) * Q_BLOCK, :],
                            preferred_element_type=F32)

    g = gate_ref[0, 0]
    for j in range(jh):
        cj = slice(j * Q_BLOCK, (j + 1) * Q_BLOCK)
        out = (g[j:j + 1, :] * o_c[:, cj] + g[jh + j:jh + j + 1, :] * o_s[:, cj]
               + g[2 * jh + j:2 * jh + j + 1, :] * o_w[:, cj])
        o_ref[0, :, j * NSA_V_DIM:(j + 1) * NSA_V_DIM] = out.T.astype(o_ref.dtype)


def _nsa_attention(q, tabs, gates, kcmp, vcmp_t, ks, vs_t, kw, vw_t, *, kc):
    b, s, _ = q.shape
    g = NSA_GROUPS
    assert s % kc == 0 and kc % SLC_BLOCK == 0
    n_cmp = kcmp.shape[2]
    n_slc = s // SLC_BLOCK
    assert n_slc <= LANES
    n_sel = min(SLC_TOPK, n_slc)
    cmp_start = np.arange(n_cmp) * CMP_STRIDE
    slc_start = np.arange(n_slc) * SLC_BLOCK
    ov = ((cmp_start[None, :] < slc_start[:, None] + SLC_BLOCK)
          & (cmp_start[None, :] + CMP_BLOCK > slc_start[:, None])).astype(np.float32)
    expand = (np.arange(s)[:, None] // SLC_BLOCK == np.arange(LANES)[None, :]).astype(np.float32)
    qw = NSA_HPG * NSA_QK_DIM
    ow = NSA_HPG * NSA_V_DIM
    nrow = NSA_HPG * Q_BLOCK
    wl = s + WINDOW
    return pl.pallas_call(
        functools.partial(_nsa_body, kc=kc, n_cmp=n_cmp, n_slc=n_slc, n_sel=float(n_sel)),
        grid=(b, g, s // Q_BLOCK),
        in_specs=[
            pl.BlockSpec((1, Q_BLOCK, qw), lambda bb, gg, i: (bb, i, gg)),
            pl.BlockSpec((1, Q_BLOCK, 3 * LANES), lambda bb, gg, i: (bb, i, 0)),
            pl.BlockSpec((1, 1, 3 * NSA_HPG, Q_BLOCK), lambda bb, gg, i: (bb, gg, 0, i)),
            pl.BlockSpec((1, 1, n_cmp, NSA_QK_DIM), lambda bb, gg, i: (bb, gg, 0, 0)),
            pl.BlockSpec((1, 1, NSA_V_DIM, n_cmp), lambda bb, gg, i: (bb, gg, 0, 0)),
            pl.BlockSpec((1, 1, s, NSA_QK_DIM), lambda bb, gg, i: (bb, gg, 0, 0)),
            pl.BlockSpec((1, 1, s // kc, NSA_V_DIM, kc), lambda bb, gg, i: (bb, gg, 0, 0, 0)),
            pl.BlockSpec((1, 1, wl, NSA_QK_DIM), lambda bb, gg, i: (bb, gg, 0, 0)),
            pl.BlockSpec((1, 1, wl // Q_BLOCK, NSA_V_DIM, Q_BLOCK), lambda bb, gg, i: (bb, gg, 0, 0, 0)),
            pl.BlockSpec((n_slc, n_cmp), lambda bb, gg, i: (0, 0)),
            pl.BlockSpec((s, LANES), lambda bb, gg, i: (0, 0)),
        ],
        out_specs=pl.BlockSpec((1, Q_BLOCK, ow), lambda bb, gg, i: (bb, i, gg)),
        out_shape=jax.ShapeDtypeStruct((b, s, NSA_HEADS * NSA_V_DIM), BF16),
        scratch_shapes=[
            pltpu.VMEM((nrow, NSA_QK_DIM), BF16),
            pltpu.VMEM((1, nrow), F32),
            pltpu.VMEM((1, nrow), F32),
            pltpu.VMEM((NSA_V_DIM, nrow), F32),
        ],
        compiler_params=_params("arbitrary", "arbitrary", "arbitrary"),
        name="nsa_attn",
    )(q, tabs, gates, kcmp, vcmp_t, ks, vs_t, kw, vw_t, jnp.asarray(ov, BF16), jnp.asarray(expand, BF16))


def _nsa_mixer(h2, bsz, seq, positions, attn_norm, w_in, k_pe, k_w1, k_b1, k_w2, v_pe, v_w1, v_b1, v_w2, w_out,
               *, kc=512):
    t = h2.shape[0]
    kc = min(kc, seq)
    g, jh, dk, dv = NSA_GROUPS, NSA_HPG, NSA_QK_DIM, NSA_V_DIM
    nq = NSA_HEADS * dk
    n_rest = w_in.shape[1] - nq
    n_rest_p = -(-n_rest // 512) * 512
    q = _mm(h2, w_in[:, :nq].astype(BF16), gain=attn_norm, out_dtype=BF16, tm=1024, tn=1024)
    w_rest = jnp.pad(w_in[:, nq:], ((0, 0), (0, n_rest_p - n_rest))).astype(BF16)
    rest = _mm(h2, w_rest, gain=attn_norm, tn=512)
    offs = np.cumsum([0, NSA_KD, NSA_VD, NSA_KD, NSA_VD, NSA_KD, NSA_VD, 3 * NSA_HEADS])
    k_c, v_c, k_s, v_s, k_w, v_w, g_logit = [rest[:, int(a):int(b)] for a, b in zip(offs[:-1], offs[1:])]

    pos = positions.reshape(t)
    cos, sin = _rope_cos_sin(pos, NSA_ROPE)

    def rope_k(k):
        return _rope_glue(k.reshape(t, g, dk), cos[:, None, :], sin[:, None, :])

    def per_group(x, d):
        return x.reshape(bsz, seq, g, d).transpose(0, 2, 1, 3).astype(BF16)

    def chunks(x, d):
        x = x.reshape(bsz, seq // CMP_STRIDE, CMP_STRIDE, g, d).transpose(0, 3, 1, 2, 4)
        return x.reshape(bsz, g, seq // CMP_STRIDE, CMP_STRIDE * d).astype(BF16)

    kcmp = _compress(chunks(rope_k(k_c), dk), k_pe, k_w1, k_b1, k_w2)
    vcmp_t = jnp.swapaxes(_compress(chunks(v_c.reshape(t, g, dv), dv), v_pe, v_w1, v_b1, v_w2), -1, -2)
    ks = per_group(rope_k(k_s), dk)
    vs_t = _chunks_t(per_group(v_s.reshape(t, g, dv), dv), kc)
    pad = ((0, 0), (0, 0), (WINDOW, 0), (0, 0))
    kw = jnp.pad(per_group(rope_k(k_w), dk), pad)
    vw_t = _chunks_t(jnp.pad(per_group(v_w.reshape(t, g, dv), dv), pad), Q_BLOCK)
    gates = jax.nn.sigmoid(g_logit).reshape(bsz, seq, 3, g, jh).transpose(0, 3, 2, 4, 1).reshape(bsz, g, 3 * jh, seq)
    tabs = _rope_lane_tables(pos, NSA_ROPE).reshape(bsz, seq, 3 * LANES)

    o = _nsa_attention(q.reshape(bsz, seq, nq), tabs, gates, kcmp, vcmp_t, ks, vs_t, kw, vw_t, kc=kc)
    return _mm(o.reshape(t, NSA_HEADS * dv), w_out.astype(BF16), res=h2)


def _norm_body(x_ref, g_ref, o_ref):
    o_ref[...] = _rms(x_ref[...], g_ref[...])


def _final_norm(h2, gain, *, tm=512):
    t, d = h2.shape
    tm = min(tm, t)
    return pl.pallas_call(
        _norm_body,
        grid=(t // tm,),
        in_specs=[pl.BlockSpec((tm, d), lambda i: (i, 0)), pl.BlockSpec((1, d), lambda i: (0, 0))],
        out_specs=pl.BlockSpec((tm, d), lambda i: (i, 0)),
        out_shape=jax.ShapeDtypeStruct((t, d), F32),
        compiler_params=_params("arbitrary"),
        name="final_norm",
    )(h2, gain.reshape(1, d))


def _ffn_layer(h2, seq, ffn_norm, w_up, conv_w, conv_b, w_down):
    return _ffn(h2, seq, ffn_norm, w_up.astype(BF16), conv_w, conv_b, w_down.astype(BF16))


def kernel(x, positions, l0_attn_norm, l0_dsa_w_in, l0_dsa_q_norm, l0_dsa_kv_norm, l0_dsa_idx_ln_g, l0_dsa_idx_ln_b, l0_dsa_w_qup, l0_dsa_w_uk, l0_dsa_w_uv, l0_dsa_w_out, l0_ffn_norm, l0_ffn_up, l0_ffn_conv_w, l0_ffn_conv_b, l0_ffn_down, l1_attn_norm, l1_fox_w_in, l1_fox_b_f, l1_fox_w_out, l1_ffn_norm, l1_ffn_up, l1_ffn_conv_w, l1_ffn_conv_b, l1_ffn_down, l2_attn_norm, l2_nsa_w_in, l2_nsa_k_pe, l2_nsa_k_w1, l2_nsa_k_b1, l2_nsa_k_w2, l2_nsa_v_pe, l2_nsa_v_w1, l2_nsa_v_b1, l2_nsa_v_w2, l2_nsa_w_out, l2_ffn_norm, l2_ffn_up, l2_ffn_conv_w, l2_ffn_conv_b, l2_ffn_down, l3_attn_norm, l3_dsa_w_in, l3_dsa_q_norm, l3_dsa_kv_norm, l3_dsa_idx_ln_g, l3_dsa_idx_ln_b, l3_dsa_w_qup, l3_dsa_w_uk, l3_dsa_w_uv, l3_dsa_w_out, l3_ffn_norm, l3_ffn_up, l3_ffn_conv_w, l3_ffn_conv_b, l3_ffn_down, final_norm):
    bsz, seq, d = x.shape
    h = x.reshape(bsz * seq, d)
    h = _dsa_mixer(h, bsz, seq, positions, l0_attn_norm, l0_dsa_w_in, l0_dsa_q_norm, l0_dsa_kv_norm,
                   l0_dsa_idx_ln_g, l0_dsa_idx_ln_b, l0_dsa_w_qup, l0_dsa_w_uk, l0_dsa_w_uv, l0_dsa_w_out)
    h = _ffn_layer(h, seq, l0_ffn_norm, l0_ffn_up, l0_ffn_conv_w, l0_ffn_conv_b, l0_ffn_down)
    h = _fox_mixer(h, bsz, seq, l1_attn_norm, l1_fox_w_in, l1_fox_b_f, l1_fox_w_out)
    h = _ffn_layer(h, seq, l1_ffn_norm, l1_ffn_up, l1_ffn_conv_w, l1_ffn_conv_b, l1_ffn_down)
    h = _nsa_mixer(h, bsz, seq, positions, l2_attn_norm, l2_nsa_w_in, l2_nsa_k_pe, l2_nsa_k_w1, l2_nsa_k_b1,
                   l2_nsa_k_w2, l2_nsa_v_pe, l2_nsa_v_w1, l2_nsa_v_b1, l2_nsa_v_w2, l2_nsa_w_out)
    h = _ffn_layer(h, seq, l2_ffn_norm, l2_ffn_up, l2_ffn_conv_w, l2_ffn_conv_b, l2_ffn_down)
    h = _dsa_mixer(h, bsz, seq, positions, l3_attn_norm, l3_dsa_w_in, l3_dsa_q_norm, l3_dsa_kv_norm,
                   l3_dsa_idx_ln_g, l3_dsa_idx_ln_b, l3_dsa_w_qup, l3_dsa_w_uk, l3_dsa_w_uv, l3_dsa_w_out)
    h = _ffn_layer(h, seq, l3_ffn_norm, l3_ffn_up, l3_ffn_conv_w, l3_ffn_conv_b, l3_ffn_down)
    return _final_norm(h, final_norm).reshape(bsz, seq, d)
```

```python
import functools

import jax
import jax.numpy as jnp
import numpy as np
from jax import lax
from jax.experimental import pallas as pl
from jax.experimental.pallas import tpu as pltpu

F32 = jnp.float32
BF16 = jnp.bfloat16

ROPE_THETA = 500000.0
NORM_EPS = 1e-6
Q_BLOCK = 128

DSA_HEADS = 32
DSA_Q_RANK = 512
DSA_KV_RANK = 256
DSA_QK_DIM = 192
DSA_ROPE = 48
DSA_NOPE = DSA_QK_DIM - DSA_ROPE
DSA_V_DIM = 128
IDX_HEADS = 16
IDX_DIM = 128
IDX_ROPE = 32
DSA_TOPK_MAX = 256
DSA_IN = DSA_Q_RANK + DSA_KV_RANK + DSA_ROPE + IDX_DIM + IDX_HEADS

FOX_HEADS = 16
FOX_DIM = 128

NSA_HEADS = 48
NSA_GROUPS = 4
NSA_HPG = NSA_HEADS // NSA_GROUPS
NSA_QK_DIM = 192
NSA_ROPE = 48
NSA_V_DIM = 128
CMP_BLOCK = 32
CMP_STRIDE = 16
CMP_HIDDEN = 256
SLC_BLOCK = 64
SLC_TOPK = 16
WINDOW = 512
NSA_KD = NSA_GROUPS * NSA_QK_DIM
NSA_VD = NSA_GROUPS * NSA_V_DIM

CONV_WIDTH = 3

LANES = 128
SUBLANES = 8
BF16_SUBLANES = 16
MXU_DIM = 256
VMEM_LIMIT = 56 * 1024 * 1024

MASK_NEG = -1e30
LOG2E = 1.4426950408889634
BISECT_ITERS = 32


def _params(*sem):
    return pltpu.CompilerParams(dimension_semantics=sem, vmem_limit_bytes=VMEM_LIMIT)


def _nt(a, b):
    return lax.dot_general(a, b, (((1,), (1,)), ((), ())), preferred_element_type=F32)


def _tn(a, b):
    return lax.dot_general(a, b, (((0,), (0,)), ((), ())), preferred_element_type=F32)


def _rms(x, g):
    return x * lax.rsqrt(jnp.mean(x * x, axis=-1, keepdims=True) + NORM_EPS) * g


def _rope_first_vreg(x0, cf, sa, sb, half):
    return (x0 * cf + pltpu.roll(x0, LANES - half, 1) * sa + pltpu.roll(x0, half, 1) * sb)


def _split3(x):
    hi = x.astype(BF16)
    r1 = x - hi.astype(F32)
    mid = r1.astype(BF16)
    lo = (r1 - mid.astype(F32)).astype(BF16)
    return hi, mid, lo


def _split3_bits(x):
    def trunc(v):
        bits = lax.bitcast_convert_type(v, jnp.uint32) & jnp.uint32(0xFFFF0000)
        return lax.bitcast_convert_type(bits, F32)

    hi = trunc(x)
    r1 = x - hi
    mid = trunc(r1)
    lo = r1 - mid
    return hi.astype(BF16), mid.astype(BF16), lo.astype(BF16)


def _online_softmax_step(s, pv_lhs, m_ref, l_ref, acc_ref, cols):
    m_prev = m_ref[:, cols]
    m_new = jnp.maximum(m_prev, s.max(axis=0, keepdims=True))
    alpha = jnp.exp2(m_prev - m_new)
    p = jnp.exp2(s - m_new)
    l_ref[:, cols] = alpha * l_ref[:, cols] + p.sum(axis=0, keepdims=True)
    acc_ref[:, cols] = alpha * acc_ref[:, cols] + jnp.dot(pv_lhs, p.astype(BF16), preferred_element_type=F32)
    m_ref[:, cols] = m_new


def _softmax_cols(z):
    mx = z.max(axis=0, keepdims=True)
    e = jnp.exp2(z - mx)
    den = e.sum(axis=0, keepdims=True)
    return e * jnp.where(mx > 0.5 * MASK_NEG, 1.0 / den, 0.0)


def _mm_body(*refs, has_gain, has_res):
    x_ref, w_ref = refs[0], refs[1]
    k = 2
    g_ref = r_ref = None
    if has_gain:
        g_ref = refs[k]
        k += 1
    if has_res:
        r_ref = refs[k]
        k += 1
    o_ref = refs[k]
    if has_gain:
        xn_ref = refs[k + 1]

        @pl.when(pl.program_id(1) == 0)
        def _():
            xn_ref[...] = _rms(x_ref[...].astype(F32), g_ref[...]).astype(BF16)

        a = xn_ref[...]
    else:
        a = x_ref[...].astype(BF16)
    acc = jnp.dot(a, w_ref[...], preferred_element_type=F32)
    if has_res:
        acc = acc + r_ref[...]
    o_ref[...] = acc.astype(o_ref.dtype)


def _mm(x, w, *, gain=None, res=None, out_dtype=F32, x_col_block=0, tm=512, tn=512):
    m = x.shape[0]
    k, n = w.shape
    tm = min(tm, m)
    tn = min(tn, n)
    assert m % tm == 0 and n % tn == 0 and x.shape[1] >= (x_col_block + 1) * k
    in_specs = [pl.BlockSpec((tm, k), lambda i, j: (i, x_col_block)),
                pl.BlockSpec((k, tn), lambda i, j: (0, j))]
    args = [x, w]
    scratch = []
    if gain is not None:
        in_specs.append(pl.BlockSpec((1, k), lambda i, j: (0, 0)))
        args.append(gain.reshape(1, k).astype(F32))
        scratch.append(pltpu.VMEM((tm, k), BF16))
    if res is not None:
        in_specs.append(pl.BlockSpec((tm, tn), lambda i, j: (i, j)))
        args.append(res)
    return pl.pallas_call(
        functools.partial(_mm_body, has_gain=gain is not None, has_res=res is not None),
        grid=(m // tm, n // tn),
        in_specs=in_specs,
        out_specs=pl.BlockSpec((tm, tn), lambda i, j: (i, j)),
        out_shape=jax.ShapeDtypeStruct((m, n), out_dtype),
        scratch_shapes=scratch,
        compiler_params=_params("arbitrary", "arbitrary"),
        name="mm",
    )(*args)


FFN_HALO = BF16_SUBLANES


def _ffn_body(h_ref, hp_ref, g_ref, wg_ref, wv_ref, cwg_ref, cwv_ref, cbg_ref, cbv_ref, wd_ref,
              o_ref, xn_ref, ug_ref, uv_ref, acc_ref, *, tm, seq):
    i = pl.program_id(0)
    j = pl.program_id(1)

    @pl.when(j == 0)
    def _():
        xn_ref[FFN_HALO:, :] = _rms(h_ref[...], g_ref[...]).astype(BF16)
        prev = _rms(hp_ref[...], g_ref[...])
        seq_start = (i * tm) % seq == 0
        xn_ref[:FFN_HALO, :] = jnp.where(seq_start, 0.0, prev).astype(BF16)
        acc_ref[...] = jnp.zeros_like(acc_ref)

    xn = xn_ref[...]
    ug_ref[...] = jnp.dot(xn, wg_ref[...], preferred_element_type=F32)
    uv_ref[...] = jnp.dot(xn, wv_ref[...], preferred_element_type=F32)

    def conv(u_ref, cw_ref, cb_ref):
        y = cb_ref[...]
        for t in range(CONV_WIDTH):
            off = FFN_HALO - (CONV_WIDTH - 1) + t
            y = y + cw_ref[t:t + 1, :] * u_ref[off:off + tm, :]
        return y

    yg = conv(ug_ref, cwg_ref, cbg_ref)
    yv = conv(uv_ref, cwv_ref, cbv_ref)
    a = (jax.nn.silu(yg) * yv).astype(BF16)
    acc_ref[...] += jnp.dot(a, wd_ref[...], preferred_element_type=F32)

    @pl.when(j == pl.num_programs(1) - 1)
    def _():
        o_ref[...] = h_ref[...] + acc_ref[...]


def _ffn(h, seq, gain, w_up, conv_w, conv_b, w_down, *, tm=512, tf=512):
    t, d = h.shape
    dff = w_down.shape[0]
    tm = min(tm, seq)
    assert seq % tm == 0 and dff % tf == 0 and tm % FFN_HALO == 0
    nf = dff // tf
    hb = tm // FFN_HALO
    conv_b2 = conv_b.reshape(1, 2 * dff)
    return pl.pallas_call(
        functools.partial(_ffn_body, tm=tm, seq=seq),
        grid=(t // tm, nf),
        in_specs=[
            pl.BlockSpec((tm, d), lambda i, j: (i, 0)),
            pl.BlockSpec((FFN_HALO, d), lambda i, j: (jnp.maximum(i * hb - 1, 0), 0)),
            pl.BlockSpec((1, d), lambda i, j: (0, 0)),
            pl.BlockSpec((d, tf), lambda i, j: (0, j)),
            pl.BlockSpec((d, tf), lambda i, j: (0, nf + j)),
            pl.BlockSpec((CONV_WIDTH, tf), lambda i, j: (0, j)),
            pl.BlockSpec((CONV_WIDTH, tf), lambda i, j: (0, nf + j)),
            pl.BlockSpec((1, tf), lambda i, j: (0, j)),
            pl.BlockSpec((1, tf), lambda i, j: (0, nf + j)),
            pl.BlockSpec((tf, d), lambda i, j: (j, 0)),
        ],
        out_specs=pl.BlockSpec((tm, d), lambda i, j: (i, 0)),
        out_shape=jax.ShapeDtypeStruct((t, d), F32),
        scratch_shapes=[
            pltpu.VMEM((tm + FFN_HALO, d), BF16),
            pltpu.VMEM((tm + FFN_HALO, tf), F32),
            pltpu.VMEM((tm + FFN_HALO, tf), F32),
            pltpu.VMEM((tm, d), F32),
        ],
        compiler_params=_params("arbitrary", "arbitrary"),
        name="ffn",
    )(h, h, gain.reshape(1, d), w_up, w_up, conv_w, conv_w, conv_b2, conv_b2, w_down)


def _rope_cos_sin(positions, rot):
    inv = ROPE_THETA ** (-jnp.arange(0, rot, 2, dtype=F32) / rot)
    ang = positions.astype(F32)[..., None] * inv
    return jnp.cos(ang), jnp.sin(ang)


def _rope_lane_tables(positions, rot):
    c, s = _rope_cos_sin(positions, rot)
    half = rot // 2
    shp = c.shape[:-1]
    cf = jnp.concatenate([c, c, jnp.ones(shp + (LANES - rot,), F32)], -1)
    sa = jnp.concatenate([-s, jnp.zeros(shp + (LANES - half,), F32)], -1)
    sb = jnp.concatenate([jnp.zeros(shp + (half,), F32), s, jnp.zeros(shp + (LANES - rot,), F32)], -1)
    return jnp.concatenate([cf, sa, sb], -1)


def _rope_glue(x, cos, sin):
    half = cos.shape[-1]
    x1, x2 = x[..., :half], x[..., half:2 * half]
    return jnp.concatenate([x1 * cos - x2 * sin, x1 * sin + x2 * cos, x[..., 2 * half:]], -1)


def _chunks_t(x, kc):
    s, d = x.shape[-2:]
    lead = x.shape[:-2]
    return jnp.swapaxes(x.reshape(lead + (s // kc, kc, d)), -1, -2)


DSA_QX = 384
DSA_QPAD = MXU_DIM
DSA_HEAD_GROUP = 8
DSA_CNT_ROWS = 4 * SUBLANES


def _dsa_body(q_ref, qi_ref, wi_ref, tab_ref, kidx_ref, kext_ref, ct_ref, wk_ref, wuv_ref, o_ref,
              qx_ref, qis_ref, sc_ref, m_ref, l_ref, acc_ref, *, kc, n_keep):
    i = pl.program_id(1)
    q0 = i * Q_BLOCK
    nc = (q0 + Q_BLOCK + kc - 1) // kc
    scale = DSA_QK_DIM ** -0.5 * LOG2E
    tab = tab_ref[0]
    cf, sa, sb = tab[:, 0:LANES], tab[:, LANES:2 * LANES], tab[:, 2 * LANES:3 * LANES]
    cfi, sai, sbi = tab[:, 3 * LANES:4 * LANES], tab[:, 4 * LANES:5 * LANES], tab[:, 5 * LANES:6 * LANES]

    for h in range(DSA_HEADS):
        xh = q_ref[0, :, h * DSA_QPAD:(h + 1) * DSA_QPAD].astype(F32)
        r0 = _rope_first_vreg(xh[:, :LANES], cf, sa, sb, DSA_ROPE // 2)
        qh = jnp.concatenate([r0, xh[:, LANES:]], axis=1).astype(BF16)
        qx = jnp.dot(qh, wk_ref[h], preferred_element_type=F32) * scale
        qx_ref[h * Q_BLOCK:(h + 1) * Q_BLOCK, :] = qx.astype(BF16)
    for h in range(IDX_HEADS):
        xi = qi_ref[0, :, h * IDX_DIM:(h + 1) * IDX_DIM].astype(F32)
        qis_ref[h * Q_BLOCK:(h + 1) * Q_BLOCK, :] = _rope_first_vreg(xi, cfi, sai, sbi, IDX_ROPE // 2).astype(BF16)

    wi = wi_ref[0, 0]
    key_i = lax.broadcasted_iota(jnp.int32, (kc, Q_BLOCK), 0)
    t_l = q0 + lax.broadcasted_iota(jnp.int32, (kc, Q_BLOCK), 1)

    def score_chunk(c, carry):
        lo, hi = carry
        k0 = pl.multiple_of(c * kc, kc)
        d = _nt(kidx_ref[0, pl.ds(k0, kc), :], qis_ref[...])
        d = jnp.maximum(d, 0.0)
        sc = d[:, 0:Q_BLOCK] * wi[0:1, :]
        for h in range(1, IDX_HEADS):
            sc = sc + d[:, h * Q_BLOCK:(h + 1) * Q_BLOCK] * wi[h:h + 1, :]
        sc = sc + 0.0
        causal = (k0 + key_i) <= t_l
        sc_ref[c] = jnp.where(causal, sc, -jnp.inf)
        lo = jnp.minimum(lo, jnp.where(causal, sc, jnp.inf).min(axis=0, keepdims=True))
        hi = jnp.maximum(hi, jnp.where(causal, sc, -jnp.inf).max(axis=0, keepdims=True))
        return lo, hi

    lo, hi = lax.fori_loop(0, nc, score_chunk,
                           (jnp.full((1, Q_BLOCK), jnp.inf, F32), jnp.full((1, Q_BLOCK), -jnp.inf, F32)))

    def bisect_more(carry):
        it, _, _, n_lo = carry
        return (it < BISECT_ITERS) & (jnp.max(n_lo) > n_keep)

    def bisect(carry):
        it, lo, hi, n_lo = carry
        mid = lo + (hi - lo) * 0.5

        def count_chunk(c, cnt):
            ge = jnp.where(sc_ref[c] >= mid, 1.0, 0.0)
            return cnt + ge.reshape(kc // DSA_CNT_ROWS, DSA_CNT_ROWS, Q_BLOCK).sum(axis=0)

        cnt = lax.fori_loop(0, nc, count_chunk, jnp.zeros((DSA_CNT_ROWS, Q_BLOCK), F32)).sum(axis=0, keepdims=True)
        ge = cnt >= n_keep
        return it + 1, jnp.where(ge, mid, lo), jnp.where(ge, hi, mid), jnp.where(ge, cnt, n_lo)

    n_causal = (q0 + 1 + lax.broadcasted_iota(jnp.int32, (1, Q_BLOCK), 1)).astype(F32)
    _, lo, hi, _ = lax.while_loop(bisect_more, bisect, (0, lo, hi, n_causal))

    def bias_chunk(c, _):
        sc_ref[c] = jnp.where(sc_ref[c] >= lo, 0.0, MASK_NEG)
        return 0

    lax.fori_loop(0, nc, bias_chunk, 0)

    m_ref[...] = jnp.full_like(m_ref, MASK_NEG)
    l_ref[...] = jnp.zeros_like(l_ref)
    acc_ref[...] = jnp.zeros_like(acc_ref)
    gl = DSA_HEAD_GROUP * Q_BLOCK

    def attn_chunk(c, _):
        k0 = pl.multiple_of(c * kc, kc)
        kx = kext_ref[0, pl.ds(k0, kc), :]
        ct = ct_ref[0, c]
        bias = jnp.tile(sc_ref[c], (1, DSA_HEAD_GROUP))
        n_slab = DSA_HEADS // DSA_HEAD_GROUP
        s = _nt(kx, qx_ref[0:gl, :]) + bias
        for g in range(n_slab):
            s_next = _nt(kx, qx_ref[(g + 1) * gl:(g + 2) * gl, :]) + bias if g + 1 < n_slab else None
            _online_softmax_step(s, ct, m_ref, l_ref, acc_ref, slice(g * gl, (g + 1) * gl))
            s = s_next
        return 0

    lax.fori_loop(0, nc, attn_chunk, 0)

    for h in range(DSA_HEADS):
        cols = slice(h * Q_BLOCK, (h + 1) * Q_BLOCK)
        o_lat = (acc_ref[:, cols] / l_ref[:, cols]).astype(BF16)
        o_ref[0, :, h * DSA_V_DIM:(h + 1) * DSA_V_DIM] = _tn(o_lat, wuv_ref[h]).astype(o_ref.dtype)


def _dsa_attention(qall, wi_l, tabs, kidx, kext, ct, wk, wuv, *, kc):
    b, s, _ = qall.shape
    assert s % kc == 0 and kc % Q_BLOCK == 0
    n_keep = min(DSA_TOPK_MAX, s // 4)
    nq = DSA_HEADS * DSA_QPAD
    ni = IDX_HEADS * IDX_DIM
    assert nq % ni == 0
    nrow = DSA_HEADS * Q_BLOCK
    return pl.pallas_call(
        functools.partial(_dsa_body, kc=kc, n_keep=float(n_keep)),
        grid=(b, s // Q_BLOCK),
        in_specs=[
            pl.BlockSpec((1, Q_BLOCK, nq), lambda bb, i: (bb, i, 0)),
            pl.BlockSpec((1, Q_BLOCK, ni), lambda bb, i: (bb, i, nq // ni)),
            pl.BlockSpec((1, 1, IDX_HEADS, Q_BLOCK), lambda bb, i: (bb, i, 0, 0)),
            pl.BlockSpec((1, Q_BLOCK, 6 * LANES), lambda bb, i: (bb, i, 0)),
            pl.BlockSpec((1, s, IDX_DIM), lambda bb, i: (bb, 0, 0)),
            pl.BlockSpec((1, s, DSA_QX), lambda bb, i: (bb, 0, 0)),
            pl.BlockSpec((1, s // kc, DSA_KV_RANK, kc), lambda bb, i: (bb, 0, 0, 0)),
            pl.BlockSpec((DSA_HEADS, DSA_QPAD, DSA_QX), lambda bb, i: (0, 0, 0)),
            pl.BlockSpec((DSA_HEADS, DSA_KV_RANK, DSA_V_DIM), lambda bb, i: (0, 0, 0)),
        ],
        out_specs=pl.BlockSpec((1, Q_BLOCK, DSA_HEADS * DSA_V_DIM), lambda bb, i: (bb, i, 0)),
        out_shape=jax.ShapeDtypeStruct((b, s, DSA_HEADS * DSA_V_DIM), BF16),
        scratch_shapes=[
            pltpu.VMEM((nrow, DSA_QX), BF16),
            pltpu.VMEM((IDX_HEADS * Q_BLOCK, IDX_DIM), BF16),
            pltpu.VMEM((s // kc, kc, Q_BLOCK), F32),
            pltpu.VMEM((1, nrow), F32),
            pltpu.VMEM((1, nrow), F32),
            pltpu.VMEM((DSA_KV_RANK, nrow), F32),
        ],
        compiler_params=_params("arbitrary", "arbitrary"),
        name="dsa_attn",
    )(qall, qall, wi_l, tabs, kidx, kext, ct, wk, wuv)


DSA_PROJ_W = (DSA_Q_RANK, DSA_KV_RANK, LANES, LANES, LANES)


def _dsa_kprep_body(ckv_ref, kpe_ref, kid_ref, wid_ref, tab_ref, kvn_ref, lng_ref, lnb_ref,
                    kext_ref, kidx_ref, ct_ref, wi_ref, *, tm):
    tab = tab_ref[...]
    cf, sa, sb = tab[:, 0:LANES], tab[:, LANES:2 * LANES], tab[:, 2 * LANES:3 * LANES]
    cfi, sai, sbi = tab[:, 3 * LANES:4 * LANES], tab[:, 4 * LANES:5 * LANES], tab[:, 5 * LANES:6 * LANES]
    c_n = _rms(ckv_ref[...], kvn_ref[...])
    kext_ref[:, :DSA_KV_RANK] = c_n.astype(BF16)
    kext_ref[:, DSA_KV_RANK:] = _rope_first_vreg(kpe_ref[...], cf, sa, sb, DSA_ROPE // 2).astype(BF16)
    ct_ref[0, 0] = c_n.T.astype(BF16)
    x = kid_ref[...]
    mu = jnp.mean(x, axis=-1, keepdims=True)
    var = jnp.mean(jnp.square(x - mu), axis=-1, keepdims=True)
    k_ln = (x - mu) * lax.rsqrt(var + NORM_EPS) * lng_ref[...] + lnb_ref[...]
    kidx_ref[...] = _rope_first_vreg(k_ln, cfi, sai, sbi, IDX_ROPE // 2).astype(BF16)
    w = wid_ref[...] * (IDX_HEADS ** -0.5 * IDX_DIM ** -0.5)
    for r in range(tm // Q_BLOCK):
        wi_ref[0, r] = w[r * Q_BLOCK:(r + 1) * Q_BLOCK, :].T[:IDX_HEADS, :]


def _dsa_kprep(proj, tabs, kv_norm, ln_g, ln_b, bsz, seq, kc):
    t = proj.shape[0]
    tm = kc
    nt = seq // tm
    col = [int(c) for c in np.cumsum((0,) + DSA_PROJ_W)]
    assert all(c % w == 0 for c, w in zip(col[1:-1], DSA_PROJ_W[1:]))

    def row(v):
        return v.reshape(1, -1)

    return pl.pallas_call(
        functools.partial(_dsa_kprep_body, tm=tm),
        grid=(t // tm,),
        in_specs=[
            pl.BlockSpec((tm, DSA_KV_RANK), lambda i: (i, col[1] // DSA_KV_RANK)),
            pl.BlockSpec((tm, LANES), lambda i: (i, col[2] // LANES)),
            pl.BlockSpec((tm, LANES), lambda i: (i, col[3] // LANES)),
            pl.BlockSpec((tm, LANES), lambda i: (i, col[4] // LANES)),
            pl.BlockSpec((tm, 6 * LANES), lambda i: (i, 0)),
            pl.BlockSpec((1, DSA_KV_RANK), lambda i: (0, 0)),
            pl.BlockSpec((1, IDX_DIM), lambda i: (0, 0)),
            pl.BlockSpec((1, IDX_DIM), lambda i: (0, 0)),
        ],
        out_specs=[
            pl.BlockSpec((tm, DSA_QX), lambda i: (i, 0)),
            pl.BlockSpec((tm, IDX_DIM), lambda i: (i, 0)),
            pl.BlockSpec((1, 1, DSA_KV_RANK, tm), lambda i: (i // nt, i % nt, 0, 0)),
            pl.BlockSpec((1, tm // Q_BLOCK, IDX_HEADS, Q_BLOCK), lambda i: (i // nt, i % nt, 0, 0)),
        ],
        out_shape=[
            jax.ShapeDtypeStruct((t, DSA_QX), BF16),
            jax.ShapeDtypeStruct((t, IDX_DIM), BF16),
            jax.ShapeDtypeStruct((bsz, nt, DSA_KV_RANK, tm), BF16),
            jax.ShapeDtypeStruct((bsz, seq // Q_BLOCK, IDX_HEADS, Q_BLOCK), F32),
        ],
        compiler_params=_params("arbitrary"),
        name="dsa_kprep",
    )(proj, proj, proj, proj, tabs, row(kv_norm), row(ln_g), row(ln_b))


def _dsa_mixer(h2, bsz, seq, positions, attn_norm, w_in, q_norm, kv_norm, idx_ln_g, idx_ln_b,
               w_qup, w_uk, w_uv, w_out, *, kc=512):
    t = h2.shape[0]
    kc = min(kc, seq)
    sizes = (DSA_Q_RANK, DSA_KV_RANK, DSA_ROPE, IDX_DIM, IDX_HEADS)
    cuts = np.cumsum((0,) + sizes)
    w_in_p = jnp.concatenate(
        [jnp.pad(w_in[:, int(a):int(b)], ((0, 0), (0, wp - (int(b) - int(a)))))
         for a, b, wp in zip(cuts[:-1], cuts[1:], DSA_PROJ_W)], axis=1).astype(BF16)
    proj = _mm(h2, w_in_p, gain=attn_norm, tm=1024, tn=w_in_p.shape[1])

    nq = DSA_HEADS * DSA_QK_DIM
    w_q = w_qup[:, :nq].reshape(DSA_Q_RANK, DSA_HEADS, DSA_QK_DIM)
    w_q = jnp.pad(w_q, ((0, 0), (0, 0), (0, DSA_QPAD - DSA_QK_DIM))).reshape(DSA_Q_RANK, DSA_HEADS * DSA_QPAD)
    w_qp = jnp.concatenate([w_q, w_qup[:, nq:]], axis=1).astype(BF16)
    qall = _mm(proj, w_qp, gain=q_norm, out_dtype=BF16, tn=w_qp.shape[1])

    pos = positions.reshape(t)
    tabs = jnp.concatenate([_rope_lane_tables(pos, DSA_ROPE), _rope_lane_tables(pos, IDX_ROPE)], -1)
    kext, kidx, ct, wi = _dsa_kprep(proj, tabs, kv_norm, idx_ln_g, idx_ln_b, bsz, seq, kc)

    wk = jnp.zeros((DSA_HEADS, DSA_QPAD, DSA_QX), F32)
    wk = wk.at[:, DSA_ROPE:DSA_QK_DIM, :DSA_KV_RANK].set(jnp.swapaxes(w_uk, 1, 2))
    eye = jnp.eye(DSA_ROPE, dtype=F32)
    wk = wk.at[:, :DSA_ROPE, DSA_KV_RANK:DSA_KV_RANK + DSA_ROPE].set(jnp.broadcast_to(eye, (DSA_HEADS,) + eye.shape))

    o = _dsa_attention(
        qall.reshape(bsz, seq, -1), wi, tabs.reshape(bsz, seq, -1),
        kidx.reshape(bsz, seq, IDX_DIM), kext.reshape(bsz, seq, DSA_QX), ct,
        wk.astype(BF16), w_uv.astype(BF16), kc=kc)
    return _mm(o.reshape(t, -1), w_out.astype(BF16), res=h2, tm=1024)


FOX_KX = MXU_DIM
FOX_HEAD_PAIR = 2


def _fox_body(q_ref, k_ref, c3_ref, e_ref, vt_ref, o_ref, qx_ref, kx_ref, m_ref, l_ref, acc_ref, *, tq, kc):
    i = pl.program_id(2)
    q0 = i * tq
    n_full = q0 // kc
    n_diag = tq // kc

    @pl.when(i == 0)
    def _():
        for a in range(FOX_HEAD_PAIR):
            kx_ref[a, :, :FOX_DIM] = k_ref[0, :, a * FOX_DIM:(a + 1) * FOX_DIM]
            kx_ref[a, :, FOX_DIM:] = _tn(c3_ref[0, a], e_ref[...]).astype(BF16)

    lane = lax.broadcasted_iota(jnp.int32, (tq, FOX_KX - FOX_DIM), 1)
    for a in range(FOX_HEAD_PAIR):
        qx_ref[a, :, :FOX_DIM] = q_ref[0, :, a * FOX_DIM:(a + 1) * FOX_DIM]
        qx_ref[a, :, FOX_DIM:] = jnp.where(lane < 3, 1.0, 0.0).astype(BF16)
    m_ref[...] = jnp.full_like(m_ref, MASK_NEG)
    l_ref[...] = jnp.zeros_like(l_ref)
    acc_ref[...] = jnp.zeros_like(acc_ref)
    key_i = lax.broadcasted_iota(jnp.int32, (kc, tq), 0)
    t_l = q0 + lax.broadcasted_iota(jnp.int32, (kc, tq), 1)

    def step(c, masked):
        k0 = pl.multiple_of(c * kc, kc)

        def logits(a):
            s = _nt(kx_ref[a, pl.ds(k0, kc), :], qx_ref[a])
            return jnp.where((k0 + key_i) <= t_l, s, MASK_NEG) if masked else s

        s = logits(0)
        for a in range(FOX_HEAD_PAIR):
            s_next = logits(a + 1) if a + 1 < FOX_HEAD_PAIR else None
            _online_softmax_step(s, vt_ref[0, a, c], m_ref, l_ref, acc_ref, slice(a * tq, (a + 1) * tq))
            s = s_next

    def full_chunk(c, _):
        step(c, False)
        return 0

    def diag_chunk(c, _):
        step(c, True)
        return 0

    lax.fori_loop(0, n_full, full_chunk, 0)
    lax.fori_loop(n_full, n_full + n_diag, diag_chunk, 0)
    for a in range(FOX_HEAD_PAIR):
        cols = slice(a * tq, (a + 1) * tq)
        o_ref[0, :, a * FOX_DIM:(a + 1) * FOX_DIM] = (acc_ref[:, cols] / l_ref[:, cols]).T.astype(o_ref.dtype)


def _fox_attention(qkv, c3, vt, *, tq, kc):
    b, s, _ = qkv.shape
    hp = FOX_HEAD_PAIR
    npair = FOX_HEADS // hp
    assert s % tq == 0 and tq % kc == 0 and FOX_HEADS % hp == 0
    e = jnp.eye(SUBLANES, FOX_KX - FOX_DIM, dtype=BF16)
    return pl.pallas_call(
        functools.partial(_fox_body, tq=tq, kc=kc),
        grid=(b, npair, s // tq),
        in_specs=[
            pl.BlockSpec((1, tq, hp * FOX_DIM), lambda bb, h, i: (bb, i, h)),
            pl.BlockSpec((1, s, hp * FOX_DIM), lambda bb, h, i: (bb, 0, npair + h)),
            pl.BlockSpec((1, hp, SUBLANES, s), lambda bb, h, i: (bb, h, 0, 0)),
            pl.BlockSpec((SUBLANES, FOX_KX - FOX_DIM), lambda bb, h, i: (0, 0)),
            pl.BlockSpec((1, hp, s // kc, FOX_DIM, kc), lambda bb, h, i: (bb, h, 0, 0, 0)),
        ],
        out_specs=pl.BlockSpec((1, tq, hp * FOX_DIM), lambda bb, h, i: (bb, i, h)),
        out_shape=jax.ShapeDtypeStruct((b, s, FOX_HEADS * FOX_DIM), BF16),
        scratch_shapes=[pltpu.VMEM((hp, tq, FOX_KX), BF16), pltpu.VMEM((hp, s, FOX_KX), BF16),
                        pltpu.VMEM((1, hp * tq), F32), pltpu.VMEM((1, hp * tq), F32),
                        pltpu.VMEM((FOX_DIM, hp * tq), F32)],
        compiler_params=_params("arbitrary", "arbitrary", "arbitrary"),
        name="fox_attn",
    )(qkv, qkv, c3, e, vt)


def _fox_mixer(h2, bsz, seq, attn_norm, w_in, b_f, w_out, *, tq=512, kc=512):
    t = h2.shape[0]
    hd = FOX_HEADS * FOX_DIM
    tq = min(tq, seq)
    kc = min(kc, tq)
    scale = FOX_DIM ** -0.5 * LOG2E
    w_qkv = jnp.concatenate([w_in[:, :hd] * scale, w_in[:, hd:3 * hd]], axis=1).astype(BF16)
    qkv = _mm(h2, w_qkv, gain=attn_norm, out_dtype=BF16, tm=1024, tn=1024)
    w_f = jnp.pad(w_in[:, 3 * hd:], ((0, 0), (0, LANES - FOX_HEADS))).astype(BF16)
    f_logit = _mm(h2, w_f, gain=attn_norm)[:, :FOX_HEADS]
    log_f = jax.nn.log_sigmoid(f_logit + b_f)
    cum = jnp.cumsum(log_f.reshape(bsz, seq, FOX_HEADS), axis=1)
    c3 = jnp.stack(_split3_bits(-cum * LOG2E), axis=0).transpose(1, 3, 0, 2)
    c3 = jnp.pad(c3, ((0, 0), (0, 0), (0, SUBLANES - 3), (0, 0)))
    qkv4 = qkv.reshape(bsz, seq, 3, FOX_HEADS, FOX_DIM)
    vt = _chunks_t(qkv4[:, :, 2].transpose(0, 2, 1, 3), kc)
    o = _fox_attention(qkv.reshape(bsz, seq, 3 * hd), c3, vt, tq=tq, kc=kc)
    return _mm(o.reshape(t, hd), w_out.astype(BF16), res=h2, tm=1024)


NSA_HEAD_GROUP = 4


def _cmp_body(x_ref, pe_ref, w1_ref, b1_ref, w2_ref, o_ref, *, half):
    x = x_ref[0, 0]
    w_lo = w1_ref[:half, :]
    w_hi = w1_ref[half:, :]
    a = jnp.dot(x, w_lo, preferred_element_type=F32)
    bnext = jnp.dot(x, w_hi, preferred_element_type=F32)
    n = x.shape[0]
    bnext = pltpu.roll(bnext, n - 1, 0)
    pe = pe_ref[...]
    pe_b = (jnp.dot(pe[:, :half], w_lo, preferred_element_type=F32)
            + jnp.dot(pe[:, half:], w_hi, preferred_element_type=F32))[0:1, :]
    hid = jax.nn.gelu(a + bnext + pe_b + b1_ref[...])
    o_ref[0, 0] = jnp.dot(hid.astype(BF16), w2_ref[...], preferred_element_type=F32).astype(o_ref.dtype)


def _compress(x, pe, w1, b1, w2):
    b, g, n, kd = x.shape
    dout = w2.shape[1]
    pe8 = jnp.broadcast_to(pe.reshape(1, 2 * kd), (8, 2 * kd)).astype(BF16)
    return pl.pallas_call(
        functools.partial(_cmp_body, half=kd),
        grid=(b, g),
        in_specs=[
            pl.BlockSpec((1, 1, n, kd), lambda bb, gg: (bb, gg, 0, 0)),
            pl.BlockSpec((8, 2 * kd), lambda bb, gg: (0, 0)),
            pl.BlockSpec((2 * kd, CMP_HIDDEN), lambda bb, gg: (0, 0)),
            pl.BlockSpec((1, CMP_HIDDEN), lambda bb, gg: (0, 0)),
            pl.BlockSpec((CMP_HIDDEN, dout), lambda bb, gg: (0, 0)),
        ],
        out_specs=pl.BlockSpec((1, 1, n, dout), lambda bb, gg: (bb, gg, 0, 0)),
        out_shape=jax.ShapeDtypeStruct((b, g, n, dout), BF16),
        compiler_params=_params("arbitrary", "arbitrary"),
        name="nsa_compress",
    )(x, pe8, w1.astype(BF16), b1.reshape(1, CMP_HIDDEN), w2.astype(BF16))


def _nsa_body(q_ref, tab_ref, gate_ref, kc_ref, vct_ref, ks_ref, vst_ref, kw_ref, vwt_ref, ov_ref, exp_ref,
              o_ref, qs_ref, m_ref, l_ref, acc_ref, *, kc, n_cmp, n_slc, n_sel):
    i = pl.program_id(2)
    q0 = i * Q_BLOCK
    nc = (q0 + Q_BLOCK + kc - 1) // kc
    jh = NSA_HPG
    scale = NSA_QK_DIM ** -0.5 * LOG2E
    tab = tab_ref[0]
    cf, sa, sb = tab[:, 0:LANES], tab[:, LANES:2 * LANES], tab[:, 2 * LANES:3 * LANES]

    for j in range(jh):
        xj = q_ref[0, :, j * NSA_QK_DIM:(j + 1) * NSA_QK_DIM].astype(F32)
        r0 = _rope_first_vreg(xj[:, :LANES], cf, sa, sb, NSA_ROPE // 2)
        qs_ref[j * Q_BLOCK:(j + 1) * Q_BLOCK, :] = (jnp.concatenate([r0, xj[:, LANES:]], axis=1) * scale).astype(BF16)
    qs = qs_ref[...]

    n_id = lax.broadcasted_iota(jnp.int32, (n_cmp, Q_BLOCK), 0)
    t_c = q0 + lax.broadcasted_iota(jnp.int32, (n_cmp, Q_BLOCK), 1)
    cbias = jnp.where((n_id * CMP_STRIDE + (CMP_BLOCK - 1)) <= t_c, 0.0, MASK_NEG)
    pc = _softmax_cols(_nt(kc_ref[0, 0], qs) + jnp.tile(cbias, (1, jh)))
    o_c = jnp.dot(vct_ref[0, 0], pc.astype(BF16), preferred_element_type=F32)

    pcs = pc[:, 0:Q_BLOCK]
    for j in range(1, jh):
        pcs = pcs + pc[:, j * Q_BLOCK:(j + 1) * Q_BLOCK]
    ov = ov_ref[...]
    imp = sum(jnp.dot(ov, term, preferred_element_type=F32) for term in _split3(pcs))

    blk = lax.broadcasted_iota(jnp.int32, (n_slc, Q_BLOCK), 0)
    t_b = q0 + lax.broadcasted_iota(jnp.int32, (n_slc, Q_BLOCK), 1)
    cur = lax.shift_right_logical(t_b, int(np.log2(SLC_BLOCK)))
    causal_blk = blk * SLC_BLOCK <= t_b
    forced = (blk == 0) | (blk == cur) | (blk == cur - 1)
    val = jnp.where(causal_blk, jnp.where(forced, jnp.inf, imp), -jnp.inf)
    rank = jnp.zeros((n_slc, Q_BLOCK), F32)
    for mp in range(n_slc):
        vrow = val[mp:mp + 1, :]
        before = (vrow > val) | ((vrow == val) & (blk > mp))
        rank = rank + jnp.where(before, 1.0, 0.0)
    sel = jnp.where((rank < n_sel) & causal_blk, 1.0, 0.0)
    sel_p = jnp.concatenate([sel, jnp.zeros((LANES - n_slc, Q_BLOCK), F32)], axis=0).astype(BF16)

    m_ref[...] = jnp.full_like(m_ref, MASK_NEG)
    l_ref[...] = jnp.zeros_like(l_ref)
    acc_ref[...] = jnp.zeros_like(acc_ref)
    key_i = lax.broadcasted_iota(jnp.int32, (kc, Q_BLOCK), 0)
    t_k = q0 + lax.broadcasted_iota(jnp.int32, (kc, Q_BLOCK), 1)
    gl = NSA_HEAD_GROUP * Q_BLOCK

    def sel_chunk(c, _):
        k0 = pl.multiple_of(c * kc, kc)
        hit = jnp.dot(exp_ref[pl.ds(k0, kc), :], sel_p, preferred_element_type=F32)
        bias = jnp.where((hit > 0.5) & ((k0 + key_i) <= t_k), 0.0, MASK_NEG)
        bias = jnp.tile(bias, (1, NSA_HEAD_GROUP))
        ksc = ks_ref[0, 0, pl.ds(k0, kc), :]
        vt = vst_ref[0, 0, c]
        n_slab = jh // NSA_HEAD_GROUP
        s = _nt(ksc, qs_ref[0:gl, :]) + bias
        for hg in range(n_slab):
            s_next = _nt(ksc, qs_ref[(hg + 1) * gl:(hg + 2) * gl, :]) + bias if hg + 1 < n_slab else None
            _online_softmax_step(s, vt, m_ref, l_ref, acc_ref, slice(hg * gl, (hg + 1) * gl))
            s = s_next
        return 0

    lax.fori_loop(0, nc, sel_chunk, 0)
    o_s = acc_ref[...] / l_ref[...]

    wl = WINDOW + Q_BLOCK
    kstart = pl.multiple_of(q0, Q_BLOCK)
    s_pos = q0 - WINDOW + lax.broadcasted_iota(jnp.int32, (wl, Q_BLOCK), 0)
    t_w = q0 + lax.broadcasted_iota(jnp.int32, (wl, Q_BLOCK), 1)
    wbias = jnp.where((s_pos >= 0) & (s_pos <= t_w) & (s_pos > t_w - WINDOW), 0.0, MASK_NEG)
    pw = _softmax_cols(_nt(kw_ref[0, 0, pl.ds(kstart, wl), :], qs) + jnp.tile(wbias, (1, jh))).astype(BF16)
    o_w = jnp.dot(vwt_ref[0, 0, i], pw[0:Q_BLOCK, :], preferred_element_type=F32)
    for c in range(1, wl // Q_BLOCK):
        o_w = o_w + jnp.dot(vwt_ref[0, 0, i + c], pw[c * Q_BLOCK:(c + 1) * Q_BLOCK, :],
                            preferred_element_type=F32)

    g = gate_ref[0, 0]
    for j in range(jh):
        cj = slice(j * Q_BLOCK, (j + 1) * Q_BLOCK)
        out = (g[j:j + 1, :] * o_c[:, cj] + g[jh + j:jh + j + 1, :] * o_s[:, cj]
               + g[2 * jh + j:2 * jh + j + 1, :] * o_w[:, cj])
        o_ref[0, :, j * NSA_V_DIM:(j + 1) * NSA_V_DIM] = out.T.astype(o_ref.dtype)


def _nsa_attention(q, tabs, gates, kcmp, vcmp_t, ks, vs_t, kw, vw_t, *, kc):
    b, s, _ = q.shape
    g = NSA_GROUPS
    assert s % kc == 0 and kc % SLC_BLOCK == 0
    n_cmp = kcmp.shape[2]
    n_slc = s // SLC_BLOCK
    assert n_slc <= LANES
    n_sel = min(SLC_TOPK, n_slc)
    cmp_start = np.arange(n_cmp) * CMP_STRIDE
    slc_start = np.arange(n_slc) * SLC_BLOCK
    ov = ((cmp_start[None, :] < slc_start[:, None] + SLC_BLOCK)
          & (cmp_start[None, :] + CMP_BLOCK > slc_start[:, None])).astype(np.float32)
    expand = (np.arange(s)[:, None] // SLC_BLOCK == np.arange(LANES)[None, :]).astype(np.float32)
    qw = NSA_HPG * NSA_QK_DIM
    ow = NSA_HPG * NSA_V_DIM
    nrow = NSA_HPG * Q_BLOCK
    wl = s + WINDOW
    return pl.pallas_call(
        functools.partial(_nsa_body, kc=kc, n_cmp=n_cmp, n_slc=n_slc, n_sel=float(n_sel)),
        grid=(b, g, s // Q_BLOCK),
        in_specs=[
            pl.BlockSpec((1, Q_BLOCK, qw), lambda bb, gg, i: (bb, i, gg)),
            pl.BlockSpec((1, Q_BLOCK, 3 * LANES), lambda bb, gg, i: (bb, i, 0)),
            pl.BlockSpec((1, 1, 3 * NSA_HPG, Q_BLOCK), lambda bb, gg, i: (bb, gg, 0, i)),
            pl.BlockSpec((1, 1, n_cmp, NSA_QK_DIM), lambda bb, gg, i: (bb, gg, 0, 0)),
            pl.BlockSpec((1, 1, NSA_V_DIM, n_cmp), lambda bb, gg, i: (bb, gg, 0, 0)),
            pl.BlockSpec((1, 1, s, NSA_QK_DIM), lambda bb, gg, i: (bb, gg, 0, 0)),
            pl.BlockSpec((1, 1, s // kc, NSA_V_DIM, kc), lambda bb, gg, i: (bb, gg, 0, 0, 0)),
            pl.BlockSpec((1, 1, wl, NSA_QK_DIM), lambda bb, gg, i: (bb, gg, 0, 0)),
            pl.BlockSpec((1, 1, wl // Q_BLOCK, NSA_V_DIM, Q_BLOCK), lambda bb, gg, i: (bb, gg, 0, 0, 0)),
            pl.BlockSpec((n_slc, n_cmp), lambda bb, gg, i: (0, 0)),
            pl.BlockSpec((s, LANES), lambda bb, gg, i: (0, 0)),
        ],
        out_specs=pl.BlockSpec((1, Q_BLOCK, ow), lambda bb, gg, i: (bb, i, gg)),
        out_shape=jax.ShapeDtypeStruct((b, s, NSA_HEADS * NSA_V_DIM), BF16),
        scratch_shapes=[
            pltpu.VMEM((nrow, NSA_QK_DIM), BF16),
            pltpu.VMEM((1, nrow), F32),
            pltpu.VMEM((1, nrow), F32),
            pltpu.VMEM((NSA_V_DIM, nrow), F32),
        ],
        compiler_params=_params("arbitrary", "arbitrary", "arbitrary"),
        name="nsa_attn",
    )(q, tabs, gates, kcmp, vcmp_t, ks, vs_t, kw, vw_t, jnp.asarray(ov, BF16), jnp.asarray(expand, BF16))


def _nsa_mixer(h2, bsz, seq, positions, attn_norm, w_in, k_pe, k_w1, k_b1, k_w2, v_pe, v_w1, v_b1, v_w2, w_out,
               *, kc=512):
    t = h2.shape[0]
    kc = min(kc, seq)
    g, jh, dk, dv = NSA_GROUPS, NSA_HPG, NSA_QK_DIM, NSA_V_DIM
    nq = NSA_HEADS * dk
    n_rest = w_in.shape[1] - nq
    n_rest_p = -(-n_rest // 512) * 512
    q = _mm(h2, w_in[:, :nq].astype(BF16), gain=attn_norm, out_dtype=BF16, tm=1024, tn=1024)
    w_rest = jnp.pad(w_in[:, nq:], ((0, 0), (0, n_rest_p - n_rest))).astype(BF16)
    rest = _mm(h2, w_rest, gain=attn_norm, tm=1024, tn=1024)
    offs = np.cumsum([0, NSA_KD, NSA_VD, NSA_KD, NSA_VD, NSA_KD, NSA_VD, 3 * NSA_HEADS])
    k_c, v_c, k_s, v_s, k_w, v_w, g_logit = [rest[:, int(a):int(b)] for a, b in zip(offs[:-1], offs[1:])]

    pos = positions.reshape(t)
    cos, sin = _rope_cos_sin(pos, NSA_ROPE)

    def rope_k(k):
        return _rope_glue(k.reshape(t, g, dk), cos[:, None, :], sin[:, None, :])

    def per_group(x, d):
        return x.reshape(bsz, seq, g, d).transpose(0, 2, 1, 3).astype(BF16)

    def chunks(x, d):
        x = x.reshape(bsz, seq // CMP_STRIDE, CMP_STRIDE, g, d).transpose(0, 3, 1, 2, 4)
        return x.reshape(bsz, g, seq // CMP_STRIDE, CMP_STRIDE * d).astype(BF16)

    kcmp = _compress(chunks(rope_k(k_c), dk), k_pe, k_w1, k_b1, k_w2)
    vcmp_t = jnp.swapaxes(_compress(chunks(v_c.reshape(t, g, dv), dv), v_pe, v_w1, v_b1, v_w2), -1, -2)
    ks = per_group(rope_k(k_s), dk)
    vs_t = _chunks_t(per_group(v_s.reshape(t, g, dv), dv), kc)
    pad = ((0, 0), (0, 0), (WINDOW, 0), (0, 0))
    kw = jnp.pad(per_group(rope_k(k_w), dk), pad)
    vw_t = _chunks_t(jnp.pad(per_group(v_w.reshape(t, g, dv), dv), pad), Q_BLOCK)
    gates = jax.nn.sigmoid(g_logit).reshape(bsz, seq, 3, g, jh).transpose(0, 3, 2, 4, 1).reshape(bsz, g, 3 * jh, seq)
    tabs = _rope_lane_tables(pos, NSA_ROPE).reshape(bsz, seq, 3 * LANES)

    o = _nsa_attention(q.reshape(bsz, seq, nq), tabs, gates, kcmp, vcmp_t, ks, vs_t, kw, vw_t, kc=kc)
    return _mm(o.reshape(t, NSA_HEADS * dv), w_out.astype(BF16), res=h2, tm=1024)


def _norm_body(x_ref, g_ref, o_ref):
    o_ref[...] = _rms(x_ref[...], g_ref[...])


def _final_norm(h2, gain, *, tm=512):
    t, d = h2.shape
    tm = min(tm, t)
    return pl.pallas_call(
        _norm_body,
        grid=(t // tm,),
        in_specs=[pl.BlockSpec((tm, d), lambda i: (i, 0)), pl.BlockSpec((1, d), lambda i: (0, 0))],
        out_specs=pl.BlockSpec((tm, d), lambda i: (i, 0)),
        out_shape=jax.ShapeDtypeStruct((t, d), F32),
        compiler_params=_params("arbitrary"),
        name="final_norm",
    )(h2, gain.reshape(1, d))


def _ffn_layer(h2, seq, ffn_norm, w_up, conv_w, conv_b, w_down):
    return _ffn(h2, seq, ffn_norm, w_up.astype(BF16), conv_w, conv_b, w_down.astype(BF16))


def kernel(x, positions, l0_attn_norm, l0_dsa_w_in, l0_dsa_q_norm, l0_dsa_kv_norm, l0_dsa_idx_ln_g, l0_dsa_idx_ln_b, l0_dsa_w_qup, l0_dsa_w_uk, l0_dsa_w_uv, l0_dsa_w_out, l0_ffn_norm, l0_ffn_up, l0_ffn_conv_w, l0_ffn_conv_b, l0_ffn_down, l1_attn_norm, l1_fox_w_in, l1_fox_b_f, l1_fox_w_out, l1_ffn_norm, l1_ffn_up, l1_ffn_conv_w, l1_ffn_conv_b, l1_ffn_down, l2_attn_norm, l2_nsa_w_in, l2_nsa_k_pe, l2_nsa_k_w1, l2_nsa_k_b1, l2_nsa_k_w2, l2_nsa_v_pe, l2_nsa_v_w1, l2_nsa_v_b1, l2_nsa_v_w2, l2_nsa_w_out, l2_ffn_norm, l2_ffn_up, l2_ffn_conv_w, l2_ffn_conv_b, l2_ffn_down, l3_attn_norm, l3_dsa_w_in, l3_dsa_q_norm, l3_dsa_kv_norm, l3_dsa_idx_ln_g, l3_dsa_idx_ln_b, l3_dsa_w_qup, l3_dsa_w_uk, l3_dsa_w_uv, l3_dsa_w_out, l3_ffn_norm, l3_ffn_up, l3_ffn_conv_w, l3_ffn_conv_b, l3_ffn_down, final_norm):
    bsz, seq, d = x.shape
    h = x.reshape(bsz * seq, d)
    h = _dsa_mixer(h, bsz, seq, positions, l0_attn_norm, l0_dsa_w_in, l0_dsa_q_norm, l0_dsa_kv_norm,
                   l0_dsa_idx_ln_g, l0_dsa_idx_ln_b, l0_dsa_w_qup, l0_dsa_w_uk, l0_dsa_w_uv, l0_dsa_w_out)
    h = _ffn_layer(h, seq, l0_ffn_norm, l0_ffn_up, l0_ffn_conv_w, l0_ffn_conv_b, l0_ffn_down)
    h = _fox_mixer(h, bsz, seq, l1_attn_norm, l1_fox_w_in, l1_fox_b_f, l1_fox_w_out)
    h = _ffn_layer(h, seq, l1_ffn_norm, l1_ffn_up, l1_ffn_conv_w, l1_ffn_conv_b, l1_ffn_down)
    h = _nsa_mixer(h, bsz, seq, positions, l2_attn_norm, l2_nsa_w_in, l2_nsa_k_pe, l2_nsa_k_w1, l2_nsa_k_b1,
                   l2_nsa_k_w2, l2_nsa_v_pe, l2_nsa_v_w1, l2_nsa_v_b1, l2_nsa_v_w2, l2_nsa_w_out)
    h = _ffn_layer(h, seq, l2_ffn_norm, l2_ffn_up, l2_ffn_conv_w, l2_ffn_conv_b, l2_ffn_down)
    h = _dsa_mixer(h, bsz, seq, positions, l3_attn_norm, l3_dsa_w_in, l3_dsa_q_norm, l3_dsa_kv_norm,
                   l3_dsa_idx_ln_g, l3_dsa_idx_ln_b, l3_dsa_w_qup, l3_dsa_w_uk, l3_dsa_w_uv, l3_dsa_w_out)
    h = _ffn_layer(h, seq, l3_ffn_norm, l3_ffn_up, l3_ffn_conv_w, l3_ffn_conv_b, l3_ffn_down)
    return _final_norm(h, final_norm).reshape(bsz, seq, d)
```

```python
import functools

import jax
import jax.numpy as jnp
import numpy as np
from jax import lax
from jax.experimental import pallas as pl
from jax.experimental.pallas import tpu as pltpu

F32 = jnp.float32
BF16 = jnp.bfloat16

ROPE_THETA = 500000.0
NORM_EPS = 1e-6
Q_BLOCK = 128

DSA_HEADS = 32
DSA_Q_RANK = 512
DSA_KV_RANK = 256
DSA_QK_DIM = 192
DSA_ROPE = 48
DSA_NOPE = DSA_QK_DIM - DSA_ROPE
DSA_V_DIM = 128
IDX_HEADS = 16
IDX_DIM = 128
IDX_ROPE = 32
DSA_TOPK_MAX = 256
DSA_IN = DSA_Q_RANK + DSA_KV_RANK + DSA_ROPE + IDX_DIM + IDX_HEADS

FOX_HEADS = 16
FOX_DIM = 128

NSA_HEADS = 48
NSA_GROUPS = 4
NSA_HPG = NSA_HEADS // NSA_GROUPS
NSA_QK_DIM = 192
NSA_ROPE = 48
NSA_V_DIM = 128
CMP_BLOCK = 32
CMP_STRIDE = 16
CMP_HIDDEN = 256
SLC_BLOCK = 64
SLC_TOPK = 16
WINDOW = 512
NSA_KD = NSA_GROUPS * NSA_QK_DIM
NSA_VD = NSA_GROUPS * NSA_V_DIM

CONV_WIDTH = 3

LANES = 128
SUBLANES = 8
BF16_SUBLANES = 16
MXU_DIM = 256
VMEM_LIMIT = 56 * 1024 * 1024

MASK_NEG = -1e30
SUM_ROWS = BF16_SUBLANES
LOG2E = 1.4426950408889634
BISECT_ITERS = 32
BISECT_UNROLL = 4


def _params(*sem):
    return pltpu.CompilerParams(dimension_semantics=sem, vmem_limit_bytes=VMEM_LIMIT)


def _nt(a, b):
    return lax.dot_general(a, b, (((1,), (1,)), ((), ())), preferred_element_type=F32)


def _tn(a, b):
    return lax.dot_general(a, b, (((0,), (0,)), ((), ())), preferred_element_type=F32)


def _rms(x, g):
    return x * lax.rsqrt(jnp.mean(x * x, axis=-1, keepdims=True) + NORM_EPS) * g


def _rope_first_vreg(x0, cf, sa, sb, half):
    return (x0 * cf + pltpu.roll(x0, LANES - half, 1) * sa + pltpu.roll(x0, half, 1) * sb)


def _split3(x):
    hi = x.astype(BF16)
    r1 = x - hi.astype(F32)
    mid = r1.astype(BF16)
    lo = (r1 - mid.astype(F32)).astype(BF16)
    return hi, mid, lo


def _split3_bits(x):
    def trunc(v):
        bits = lax.bitcast_convert_type(v, jnp.uint32) & jnp.uint32(0xFFFF0000)
        return lax.bitcast_convert_type(bits, F32)

    hi = trunc(x)
    r1 = x - hi
    mid = trunc(r1)
    lo = r1 - mid
    return hi.astype(BF16), mid.astype(BF16), lo.astype(BF16)


def _online_softmax_step(s, pv_lhs, m_ref, acc_ref, cols):
    m_prev = m_ref[:, cols]
    m_new = jnp.maximum(m_prev, s.max(axis=0, keepdims=True))
    alpha = jnp.exp2(m_prev - m_new)
    p = jnp.exp2(s - m_new).astype(BF16)
    acc_ref[:, cols] = alpha * acc_ref[:, cols] + jnp.dot(pv_lhs, p, preferred_element_type=F32)
    m_ref[:, cols] = m_new


def _softmax_cols(z):
    mx = z.max(axis=0, keepdims=True)
    e = jnp.exp2(z - mx)
    den = e.sum(axis=0, keepdims=True)
    return e * jnp.where(mx > 0.5 * MASK_NEG, 1.0 / den, 0.0)


def _mm_body(*refs, has_gain, has_res):
    x_ref, w_ref = refs[0], refs[1]
    k = 2
    g_ref = r_ref = None
    if has_gain:
        g_ref = refs[k]
        k += 1
    if has_res:
        r_ref = refs[k]
        k += 1
    o_ref = refs[k]
    if has_gain:
        xn_ref = refs[k + 1]

        @pl.when(pl.program_id(1) == 0)
        def _():
            xn_ref[...] = _rms(x_ref[...].astype(F32), g_ref[...]).astype(BF16)

        a = xn_ref[...]
    else:
        a = x_ref[...].astype(BF16)
    acc = jnp.dot(a, w_ref[...], preferred_element_type=F32)
    if has_res:
        acc = acc + r_ref[...]
    o_ref[...] = acc.astype(o_ref.dtype)


def _mm(x, w, *, gain=None, res=None, out_dtype=F32, x_col_block=0, tm=512, tn=512):
    m = x.shape[0]
    k, n = w.shape
    tm = min(tm, m)
    tn = min(tn, n)
    assert m % tm == 0 and n % tn == 0 and x.shape[1] >= (x_col_block + 1) * k
    in_specs = [pl.BlockSpec((tm, k), lambda i, j: (i, x_col_block)),
                pl.BlockSpec((k, tn), lambda i, j: (0, j))]
    args = [x, w]
    scratch = []
    if gain is not None:
        in_specs.append(pl.BlockSpec((1, k), lambda i, j: (0, 0)))
        args.append(gain.reshape(1, k).astype(F32))
        scratch.append(pltpu.VMEM((tm, k), BF16))
    if res is not None:
        in_specs.append(pl.BlockSpec((tm, tn), lambda i, j: (i, j)))
        args.append(res)
    return pl.pallas_call(
        functools.partial(_mm_body, has_gain=gain is not None, has_res=res is not None),
        grid=(m // tm, n // tn),
        in_specs=in_specs,
        out_specs=pl.BlockSpec((tm, tn), lambda i, j: (i, j)),
        out_shape=jax.ShapeDtypeStruct((m, n), out_dtype),
        scratch_shapes=scratch,
        compiler_params=_params("arbitrary", "arbitrary"),
        name="mm",
    )(*args)


FFN_HALO = BF16_SUBLANES


def _ffn_body(h_ref, hp_ref, g_ref, wg_ref, wv_ref, cwg_ref, cwv_ref, cbg_ref, cbv_ref, wd_ref,
              o_ref, xn_ref, ug_ref, uv_ref, acc_ref, *, tm, seq):
    i = pl.program_id(0)
    j = pl.program_id(1)

    @pl.when(j == 0)
    def _():
        xn_ref[FFN_HALO:, :] = _rms(h_ref[...], g_ref[...]).astype(BF16)
        prev = _rms(hp_ref[...], g_ref[...])
        seq_start = (i * tm) % seq == 0
        xn_ref[:FFN_HALO, :] = jnp.where(seq_start, 0.0, prev).astype(BF16)
        acc_ref[...] = jnp.zeros_like(acc_ref)

    xn = xn_ref[...]
    ug_ref[...] = jnp.dot(xn, wg_ref[...], preferred_element_type=F32)
    uv_ref[...] = jnp.dot(xn, wv_ref[...], preferred_element_type=F32)

    def conv(u_ref, cw_ref, cb_ref):
        y = cb_ref[...]
        for t in range(CONV_WIDTH):
            off = FFN_HALO - (CONV_WIDTH - 1) + t
            y = y + cw_ref[t:t + 1, :] * u_ref[off:off + tm, :]
        return y

    yg = conv(ug_ref, cwg_ref, cbg_ref)
    yv = conv(uv_ref, cwv_ref, cbv_ref)
    a = (jax.nn.silu(yg) * yv).astype(BF16)
    acc_ref[...] += jnp.dot(a, wd_ref[...], preferred_element_type=F32)

    @pl.when(j == pl.num_programs(1) - 1)
    def _():
        o_ref[...] = h_ref[...] + acc_ref[...]


def _ffn(h, seq, gain, w_up, conv_w, conv_b, w_down, *, tm=512, tf=512):
    t, d = h.shape
    dff = w_down.shape[0]
    tm = min(tm, seq)
    assert seq % tm == 0 and dff % tf == 0 and tm % FFN_HALO == 0
    nf = dff // tf
    hb = tm // FFN_HALO
    conv_b2 = conv_b.reshape(1, 2 * dff)
    return pl.pallas_call(
        functools.partial(_ffn_body, tm=tm, seq=seq),
        grid=(t // tm, nf),
        in_specs=[
            pl.BlockSpec((tm, d), lambda i, j: (i, 0)),
            pl.BlockSpec((FFN_HALO, d), lambda i, j: (jnp.maximum(i * hb - 1, 0), 0)),
            pl.BlockSpec((1, d), lambda i, j: (0, 0)),
            pl.BlockSpec((d, tf), lambda i, j: (0, j)),
            pl.BlockSpec((d, tf), lambda i, j: (0, nf + j)),
            pl.BlockSpec((CONV_WIDTH, tf), lambda i, j: (0, j)),
            pl.BlockSpec((CONV_WIDTH, tf), lambda i, j: (0, nf + j)),
            pl.BlockSpec((1, tf), lambda i, j: (0, j)),
            pl.BlockSpec((1, tf), lambda i, j: (0, nf + j)),
            pl.BlockSpec((tf, d), lambda i, j: (j, 0)),
        ],
        out_specs=pl.BlockSpec((tm, d), lambda i, j: (i, 0)),
        out_shape=jax.ShapeDtypeStruct((t, d), F32),
        scratch_shapes=[
            pltpu.VMEM((tm + FFN_HALO, d), BF16),
            pltpu.VMEM((tm + FFN_HALO, tf), F32),
            pltpu.VMEM((tm + FFN_HALO, tf), F32),
            pltpu.VMEM((tm, d), F32),
        ],
        compiler_params=_params("arbitrary", "arbitrary"),
        name="ffn",
    )(h, h, gain.reshape(1, d), w_up, w_up, conv_w, conv_w, conv_b2, conv_b2, w_down)


def _rope_cos_sin(positions, rot):
    inv = ROPE_THETA ** (-jnp.arange(0, rot, 2, dtype=F32) / rot)
    ang = positions.astype(F32)[..., None] * inv
    return jnp.cos(ang), jnp.sin(ang)


def _rope_lane_tables(positions, rot):
    c, s = _rope_cos_sin(positions, rot)
    half = rot // 2
    shp = c.shape[:-1]
    cf = jnp.concatenate([c, c, jnp.ones(shp + (LANES - rot,), F32)], -1)
    sa = jnp.concatenate([-s, jnp.zeros(shp + (LANES - half,), F32)], -1)
    sb = jnp.concatenate([jnp.zeros(shp + (half,), F32), s, jnp.zeros(shp + (LANES - rot,), F32)], -1)
    return jnp.concatenate([cf, sa, sb], -1)


def _rope_glue(x, cos, sin):
    half = cos.shape[-1]
    x1, x2 = x[..., :half], x[..., half:2 * half]
    return jnp.concatenate([x1 * cos - x2 * sin, x1 * sin + x2 * cos, x[..., 2 * half:]], -1)


def _with_sum_rows(vt):
    return jnp.concatenate([vt, jnp.ones(vt.shape[:-2] + (SUM_ROWS, vt.shape[-1]), vt.dtype)], axis=-2)


def _chunks_t(x, kc):
    s, d = x.shape[-2:]
    lead = x.shape[:-2]
    return jnp.swapaxes(x.reshape(lead + (s // kc, kc, d)), -1, -2)


DSA_QX = 384
DSA_QPAD = MXU_DIM
DSA_HEAD_GROUP = 16
DSA_CNT_ROWS = 4 * SUBLANES


def _dsa_body(q_ref, qi_ref, wi_ref, tab_ref, kidx_ref, kext_ref, ct_ref, wk_ref, wuv_ref, o_ref,
              qx_ref, qis_ref, sc_ref, m_ref, acc_ref, *, kc, n_keep):
    i = pl.program_id(1)
    q0 = i * Q_BLOCK
    nc = (q0 + Q_BLOCK + kc - 1) // kc
    scale = DSA_QK_DIM ** -0.5 * LOG2E
    tab = tab_ref[0]
    cf, sa, sb = tab[:, 0:LANES], tab[:, LANES:2 * LANES], tab[:, 2 * LANES:3 * LANES]
    cfi, sai, sbi = tab[:, 3 * LANES:4 * LANES], tab[:, 4 * LANES:5 * LANES], tab[:, 5 * LANES:6 * LANES]

    for h in range(DSA_HEADS):
        xh = q_ref[0, :, h * DSA_QPAD:(h + 1) * DSA_QPAD].astype(F32)
        r0 = _rope_first_vreg(xh[:, :LANES], cf, sa, sb, DSA_ROPE // 2)
        qh = jnp.concatenate([r0, xh[:, LANES:]], axis=1).astype(BF16)
        qx = jnp.dot(qh, wk_ref[h], preferred_element_type=F32) * scale
        qx_ref[h * Q_BLOCK:(h + 1) * Q_BLOCK, :] = qx.astype(BF16)
    for h in range(IDX_HEADS):
        xi = qi_ref[0, :, h * IDX_DIM:(h + 1) * IDX_DIM].astype(F32)
        qis_ref[h * Q_BLOCK:(h + 1) * Q_BLOCK, :] = _rope_first_vreg(xi, cfi, sai, sbi, IDX_ROPE // 2).astype(BF16)

    wi = wi_ref[0, 0]
    key_i = lax.broadcasted_iota(jnp.int32, (kc, Q_BLOCK), 0)
    t_l = q0 + lax.broadcasted_iota(jnp.int32, (kc, Q_BLOCK), 1)

    def score_chunk(c, carry):
        lo, hi = carry
        k0 = pl.multiple_of(c * kc, kc)
        d = _nt(kidx_ref[0, pl.ds(k0, kc), :], qis_ref[...])
        d = jnp.maximum(d, 0.0)
        sc = d[:, 0:Q_BLOCK] * wi[0:1, :]
        for h in range(1, IDX_HEADS):
            sc = sc + d[:, h * Q_BLOCK:(h + 1) * Q_BLOCK] * wi[h:h + 1, :]
        sc = sc + 0.0
        causal = (k0 + key_i) <= t_l
        sc_ref[c] = jnp.where(causal, sc, -jnp.inf)
        lo = jnp.minimum(lo, jnp.where(causal, sc, jnp.inf).min(axis=0, keepdims=True))
        hi = jnp.maximum(hi, jnp.where(causal, sc, -jnp.inf).max(axis=0, keepdims=True))
        return lo, hi

    lo, hi = lax.fori_loop(0, nc, score_chunk,
                           (jnp.full((1, Q_BLOCK), jnp.inf, F32), jnp.full((1, Q_BLOCK), -jnp.inf, F32)))

    def bisect_more(carry):
        it, _, _, n_lo = carry
        return (it < BISECT_ITERS) & (jnp.max(n_lo) > n_keep)

    def bisect(carry):
        it, lo, hi, n_lo = carry

        def count_chunk_at(mid):
            def count_chunk(c, cnt):
                ge = jnp.where(sc_ref[c] >= mid, 1.0, 0.0)
                return cnt + ge.reshape(kc // DSA_CNT_ROWS, DSA_CNT_ROWS, Q_BLOCK).sum(axis=0)
            return count_chunk

        for _ in range(BISECT_UNROLL):
            mid = lo + (hi - lo) * 0.5
            cnt = lax.fori_loop(0, nc, count_chunk_at(mid), jnp.zeros((DSA_CNT_ROWS, Q_BLOCK), F32))
            cnt = cnt.sum(axis=0, keepdims=True)
            ge = cnt >= n_keep
            lo, hi, n_lo = jnp.where(ge, mid, lo), jnp.where(ge, hi, mid), jnp.where(ge, cnt, n_lo)
        return it + BISECT_UNROLL, lo, hi, n_lo

    n_causal = (q0 + 1 + lax.broadcasted_iota(jnp.int32, (1, Q_BLOCK), 1)).astype(F32)
    _, lo, hi, _ = lax.while_loop(bisect_more, bisect, (0, lo, hi, n_causal))

    def bias_chunk(c, _):
        sc_ref[c] = jnp.where(sc_ref[c] >= lo, 0.0, MASK_NEG)
        return 0

    lax.fori_loop(0, nc, bias_chunk, 0)

    m_ref[...] = jnp.full_like(m_ref, MASK_NEG)
    acc_ref[...] = jnp.zeros_like(acc_ref)
    gl = DSA_HEAD_GROUP * Q_BLOCK

    def attn_chunk(c, _):
        k0 = pl.multiple_of(c * kc, kc)
        kx = kext_ref[0, pl.ds(k0, kc), :]
        ct = ct_ref[0, c]
        bias = jnp.tile(sc_ref[c], (1, DSA_HEAD_GROUP))
        n_slab = DSA_HEADS // DSA_HEAD_GROUP
        s = _nt(kx, qx_ref[0:gl, :]) + bias
        for g in range(n_slab):
            s_next = _nt(kx, qx_ref[(g + 1) * gl:(g + 2) * gl, :]) + bias if g + 1 < n_slab else None
            _online_softmax_step(s, ct, m_ref, acc_ref, slice(g * gl, (g + 1) * gl))
            s = s_next
        return 0

    lax.fori_loop(0, nc, attn_chunk, 0)

    for h in range(DSA_HEADS):
        cols = slice(h * Q_BLOCK, (h + 1) * Q_BLOCK)
        o_lat = (acc_ref[:DSA_KV_RANK, cols] / acc_ref[DSA_KV_RANK:DSA_KV_RANK + 1, cols]).astype(BF16)
        o_ref[0, :, h * DSA_V_DIM:(h + 1) * DSA_V_DIM] = _tn(o_lat, wuv_ref[h]).astype(o_ref.dtype)


def _dsa_attention(qall, wi_l, tabs, kidx, kext, ct, wk, wuv, *, kc):
    b, s, _ = qall.shape
    assert s % kc == 0 and kc % Q_BLOCK == 0
    n_keep = min(DSA_TOPK_MAX, s // 4)
    nq = DSA_HEADS * DSA_QPAD
    ni = IDX_HEADS * IDX_DIM
    assert nq % ni == 0
    nrow = DSA_HEADS * Q_BLOCK
    return pl.pallas_call(
        functools.partial(_dsa_body, kc=kc, n_keep=float(n_keep)),
        grid=(b, s // Q_BLOCK),
        in_specs=[
            pl.BlockSpec((1, Q_BLOCK, nq), lambda bb, i: (bb, i, 0)),
            pl.BlockSpec((1, Q_BLOCK, ni), lambda bb, i: (bb, i, nq // ni)),
            pl.BlockSpec((1, 1, IDX_HEADS, Q_BLOCK), lambda bb, i: (bb, i, 0, 0)),
            pl.BlockSpec((1, Q_BLOCK, 6 * LANES), lambda bb, i: (bb, i, 0)),
            pl.BlockSpec((1, s, IDX_DIM), lambda bb, i: (bb, 0, 0)),
            pl.BlockSpec((1, s, DSA_QX), lambda bb, i: (bb, 0, 0)),
            pl.BlockSpec((1, s // kc, DSA_KV_RANK + SUM_ROWS, kc), lambda bb, i: (bb, 0, 0, 0)),
            pl.BlockSpec((DSA_HEADS, DSA_QPAD, DSA_QX), lambda bb, i: (0, 0, 0)),
            pl.BlockSpec((DSA_HEADS, DSA_KV_RANK, DSA_V_DIM), lambda bb, i: (0, 0, 0)),
        ],
        out_specs=pl.BlockSpec((1, Q_BLOCK, DSA_HEADS * DSA_V_DIM), lambda bb, i: (bb, i, 0)),
        out_shape=jax.ShapeDtypeStruct((b, s, DSA_HEADS * DSA_V_DIM), BF16),
        scratch_shapes=[
            pltpu.VMEM((nrow, DSA_QX), BF16),
            pltpu.VMEM((IDX_HEADS * Q_BLOCK, IDX_DIM), BF16),
            pltpu.VMEM((s // kc, kc, Q_BLOCK), F32),
            pltpu.VMEM((1, nrow), F32),
            pltpu.VMEM((DSA_KV_RANK + SUM_ROWS, nrow), F32),
        ],
        compiler_params=_params("arbitrary", "arbitrary"),
        name="dsa_attn",
    )(qall, qall, wi_l, tabs, kidx, kext, ct, wk, wuv)


DSA_PROJ_W = (DSA_Q_RANK, DSA_KV_RANK, LANES, LANES, LANES)


def _dsa_kprep_body(ckv_ref, kpe_ref, kid_ref, wid_ref, tab_ref, kvn_ref, lng_ref, lnb_ref,
                    kext_ref, kidx_ref, ct_ref, wi_ref, *, tm):
    tab = tab_ref[...]
    cf, sa, sb = tab[:, 0:LANES], tab[:, LANES:2 * LANES], tab[:, 2 * LANES:3 * LANES]
    cfi, sai, sbi = tab[:, 3 * LANES:4 * LANES], tab[:, 4 * LANES:5 * LANES], tab[:, 5 * LANES:6 * LANES]
    c_n = _rms(ckv_ref[...], kvn_ref[...])
    kext_ref[:, :DSA_KV_RANK] = c_n.astype(BF16)
    kext_ref[:, DSA_KV_RANK:] = _rope_first_vreg(kpe_ref[...], cf, sa, sb, DSA_ROPE // 2).astype(BF16)
    ct_ref[0, 0, :DSA_KV_RANK, :] = c_n.T.astype(BF16)
    ct_ref[0, 0, DSA_KV_RANK:, :] = jnp.ones((SUM_ROWS, tm), BF16)
    x = kid_ref[...]
    mu = jnp.mean(x, axis=-1, keepdims=True)
    var = jnp.mean(jnp.square(x - mu), axis=-1, keepdims=True)
    k_ln = (x - mu) * lax.rsqrt(var + NORM_EPS) * lng_ref[...] + lnb_ref[...]
    kidx_ref[...] = _rope_first_vreg(k_ln, cfi, sai, sbi, IDX_ROPE // 2).astype(BF16)
    w = wid_ref[...] * (IDX_HEADS ** -0.5 * IDX_DIM ** -0.5)
    for r in range(tm // Q_BLOCK):
        wi_ref[0, r] = w[r * Q_BLOCK:(r + 1) * Q_BLOCK, :].T[:IDX_HEADS, :]


def _dsa_kprep(proj, tabs, kv_norm, ln_g, ln_b, bsz, seq, kc):
    t = proj.shape[0]
    tm = kc
    nt = seq // tm
    col = [int(c) for c in np.cumsum((0,) + DSA_PROJ_W)]
    assert all(c % w == 0 for c, w in zip(col[1:-1], DSA_PROJ_W[1:]))

    def row(v):
        return v.reshape(1, -1)

    return pl.pallas_call(
        functools.partial(_dsa_kprep_body, tm=tm),
        grid=(t // tm,),
        in_specs=[
            pl.BlockSpec((tm, DSA_KV_RANK), lambda i: (i, col[1] // DSA_KV_RANK)),
            pl.BlockSpec((tm, LANES), lambda i: (i, col[2] // LANES)),
            pl.BlockSpec((tm, LANES), lambda i: (i, col[3] // LANES)),
            pl.BlockSpec((tm, LANES), lambda i: (i, col[4] // LANES)),
            pl.BlockSpec((tm, 6 * LANES), lambda i: (i, 0)),
            pl.BlockSpec((1, DSA_KV_RANK), lambda i: (0, 0)),
            pl.BlockSpec((1, IDX_DIM), lambda i: (0, 0)),
            pl.BlockSpec((1, IDX_DIM), lambda i: (0, 0)),
        ],
        out_specs=[
            pl.BlockSpec((tm, DSA_QX), lambda i: (i, 0)),
            pl.BlockSpec((tm, IDX_DIM), lambda i: (i, 0)),
            pl.BlockSpec((1, 1, DSA_KV_RANK + SUM_ROWS, tm), lambda i: (i // nt, i % nt, 0, 0)),
            pl.BlockSpec((1, tm // Q_BLOCK, IDX_HEADS, Q_BLOCK), lambda i: (i // nt, i % nt, 0, 0)),
        ],
        out_shape=[
            jax.ShapeDtypeStruct((t, DSA_QX), BF16),
            jax.ShapeDtypeStruct((t, IDX_DIM), BF16),
            jax.ShapeDtypeStruct((bsz, nt, DSA_KV_RANK + SUM_ROWS, tm), BF16),
            jax.ShapeDtypeStruct((bsz, seq // Q_BLOCK, IDX_HEADS, Q_BLOCK), F32),
        ],
        compiler_params=_params("arbitrary"),
        name="dsa_kprep",
    )(proj, proj, proj, proj, tabs, row(kv_norm), row(ln_g), row(ln_b))


def _dsa_mixer(h2, bsz, seq, positions, attn_norm, w_in, q_norm, kv_norm, idx_ln_g, idx_ln_b,
               w_qup, w_uk, w_uv, w_out, *, kc=512):
    t = h2.shape[0]
    kc = min(kc, seq)
    sizes = (DSA_Q_RANK, DSA_KV_RANK, DSA_ROPE, IDX_DIM, IDX_HEADS)
    cuts = np.cumsum((0,) + sizes)
    w_in_p = jnp.concatenate(
        [jnp.pad(w_in[:, int(a):int(b)], ((0, 0), (0, wp - (int(b) - int(a)))))
         for a, b, wp in zip(cuts[:-1], cuts[1:], DSA_PROJ_W)], axis=1).astype(BF16)
    proj = _mm(h2, w_in_p, gain=attn_norm, tm=1024, tn=w_in_p.shape[1])

    nq = DSA_HEADS * DSA_QK_DIM
    w_q = w_qup[:, :nq].reshape(DSA_Q_RANK, DSA_HEADS, DSA_QK_DIM)
    w_q = jnp.pad(w_q, ((0, 0), (0, 0), (0, DSA_QPAD - DSA_QK_DIM))).reshape(DSA_Q_RANK, DSA_HEADS * DSA_QPAD)
    w_qp = jnp.concatenate([w_q, w_qup[:, nq:]], axis=1).astype(BF16)
    qall = _mm(proj, w_qp, gain=q_norm, out_dtype=BF16, tn=w_qp.shape[1])

    pos = positions.reshape(t)
    tabs = jnp.concatenate([_rope_lane_tables(pos, DSA_ROPE), _rope_lane_tables(pos, IDX_ROPE)], -1)
    kext, kidx, ct, wi = _dsa_kprep(proj, tabs, kv_norm, idx_ln_g, idx_ln_b, bsz, seq, kc)

    wk = jnp.zeros((DSA_HEADS, DSA_QPAD, DSA_QX), F32)
    wk = wk.at[:, DSA_ROPE:DSA_QK_DIM, :DSA_KV_RANK].set(jnp.swapaxes(w_uk, 1, 2))
    eye = jnp.eye(DSA_ROPE, dtype=F32)
    wk = wk.at[:, :DSA_ROPE, DSA_KV_RANK:DSA_KV_RANK + DSA_ROPE].set(jnp.broadcast_to(eye, (DSA_HEADS,) + eye.shape))

    o = _dsa_attention(
        qall.reshape(bsz, seq, -1), wi, tabs.reshape(bsz, seq, -1),
        kidx.reshape(bsz, seq, IDX_DIM), kext.reshape(bsz, seq, DSA_QX), ct,
        wk.astype(BF16), w_uv.astype(BF16), kc=kc)
    return _mm(o.reshape(t, -1), w_out.astype(BF16), res=h2, tm=1024)


FOX_KX = MXU_DIM
FOX_HEAD_PAIR = 2


def _fox_body(q_ref, k_ref, c3_ref, e_ref, vt_ref, o_ref, qx_ref, kx_ref, m_ref, acc_ref, *, tq, kc):
    i = pl.program_id(2)
    q0 = i * tq
    n_full = q0 // kc
    n_diag = tq // kc

    @pl.when(i == 0)
    def _():
        for a in range(FOX_HEAD_PAIR):
            kx_ref[a, :, :FOX_DIM] = k_ref[0, :, a * FOX_DIM:(a + 1) * FOX_DIM]
            kx_ref[a, :, FOX_DIM:] = _tn(c3_ref[0, a], e_ref[...]).astype(BF16)

    lane = lax.broadcasted_iota(jnp.int32, (tq, FOX_KX - FOX_DIM), 1)
    for a in range(FOX_HEAD_PAIR):
        qx_ref[a, :, :FOX_DIM] = q_ref[0, :, a * FOX_DIM:(a + 1) * FOX_DIM]
        qx_ref[a, :, FOX_DIM:] = jnp.where(lane < 3, 1.0, 0.0).astype(BF16)
    m_ref[...] = jnp.full_like(m_ref, MASK_NEG)
    acc_ref[...] = jnp.zeros_like(acc_ref)
    key_i = lax.broadcasted_iota(jnp.int32, (kc, tq), 0)
    t_l = q0 + lax.broadcasted_iota(jnp.int32, (kc, tq), 1)

    def step(c, masked):
        k0 = pl.multiple_of(c * kc, kc)

        def logits(a):
            s = _nt(kx_ref[a, pl.ds(k0, kc), :], qx_ref[a])
            return jnp.where((k0 + key_i) <= t_l, s, MASK_NEG) if masked else s

        s = logits(0)
        for a in range(FOX_HEAD_PAIR):
            s_next = logits(a + 1) if a + 1 < FOX_HEAD_PAIR else None
            _online_softmax_step(s, vt_ref[0, a, c], m_ref, acc_ref, slice(a * tq, (a + 1) * tq))
            s = s_next

    def full_chunk(c, _):
        step(c, False)
        return 0

    def diag_chunk(c, _):
        step(c, True)
        return 0

    lax.fori_loop(0, n_full, full_chunk, 0)
    lax.fori_loop(n_full, n_full + n_diag, diag_chunk, 0)
    for a in range(FOX_HEAD_PAIR):
        cols = slice(a * tq, (a + 1) * tq)
        o_a = acc_ref[:FOX_DIM, cols] / acc_ref[FOX_DIM:FOX_DIM + 1, cols]
        o_ref[0, :, a * FOX_DIM:(a + 1) * FOX_DIM] = o_a.T.astype(o_ref.dtype)


def _fox_attention(qkv, c3, vt, *, tq, kc):
    b, s, _ = qkv.shape
    hp = FOX_HEAD_PAIR
    npair = FOX_HEADS // hp
    assert s % tq == 0 and tq % kc == 0 and FOX_HEADS % hp == 0
    e = jnp.eye(SUBLANES, FOX_KX - FOX_DIM, dtype=BF16)
    return pl.pallas_call(
        functools.partial(_fox_body, tq=tq, kc=kc),
        grid=(b, npair, s // tq),
        in_specs=[
            pl.BlockSpec((1, tq, hp * FOX_DIM), lambda bb, h, i: (bb, i, h)),
            pl.BlockSpec((1, s, hp * FOX_DIM), lambda bb, h, i: (bb, 0, npair + h)),
            pl.BlockSpec((1, hp, SUBLANES, s), lambda bb, h, i: (bb, h, 0, 0)),
            pl.BlockSpec((SUBLANES, FOX_KX - FOX_DIM), lambda bb, h, i: (0, 0)),
            pl.BlockSpec((1, hp, s // kc, FOX_DIM + SUM_ROWS, kc), lambda bb, h, i: (bb, h, 0, 0, 0)),
        ],
        out_specs=pl.BlockSpec((1, tq, hp * FOX_DIM), lambda bb, h, i: (bb, i, h)),
        out_shape=jax.ShapeDtypeStruct((b, s, FOX_HEADS * FOX_DIM), BF16),
        scratch_shapes=[pltpu.VMEM((hp, tq, FOX_KX), BF16), pltpu.VMEM((hp, s, FOX_KX), BF16),
                        pltpu.VMEM((1, hp * tq), F32), pltpu.VMEM((FOX_DIM + SUM_ROWS, hp * tq), F32)],
        compiler_params=_params("arbitrary", "arbitrary", "arbitrary"),
        name="fox_attn",
    )(qkv, qkv, c3, e, vt)


def _fox_mixer(h2, bsz, seq, attn_norm, w_in, b_f, w_out, *, tq=512, kc=512):
    t = h2.shape[0]
    hd = FOX_HEADS * FOX_DIM
    tq = min(tq, seq)
    kc = min(kc, tq)
    scale = FOX_DIM ** -0.5 * LOG2E
    w_qkv = jnp.concatenate([w_in[:, :hd] * scale, w_in[:, hd:3 * hd]], axis=1).astype(BF16)
    qkv = _mm(h2, w_qkv, gain=attn_norm, out_dtype=BF16, tm=1024, tn=1024)
    w_f = jnp.pad(w_in[:, 3 * hd:], ((0, 0), (0, LANES - FOX_HEADS))).astype(BF16)
    f_logit = _mm(h2, w_f, gain=attn_norm)[:, :FOX_HEADS]
    log_f = jax.nn.log_sigmoid(f_logit + b_f)
    cum = jnp.cumsum(log_f.reshape(bsz, seq, FOX_HEADS), axis=1)
    c3 = jnp.stack(_split3_bits(-cum * LOG2E), axis=0).transpose(1, 3, 0, 2)
    c3 = jnp.pad(c3, ((0, 0), (0, 0), (0, SUBLANES - 3), (0, 0)))
    qkv4 = qkv.reshape(bsz, seq, 3, FOX_HEADS, FOX_DIM)
    vt = _with_sum_rows(_chunks_t(qkv4[:, :, 2].transpose(0, 2, 1, 3), kc))
    o = _fox_attention(qkv.reshape(bsz, seq, 3 * hd), c3, vt, tq=tq, kc=kc)
    return _mm(o.reshape(t, hd), w_out.astype(BF16), res=h2, tm=1024)


NSA_HEAD_GROUP = 4


def _cmp_body(x_ref, pe_ref, w1_ref, b1_ref, w2_ref, o_ref, *, half):
    x = x_ref[0, 0]
    w_lo = w1_ref[:half, :]
    w_hi = w1_ref[half:, :]
    a = jnp.dot(x, w_lo, preferred_element_type=F32)
    bnext = jnp.dot(x, w_hi, preferred_element_type=F32)
    n = x.shape[0]
    bnext = pltpu.roll(bnext, n - 1, 0)
    pe = pe_ref[...]
    pe_b = (jnp.dot(pe[:, :half], w_lo, preferred_element_type=F32)
            + jnp.dot(pe[:, half:], w_hi, preferred_element_type=F32))[0:1, :]
    hid = jax.nn.gelu(a + bnext + pe_b + b1_ref[...])
    o_ref[0, 0] = jnp.dot(hid.astype(BF16), w2_ref[...], preferred_element_type=F32).astype(o_ref.dtype)


def _compress(x, pe, w1, b1, w2):
    b, g, n, kd = x.shape
    dout = w2.shape[1]
    pe8 = jnp.broadcast_to(pe.reshape(1, 2 * kd), (8, 2 * kd)).astype(BF16)
    return pl.pallas_call(
        functools.partial(_cmp_body, half=kd),
        grid=(b, g),
        in_specs=[
            pl.BlockSpec((1, 1, n, kd), lambda bb, gg: (bb, gg, 0, 0)),
            pl.BlockSpec((8, 2 * kd), lambda bb, gg: (0, 0)),
            pl.BlockSpec((2 * kd, CMP_HIDDEN), lambda bb, gg: (0, 0)),
            pl.BlockSpec((1, CMP_HIDDEN), lambda bb, gg: (0, 0)),
            pl.BlockSpec((CMP_HIDDEN, dout), lambda bb, gg: (0, 0)),
        ],
        out_specs=pl.BlockSpec((1, 1, n, dout), lambda bb, gg: (bb, gg, 0, 0)),
        out_shape=jax.ShapeDtypeStruct((b, g, n, dout), BF16),
        compiler_params=_params("arbitrary", "arbitrary"),
        name="nsa_compress",
    )(x, pe8, w1.astype(BF16), b1.reshape(1, CMP_HIDDEN), w2.astype(BF16))


def _nsa_body(q_ref, tab_ref, gate_ref, kc_ref, vct_ref, ks_ref, vst_ref, kw_ref, vwt_ref, ov_ref, exp_ref,
              o_ref, qs_ref, m_ref, acc_ref, *, kc, n_cmp, n_slc, n_sel):
    i = pl.program_id(2)
    q0 = i * Q_BLOCK
    nc = (q0 + Q_BLOCK + kc - 1) // kc
    jh = NSA_HPG
    scale = NSA_QK_DIM ** -0.5 * LOG2E
    tab = tab_ref[0]
    cf, sa, sb = tab[:, 0:LANES], tab[:, LANES:2 * LANES], tab[:, 2 * LANES:3 * LANES]

    for j in range(jh):
        xj = q_ref[0, :, j * NSA_QK_DIM:(j + 1) * NSA_QK_DIM].astype(F32)
        r0 = _rope_first_vreg(xj[:, :LANES], cf, sa, sb, NSA_ROPE // 2)
        qs_ref[j * Q_BLOCK:(j + 1) * Q_BLOCK, :] = (jnp.concatenate([r0, xj[:, LANES:]], axis=1) * scale).astype(BF16)
    qs = qs_ref[...]

    n_id = lax.broadcasted_iota(jnp.int32, (n_cmp, Q_BLOCK), 0)
    t_c = q0 + lax.broadcasted_iota(jnp.int32, (n_cmp, Q_BLOCK), 1)
    cbias = jnp.where((n_id * CMP_STRIDE + (CMP_BLOCK - 1)) <= t_c, 0.0, MASK_NEG)
    pc = _softmax_cols(_nt(kc_ref[0, 0], qs) + jnp.tile(cbias, (1, jh)))
    o_c = jnp.dot(vct_ref[0, 0], pc.astype(BF16), preferred_element_type=F32)

    pcs = pc[:, 0:Q_BLOCK]
    for j in range(1, jh):
        pcs = pcs + pc[:, j * Q_BLOCK:(j + 1) * Q_BLOCK]
    ov = ov_ref[...]
    imp = sum(jnp.dot(ov, term, preferred_element_type=F32) for term in _split3(pcs))

    blk = lax.broadcasted_iota(jnp.int32, (n_slc, Q_BLOCK), 0)
    t_b = q0 + lax.broadcasted_iota(jnp.int32, (n_slc, Q_BLOCK), 1)
    cur = lax.shift_right_logical(t_b, int(np.log2(SLC_BLOCK)))
    causal_blk = blk * SLC_BLOCK <= t_b
    forced = (blk == 0) | (blk == cur) | (blk == cur - 1)
    val = jnp.where(causal_blk, jnp.where(forced, jnp.inf, imp), -jnp.inf)
    rank = jnp.zeros((n_slc, Q_BLOCK), F32)
    for mp in range(n_slc):
        vrow = val[mp:mp + 1, :]
        before = (vrow > val) | ((vrow == val) & (blk > mp))
        rank = rank + jnp.where(before, 1.0, 0.0)
    sel = jnp.where((rank < n_sel) & causal_blk, 1.0, 0.0)
    sel_p = jnp.concatenate([sel, jnp.zeros((LANES - n_slc, Q_BLOCK), F32)], axis=0).astype(BF16)

    m_ref[...] = jnp.full_like(m_ref, MASK_NEG)
    acc_ref[...] = jnp.zeros_like(acc_ref)
    key_i = lax.broadcasted_iota(jnp.int32, (kc, Q_BLOCK), 0)
    t_k = q0 + lax.broadcasted_iota(jnp.int32, (kc, Q_BLOCK), 1)
    gl = NSA_HEAD_GROUP * Q_BLOCK

    def sel_chunk(c, _):
        k0 = pl.multiple_of(c * kc, kc)
        hit = jnp.dot(exp_ref[pl.ds(k0, kc), :], sel_p, preferred_element_type=F32)
        bias = jnp.where((hit > 0.5) & ((k0 + key_i) <= t_k), 0.0, MASK_NEG)
        bias = jnp.tile(bias, (1, NSA_HEAD_GROUP))
        ksc = ks_ref[0, 0, pl.ds(k0, kc), :]
        vt = vst_ref[0, 0, c]
        n_slab = jh // NSA_HEAD_GROUP
        s = _nt(ksc, qs_ref[0:gl, :]) + bias
        for hg in range(n_slab):
            s_next = _nt(ksc, qs_ref[(hg + 1) * gl:(hg + 2) * gl, :]) + bias if hg + 1 < n_slab else None
            _online_softmax_step(s, vt, m_ref, acc_ref, slice(hg * gl, (hg + 1) * gl))
            s = s_next
        return 0

    lax.fori_loop(0, nc, sel_chunk, 0)
    o_s = acc_ref[:NSA_V_DIM, :] / acc_ref[NSA_V_DIM:NSA_V_DIM + 1, :]

    wl = WINDOW + Q_BLOCK
    kstart = pl.multiple_of(q0, Q_BLOCK)
    s_pos = q0 - WINDOW + lax.broadcasted_iota(jnp.int32, (wl, Q_BLOCK), 0)
    t_w = q0 + lax.broadcasted_iota(jnp.int32, (wl, Q_BLOCK), 1)
    wbias = jnp.where((s_pos >= 0) & (s_pos <= t_w) & (s_pos > t_w - WINDOW), 0.0, MASK_NEG)
    pw = _softmax_cols(_nt(kw_ref[0, 0, pl.ds(kstart, wl), :], qs) + jnp.tile(wbias, (1, jh))).astype(BF16)
    o_w = jnp.dot(vwt_ref[0, 0, i], pw[0:Q_BLOCK, :], preferred_element_type=F32)
    for c in range(1, wl // Q_BLOCK):
        o_w = o_w + jnp.dot(vwt_ref[0, 0, i + c], pw[c * Q_BLOCK:(c + 1) * Q_BLOCK, :],
                            preferred_element_type=F32)

    g = gate_ref[0, 0]
    for j in range(jh):
        cj = slice(j * Q_BLOCK, (j + 1) * Q_BLOCK)
        out = (g[j:j + 1, :] * o_c[:, cj] + g[jh + j:jh + j + 1, :] * o_s[:, cj]
               + g[2 * jh + j:2 * jh + j + 1, :] * o_w[:, cj])
        o_ref[0, :, j * NSA_V_DIM:(j + 1) * NSA_V_DIM] = out.T.astype(o_ref.dtype)


def _nsa_attention(q, tabs, gates, kcmp, vcmp_t, ks, vs_t, kw, vw_t, *, kc):
    b, s, _ = q.shape
    g = NSA_GROUPS
    assert s % kc == 0 and kc % SLC_BLOCK == 0
    n_cmp = kcmp.shape[2]
    n_slc = s // SLC_BLOCK
    assert n_slc <= LANES
    n_sel = min(SLC_TOPK, n_slc)
    cmp_start = np.arange(n_cmp) * CMP_STRIDE
    slc_start = np.arange(n_slc) * SLC_BLOCK
    ov = ((cmp_start[None, :] < slc_start[:, None] + SLC_BLOCK)
          & (cmp_start[None, :] + CMP_BLOCK > slc_start[:, None])).astype(np.float32)
    expand = (np.arange(s)[:, None] // SLC_BLOCK == np.arange(LANES)[None, :]).astype(np.float32)
    qw = NSA_HPG * NSA_QK_DIM
    ow = NSA_HPG * NSA_V_DIM
    nrow = NSA_HPG * Q_BLOCK
    wl = s + WINDOW
    return pl.pallas_call(
        functools.partial(_nsa_body, kc=kc, n_cmp=n_cmp, n_slc=n_slc, n_sel=float(n_sel)),
        grid=(b, g, s // Q_BLOCK),
        in_specs=[
            pl.BlockSpec((1, Q_BLOCK, qw), lambda bb, gg, i: (bb, i, gg)),
            pl.BlockSpec((1, Q_BLOCK, 3 * LANES), lambda bb, gg, i: (bb, i, 0)),
            pl.BlockSpec((1, 1, 3 * NSA_HPG, Q_BLOCK), lambda bb, gg, i: (bb, gg, 0, i)),
            pl.BlockSpec((1, 1, n_cmp, NSA_QK_DIM), lambda bb, gg, i: (bb, gg, 0, 0)),
            pl.BlockSpec((1, 1, NSA_V_DIM, n_cmp), lambda bb, gg, i: (bb, gg, 0, 0)),
            pl.BlockSpec((1, 1, s, NSA_QK_DIM), lambda bb, gg, i: (bb, gg, 0, 0)),
            pl.BlockSpec((1, 1, s // kc, NSA_V_DIM + SUM_ROWS, kc), lambda bb, gg, i: (bb, gg, 0, 0, 0)),
            pl.BlockSpec((1, 1, wl, NSA_QK_DIM), lambda bb, gg, i: (bb, gg, 0, 0)),
            pl.BlockSpec((1, 1, wl // Q_BLOCK, NSA_V_DIM, Q_BLOCK), lambda bb, gg, i: (bb, gg, 0, 0, 0)),
            pl.BlockSpec((n_slc, n_cmp), lambda bb, gg, i: (0, 0)),
            pl.BlockSpec((s, LANES), lambda bb, gg, i: (0, 0)),
        ],
        out_specs=pl.BlockSpec((1, Q_BLOCK, ow), lambda bb, gg, i: (bb, i, gg)),
        out_shape=jax.ShapeDtypeStruct((b, s, NSA_HEADS * NSA_V_DIM), BF16),
        scratch_shapes=[
            pltpu.VMEM((nrow, NSA_QK_DIM), BF16),
            pltpu.VMEM((1, nrow), F32),
            pltpu.VMEM((NSA_V_DIM + SUM_ROWS, nrow), F32),
        ],
        compiler_params=_params("arbitrary", "arbitrary", "arbitrary"),
        name="nsa_attn",
    )(q, tabs, gates, kcmp, vcmp_t, ks, vs_t, kw, vw_t, jnp.asarray(ov, BF16), jnp.asarray(expand, BF16))


def _nsa_mixer(h2, bsz, seq, positions, attn_norm, w_in, k_pe, k_w1, k_b1, k_w2, v_pe, v_w1, v_b1, v_w2, w_out,
               *, kc=1024):
    t = h2.shape[0]
    kc = min(kc, seq)
    g, jh, dk, dv = NSA_GROUPS, NSA_HPG, NSA_QK_DIM, NSA_V_DIM
    nq = NSA_HEADS * dk
    n_rest = w_in.shape[1] - nq
    n_rest_p = -(-n_rest // 512) * 512
    q = _mm(h2, w_in[:, :nq].astype(BF16), gain=attn_norm, out_dtype=BF16, tm=1024, tn=1024)
    w_rest = jnp.pad(w_in[:, nq:], ((0, 0), (0, n_rest_p - n_rest))).astype(BF16)
    rest = _mm(h2, w_rest, gain=attn_norm, tm=1024, tn=1024)
    offs = np.cumsum([0, NSA_KD, NSA_VD, NSA_KD, NSA_VD, NSA_KD, NSA_VD, 3 * NSA_HEADS])
    k_c, v_c, k_s, v_s, k_w, v_w, g_logit = [rest[:, int(a):int(b)] for a, b in zip(offs[:-1], offs[1:])]

    pos = positions.reshape(t)
    cos, sin = _rope_cos_sin(pos, NSA_ROPE)

    def rope_k(k):
        return _rope_glue(k.reshape(t, g, dk), cos[:, None, :], sin[:, None, :])

    def per_group(x, d):
        return x.reshape(bsz, seq, g, d).transpose(0, 2, 1, 3).astype(BF16)

    def chunks(x, d):
        x = x.reshape(bsz, seq // CMP_STRIDE, CMP_STRIDE, g, d).transpose(0, 3, 1, 2, 4)
        return x.reshape(bsz, g, seq // CMP_STRIDE, CMP_STRIDE * d).astype(BF16)

    kcmp = _compress(chunks(rope_k(k_c), dk), k_pe, k_w1, k_b1, k_w2)
    vcmp_t = jnp.swapaxes(_compress(chunks(v_c.reshape(t, g, dv), dv), v_pe, v_w1, v_b1, v_w2), -1, -2)
    ks = per_group(rope_k(k_s), dk)
    vs_t = _with_sum_rows(_chunks_t(per_group(v_s.reshape(t, g, dv), dv), kc))
    pad = ((0, 0), (0, 0), (WINDOW, 0), (0, 0))
    kw = jnp.pad(per_group(rope_k(k_w), dk), pad)
    vw_t = _chunks_t(jnp.pad(per_group(v_w.reshape(t, g, dv), dv), pad), Q_BLOCK)
    gates = jax.nn.sigmoid(g_logit).reshape(bsz, seq, 3, g, jh).transpose(0, 3, 2, 4, 1).reshape(bsz, g, 3 * jh, seq)
    tabs = _rope_lane_tables(pos, NSA_ROPE).reshape(bsz, seq, 3 * LANES)

    o = _nsa_attention(q.reshape(bsz, seq, nq), tabs, gates, kcmp, vcmp_t, ks, vs_t, kw, vw_t, kc=kc)
    return _mm(o.reshape(t, NSA_HEADS * dv), w_out.astype(BF16), res=h2, tm=1024)


def _norm_body(x_ref, g_ref, o_ref):
    o_ref[...] = _rms(x_ref[...], g_ref[...])


def _final_norm(h2, gain, *, tm=512):
    t, d = h2.shape
    tm = min(tm, t)
    return pl.pallas_call(
        _norm_body,
        grid=(t // tm,),
        in_specs=[pl.BlockSpec((tm, d), lambda i: (i, 0)), pl.BlockSpec((1, d), lambda i: (0, 0))],
        out_specs=pl.BlockSpec((tm, d), lambda i: (i, 0)),
        out_shape=jax.ShapeDtypeStruct((t, d), F32),
        compiler_params=_params("arbitrary"),
        name="final_norm",
    )(h2, gain.reshape(1, d))


def _ffn_layer(h2, seq, ffn_norm, w_up, conv_w, conv_b, w_down):
    return _ffn(h2, seq, ffn_norm, w_up.astype(BF16), conv_w, conv_b, w_down.astype(BF16))


def kernel(x, positions, l0_attn_norm, l0_dsa_w_in, l0_dsa_q_norm, l0_dsa_kv_norm, l0_dsa_idx_ln_g, l0_dsa_idx_ln_b, l0_dsa_w_qup, l0_dsa_w_uk, l0_dsa_w_uv, l0_dsa_w_out, l0_ffn_norm, l0_ffn_up, l0_ffn_conv_w, l0_ffn_conv_b, l0_ffn_down, l1_attn_norm, l1_fox_w_in, l1_fox_b_f, l1_fox_w_out, l1_ffn_norm, l1_ffn_up, l1_ffn_conv_w, l1_ffn_conv_b, l1_ffn_down, l2_attn_norm, l2_nsa_w_in, l2_nsa_k_pe, l2_nsa_k_w1, l2_nsa_k_b1, l2_nsa_k_w2, l2_nsa_v_pe, l2_nsa_v_w1, l2_nsa_v_b1, l2_nsa_v_w2, l2_nsa_w_out, l2_ffn_norm, l2_ffn_up, l2_ffn_conv_w, l2_ffn_conv_b, l2_ffn_down, l3_attn_norm, l3_dsa_w_in, l3_dsa_q_norm, l3_dsa_kv_norm, l3_dsa_idx_ln_g, l3_dsa_idx_ln_b, l3_dsa_w_qup, l3_dsa_w_uk, l3_dsa_w_uv, l3_dsa_w_out, l3_ffn_norm, l3_ffn_up, l3_ffn_conv_w, l3_ffn_conv_b, l3_ffn_down, final_norm):
    bsz, seq, d = x.shape
    h = x.reshape(bsz * seq, d)
    h = _dsa_mixer(h, bsz, seq, positions, l0_attn_norm, l0_dsa_w_in, l0_dsa_q_norm, l0_dsa_kv_norm,
                   l0_dsa_idx_ln_g, l0_dsa_idx_ln_b, l0_dsa_w_qup, l0_dsa_w_uk, l0_dsa_w_uv, l0_dsa_w_out)
    h = _ffn_layer(h, seq, l0_ffn_norm, l0_ffn_up, l0_ffn_conv_w, l0_ffn_conv_b, l0_ffn_down)
    h = _fox_mixer(h, bsz, seq, l1_attn_norm, l1_fox_w_in, l1_fox_b_f, l1_fox_w_out)
    h = _ffn_layer(h, seq, l1_ffn_norm, l1_ffn_up, l1_ffn_conv_w, l1_ffn_conv_b, l1_ffn_down)
    h = _nsa_mixer(h, bsz, seq, positions, l2_attn_norm, l2_nsa_w_in, l2_nsa_k_pe, l2_nsa_k_w1, l2_nsa_k_b1,
                   l2_nsa_k_w2, l2_nsa_v_pe, l2_nsa_v_w1, l2_nsa_v_b1, l2_nsa_v_w2, l2_nsa_w_out)
    h = _ffn_layer(h, seq, l2_ffn_norm, l2_ffn_up, l2_ffn_conv_w, l2_ffn_conv_b, l2_ffn_down)
    h = _dsa_mixer(h, bsz, seq, positions, l3_attn_norm, l3_dsa_w_in, l3_dsa_q_norm, l3_dsa_kv_norm,
                   l3_dsa_idx_ln_g, l3_dsa_idx_ln_b, l3_dsa_w_qup, l3_dsa_w_uk, l3_dsa_w_uv, l3_dsa_w_out)
    h = _ffn_layer(h, seq, l3_ffn_norm, l3_ffn_up, l3_ffn_conv_w, l3_ffn_conv_b, l3_ffn_down)
    return _final_norm(h, final_norm).reshape(bsz, seq, d)
```

```python
import functools

import jax
import jax.numpy as jnp
import numpy as np
from jax import lax
from jax.experimental import pallas as pl
from jax.experimental.pallas import tpu as pltpu

F32 = jnp.float32
BF16 = jnp.bfloat16

ROPE_THETA = 500000.0
NORM_EPS = 1e-6
Q_BLOCK = 128

DSA_HEADS = 32
DSA_Q_RANK = 512
DSA_KV_RANK = 256
DSA_QK_DIM = 192
DSA_ROPE = 48
DSA_NOPE = DSA_QK_DIM - DSA_ROPE
DSA_V_DIM = 128
IDX_HEADS = 16
IDX_DIM = 128
IDX_ROPE = 32
DSA_TOPK_MAX = 256
DSA_IN = DSA_Q_RANK + DSA_KV_RANK + DSA_ROPE + IDX_DIM + IDX_HEADS

FOX_HEADS = 16
FOX_DIM = 128

NSA_HEADS = 48
NSA_GROUPS = 4
NSA_HPG = NSA_HEADS // NSA_GROUPS
NSA_QK_DIM = 192
NSA_ROPE = 48
NSA_V_DIM = 128
CMP_BLOCK = 32
CMP_STRIDE = 16
CMP_HIDDEN = 256
SLC_BLOCK = 64
SLC_TOPK = 16
WINDOW = 512
NSA_KD = NSA_GROUPS * NSA_QK_DIM
NSA_VD = NSA_GROUPS * NSA_V_DIM

CONV_WIDTH = 3

LANES = 128
SUBLANES = 8
BF16_SUBLANES = 16
MXU_DIM = 256
VMEM_LIMIT = 56 * 1024 * 1024

MASK_NEG = -1e30
SUM_ROWS = BF16_SUBLANES
LOG2E = 1.4426950408889634
BISECT_ITERS = 32
BISECT_UNROLL = 4


def _params(*sem):
    return pltpu.CompilerParams(dimension_semantics=sem, vmem_limit_bytes=VMEM_LIMIT)


def _nt(a, b):
    return lax.dot_general(a, b, (((1,), (1,)), ((), ())), preferred_element_type=F32)


def _tn(a, b):
    return lax.dot_general(a, b, (((0,), (0,)), ((), ())), preferred_element_type=F32)


def _rms(x, g):
    return x * lax.rsqrt(jnp.mean(x * x, axis=-1, keepdims=True) + NORM_EPS) * g


def _rope_first_vreg(x0, cf, sa, sb, half):
    return (x0 * cf + pltpu.roll(x0, LANES - half, 1) * sa + pltpu.roll(x0, half, 1) * sb)


def _split3(x):
    hi = x.astype(BF16)
    r1 = x - hi.astype(F32)
    mid = r1.astype(BF16)
    lo = (r1 - mid.astype(F32)).astype(BF16)
    return hi, mid, lo


def _split3_bits(x):
    def trunc(v):
        bits = lax.bitcast_convert_type(v, jnp.uint32) & jnp.uint32(0xFFFF0000)
        return lax.bitcast_convert_type(bits, F32)

    hi = trunc(x)
    r1 = x - hi
    mid = trunc(r1)
    lo = r1 - mid
    return hi.astype(BF16), mid.astype(BF16), lo.astype(BF16)


def _online_softmax_step(s, pv_lhs, m_ref, acc_ref, cols):
    m_prev = m_ref[:, cols]
    m_new = jnp.maximum(m_prev, s.max(axis=0, keepdims=True))
    alpha = jnp.exp2(m_prev - m_new)
    p = jnp.exp2(s - m_new).astype(BF16)
    acc_ref[:, cols] = alpha * acc_ref[:, cols] + jnp.dot(pv_lhs, p, preferred_element_type=F32)
    m_ref[:, cols] = m_new


def _softmax_cols(z):
    mx = z.max(axis=0, keepdims=True)
    e = jnp.exp2(z - mx)
    den = e.sum(axis=0, keepdims=True)
    return e * jnp.where(mx > 0.5 * MASK_NEG, 1.0 / den, 0.0)


def _mm_body(*refs, has_gain, has_res):
    x_ref, w_ref = refs[0], refs[1]
    k = 2
    g_ref = r_ref = None
    if has_gain:
        g_ref = refs[k]
        k += 1
    if has_res:
        r_ref = refs[k]
        k += 1
    o_ref = refs[k]
    if has_gain:
        xn_ref = refs[k + 1]

        @pl.when(pl.program_id(1) == 0)
        def _():
            xn_ref[...] = _rms(x_ref[...].astype(F32), g_ref[...]).astype(BF16)

        a = xn_ref[...]
    else:
        a = x_ref[...].astype(BF16)
    acc = jnp.dot(a, w_ref[...], preferred_element_type=F32)
    if has_res:
        acc = acc + r_ref[...]
    o_ref[...] = acc.astype(o_ref.dtype)


def _mm(x, w, *, gain=None, res=None, out_dtype=F32, x_col_block=0, tm=512, tn=512):
    m = x.shape[0]
    k, n = w.shape
    tm = min(tm, m)
    tn = min(tn, n)
    assert m % tm == 0 and n % tn == 0 and x.shape[1] >= (x_col_block + 1) * k
    in_specs = [pl.BlockSpec((tm, k), lambda i, j: (i, x_col_block)),
                pl.BlockSpec((k, tn), lambda i, j: (0, j))]
    args = [x, w]
    scratch = []
    if gain is not None:
        in_specs.append(pl.BlockSpec((1, k), lambda i, j: (0, 0)))
        args.append(gain.reshape(1, k).astype(F32))
        scratch.append(pltpu.VMEM((tm, k), BF16))
    if res is not None:
        in_specs.append(pl.BlockSpec((tm, tn), lambda i, j: (i, j)))
        args.append(res)
    return pl.pallas_call(
        functools.partial(_mm_body, has_gain=gain is not None, has_res=res is not None),
        grid=(m // tm, n // tn),
        in_specs=in_specs,
        out_specs=pl.BlockSpec((tm, tn), lambda i, j: (i, j)),
        out_shape=jax.ShapeDtypeStruct((m, n), out_dtype),
        scratch_shapes=scratch,
        compiler_params=_params("arbitrary", "arbitrary"),
        name="mm",
    )(*args)


FFN_HALO = BF16_SUBLANES


def _ffn_body(h_ref, hp_ref, g_ref, wg_ref, wv_ref, cwg_ref, cwv_ref, cbg_ref, cbv_ref, wd_ref,
              o_ref, xn_ref, ug_ref, uv_ref, acc_ref, *, tm, seq):
    i = pl.program_id(0)
    j = pl.program_id(1)

    @pl.when(j == 0)
    def _():
        xn_ref[FFN_HALO:, :] = _rms(h_ref[...], g_ref[...]).astype(BF16)
        prev = _rms(hp_ref[...], g_ref[...])
        seq_start = (i * tm) % seq == 0
        xn_ref[:FFN_HALO, :] = jnp.where(seq_start, 0.0, prev).astype(BF16)
        acc_ref[...] = jnp.zeros_like(acc_ref)

    xn = xn_ref[...]
    ug_ref[...] = jnp.dot(xn, wg_ref[...], preferred_element_type=F32)
    uv_ref[...] = jnp.dot(xn, wv_ref[...], preferred_element_type=F32)

    def conv(u_ref, cw_ref, cb_ref):
        y = cb_ref[...]
        for t in range(CONV_WIDTH):
            off = FFN_HALO - (CONV_WIDTH - 1) + t
            y = y + cw_ref[t:t + 1, :] * u_ref[off:off + tm, :]
        return y

    yg = conv(ug_ref, cwg_ref, cbg_ref)
    yv = conv(uv_ref, cwv_ref, cbv_ref)
    a = (jax.nn.silu(yg) * yv).astype(BF16)
    acc_ref[...] += jnp.dot(a, wd_ref[...], preferred_element_type=F32)

    @pl.when(j == pl.num_programs(1) - 1)
    def _():
        o_ref[...] = h_ref[...] + acc_ref[...]


def _ffn(h, seq, gain, w_up, conv_w, conv_b, w_down, *, tm=512, tf=512):
    t, d = h.shape
    dff = w_down.shape[0]
    tm = min(tm, seq)
    assert seq % tm == 0 and dff % tf == 0 and tm % FFN_HALO == 0
    nf = dff // tf
    hb = tm // FFN_HALO
    conv_b2 = conv_b.reshape(1, 2 * dff)
    return pl.pallas_call(
        functools.partial(_ffn_body, tm=tm, seq=seq),
        grid=(t // tm, nf),
        in_specs=[
            pl.BlockSpec((tm, d), lambda i, j: (i, 0)),
            pl.BlockSpec((FFN_HALO, d), lambda i, j: (jnp.maximum(i * hb - 1, 0), 0)),
            pl.BlockSpec((1, d), lambda i, j: (0, 0)),
            pl.BlockSpec((d, tf), lambda i, j: (0, j)),
            pl.BlockSpec((d, tf), lambda i, j: (0, nf + j)),
            pl.BlockSpec((CONV_WIDTH, tf), lambda i, j: (0, j)),
            pl.BlockSpec((CONV_WIDTH, tf), lambda i, j: (0, nf + j)),
            pl.BlockSpec((1, tf), lambda i, j: (0, j)),
            pl.BlockSpec((1, tf), lambda i, j: (0, nf + j)),
            pl.BlockSpec((tf, d), lambda i, j: (j, 0)),
        ],
        out_specs=pl.BlockSpec((tm, d), lambda i, j: (i, 0)),
        out_shape=jax.ShapeDtypeStruct((t, d), F32),
        scratch_shapes=[
            pltpu.VMEM((tm + FFN_HALO, d), BF16),
            pltpu.VMEM((tm + FFN_HALO, tf), F32),
            pltpu.VMEM((tm + FFN_HALO, tf), F32),
            pltpu.VMEM((tm, d), F32),
        ],
        compiler_params=_params("arbitrary", "arbitrary"),
        name="ffn",
    )(h, h, gain.reshape(1, d), w_up, w_up, conv_w, conv_w, conv_b2, conv_b2, w_down)


def _rope_cos_sin(positions, rot):
    inv = ROPE_THETA ** (-jnp.arange(0, rot, 2, dtype=F32) / rot)
    ang = positions.astype(F32)[..., None] * inv
    return jnp.cos(ang), jnp.sin(ang)


def _rope_lane_tables(positions, rot):
    c, s = _rope_cos_sin(positions, rot)
    half = rot // 2
    shp = c.shape[:-1]
    cf = jnp.concatenate([c, c, jnp.ones(shp + (LANES - rot,), F32)], -1)
    sa = jnp.concatenate([-s, jnp.zeros(shp + (LANES - half,), F32)], -1)
    sb = jnp.concatenate([jnp.zeros(shp + (half,), F32), s, jnp.zeros(shp + (LANES - rot,), F32)], -1)
    return jnp.concatenate([cf, sa, sb], -1)


def _rope_glue(x, cos, sin):
    half = cos.shape[-1]
    x1, x2 = x[..., :half], x[..., half:2 * half]
    return jnp.concatenate([x1 * cos - x2 * sin, x1 * sin + x2 * cos, x[..., 2 * half:]], -1)


def _with_sum_rows(vt):
    return jnp.concatenate([vt, jnp.ones(vt.shape[:-2] + (SUM_ROWS, vt.shape[-1]), vt.dtype)], axis=-2)


def _chunks_t(x, kc):
    s, d = x.shape[-2:]
    lead = x.shape[:-2]
    return jnp.swapaxes(x.reshape(lead + (s // kc, kc, d)), -1, -2)


DSA_QX = 384
DSA_QPAD = MXU_DIM
DSA_HEAD_GROUP = 16
DSA_CNT_ROWS = 4 * SUBLANES


def _dsa_body(q_ref, qi_ref, wi_ref, tab_ref, kidx_ref, kext_ref, ct_ref, wk_ref, wuv_ref, tri_ref, o_ref,
              qx_ref, qis_ref, sc_ref, m_ref, acc_ref, *, kc, n_keep):
    i = pl.program_id(1)
    q0 = i * Q_BLOCK
    nc = (q0 + Q_BLOCK + kc - 1) // kc
    scale = DSA_QK_DIM ** -0.5 * LOG2E
    tab = tab_ref[0]
    cf, sa, sb = tab[:, 0:LANES], tab[:, LANES:2 * LANES], tab[:, 2 * LANES:3 * LANES]
    cfi, sai, sbi = tab[:, 3 * LANES:4 * LANES], tab[:, 4 * LANES:5 * LANES], tab[:, 5 * LANES:6 * LANES]

    for h in range(DSA_HEADS):
        xh = q_ref[0, :, h * DSA_QPAD:(h + 1) * DSA_QPAD].astype(F32)
        r0 = _rope_first_vreg(xh[:, :LANES], cf, sa, sb, DSA_ROPE // 2)
        qh = jnp.concatenate([r0, xh[:, LANES:]], axis=1).astype(BF16)
        qx = jnp.dot(qh, wk_ref[h], preferred_element_type=F32) * scale
        qx_ref[h * Q_BLOCK:(h + 1) * Q_BLOCK, :] = qx.astype(BF16)
    for h in range(IDX_HEADS):
        xi = qi_ref[0, :, h * IDX_DIM:(h + 1) * IDX_DIM].astype(F32)
        qis_ref[h * Q_BLOCK:(h + 1) * Q_BLOCK, :] = _rope_first_vreg(xi, cfi, sai, sbi, IDX_ROPE // 2).astype(BF16)

    wi = wi_ref[0, 0]
    key_i = lax.broadcasted_iota(jnp.int32, (kc, Q_BLOCK), 0)
    t_l = q0 + lax.broadcasted_iota(jnp.int32, (kc, Q_BLOCK), 1)

    def score_chunk(c, carry):
        lo, hi = carry
        k0 = pl.multiple_of(c * kc, kc)
        d = _nt(kidx_ref[0, pl.ds(k0, kc), :], qis_ref[...])
        d = jnp.maximum(d, 0.0)
        sc = d[:, 0:Q_BLOCK] * wi[0:1, :]
        for h in range(1, IDX_HEADS):
            sc = sc + d[:, h * Q_BLOCK:(h + 1) * Q_BLOCK] * wi[h:h + 1, :]
        sc = sc + 0.0
        causal = (k0 + key_i) <= t_l
        sc_ref[c] = jnp.where(causal, sc, -jnp.inf)
        lo = jnp.minimum(lo, jnp.where(causal, sc, jnp.inf).min(axis=0, keepdims=True))
        hi = jnp.maximum(hi, jnp.where(causal, sc, -jnp.inf).max(axis=0, keepdims=True))
        return lo, hi

    lo, hi = lax.fori_loop(0, nc, score_chunk,
                           (jnp.full((1, Q_BLOCK), jnp.inf, F32), jnp.full((1, Q_BLOCK), -jnp.inf, F32)))

    def bisect_more(carry):
        it, _, _, n_lo = carry
        return (it < BISECT_ITERS) & (jnp.max(n_lo) > n_keep)

    def bisect(carry):
        it, lo, hi, n_lo = carry

        def count_chunk_at(mid):
            def count_chunk(c, cnt):
                ge = jnp.where(sc_ref[c] >= mid, 1.0, 0.0)
                return cnt + ge.reshape(kc // DSA_CNT_ROWS, DSA_CNT_ROWS, Q_BLOCK).sum(axis=0)
            return count_chunk

        for _ in range(BISECT_UNROLL):
            mid = lo + (hi - lo) * 0.5
            cnt = lax.fori_loop(0, nc, count_chunk_at(mid), jnp.zeros((DSA_CNT_ROWS, Q_BLOCK), F32))
            cnt = cnt.sum(axis=0, keepdims=True)
            ge = cnt >= n_keep
            lo, hi, n_lo = jnp.where(ge, mid, lo), jnp.where(ge, hi, mid), jnp.where(ge, cnt, n_lo)
        return it + BISECT_UNROLL, lo, hi, n_lo

    n_causal = (q0 + 1 + lax.broadcasted_iota(jnp.int32, (1, Q_BLOCK), 1)).astype(F32)
    it_end, lo, hi, n_lo = lax.while_loop(bisect_more, bisect, (0, lo, hi, n_causal))

    @pl.when(it_end >= BISECT_ITERS)
    def _():
        def min_chunk(c, v):
            x = sc_ref[c]
            return jnp.minimum(v, jnp.where(x >= lo, x, jnp.inf).min(axis=0, keepdims=True))

        vth = lax.fori_loop(0, nc, min_chunk, jnp.full((1, Q_BLOCK), jnp.inf, F32))

        def gt_chunk(c, n):
            return n + jnp.where(sc_ref[c] > vth, 1.0, 0.0).sum(axis=0, keepdims=True)

        room = n_keep - lax.fori_loop(0, nc, gt_chunk, jnp.zeros((1, Q_BLOCK), F32))
        over = n_lo > n_keep

        def drop_chunk(c, seen):
            x = sc_ref[c]
            eq = jnp.where(x == vth, 1.0, 0.0)
            rank = seen + jnp.dot(tri_ref[...], eq.astype(BF16), preferred_element_type=F32)
            sc_ref[c] = jnp.where((eq > 0.5) & (rank > room) & over, -jnp.inf, x)
            return seen + eq.sum(axis=0, keepdims=True)

        lax.fori_loop(0, nc, drop_chunk, jnp.zeros((1, Q_BLOCK), F32))

    def bias_chunk(c, _):
        sc_ref[c] = jnp.where(sc_ref[c] >= lo, 0.0, MASK_NEG)
        return 0

    lax.fori_loop(0, nc, bias_chunk, 0)

    m_ref[...] = jnp.full_like(m_ref, MASK_NEG)
    acc_ref[...] = jnp.zeros_like(acc_ref)
    gl = DSA_HEAD_GROUP * Q_BLOCK

    def attn_chunk(c, _):
        k0 = pl.multiple_of(c * kc, kc)
        kx = kext_ref[0, pl.ds(k0, kc), :]
        ct = ct_ref[0, c]
        bias = jnp.tile(sc_ref[c], (1, DSA_HEAD_GROUP))
        n_slab = DSA_HEADS // DSA_HEAD_GROUP
        s = _nt(kx, qx_ref[0:gl, :]) + bias
        for g in range(n_slab):
            s_next = _nt(kx, qx_ref[(g + 1) * gl:(g + 2) * gl, :]) + bias if g + 1 < n_slab else None
            _online_softmax_step(s, ct, m_ref, acc_ref, slice(g * gl, (g + 1) * gl))
            s = s_next
        return 0

    lax.fori_loop(0, nc, attn_chunk, 0)

    for h in range(DSA_HEADS):
        cols = slice(h * Q_BLOCK, (h + 1) * Q_BLOCK)
        o_lat = (acc_ref[:DSA_KV_RANK, cols] / acc_ref[DSA_KV_RANK:DSA_KV_RANK + 1, cols]).astype(BF16)
        o_ref[0, :, h * DSA_V_DIM:(h + 1) * DSA_V_DIM] = _tn(o_lat, wuv_ref[h]).astype(o_ref.dtype)


def _dsa_attention(qall, wi_l, tabs, kidx, kext, ct, wk, wuv, *, kc):
    b, s, _ = qall.shape
    assert s % kc == 0 and kc % Q_BLOCK == 0
    n_keep = min(DSA_TOPK_MAX, s // 4)
    nq = DSA_HEADS * DSA_QPAD
    ni = IDX_HEADS * IDX_DIM
    assert nq % ni == 0
    nrow = DSA_HEADS * Q_BLOCK
    return pl.pallas_call(
        functools.partial(_dsa_body, kc=kc, n_keep=float(n_keep)),
        grid=(b, s // Q_BLOCK),
        in_specs=[
            pl.BlockSpec((1, Q_BLOCK, nq), lambda bb, i: (bb, i, 0)),
            pl.BlockSpec((1, Q_BLOCK, ni), lambda bb, i: (bb, i, nq // ni)),
            pl.BlockSpec((1, 1, IDX_HEADS, Q_BLOCK), lambda bb, i: (bb, i, 0, 0)),
            pl.BlockSpec((1, Q_BLOCK, 6 * LANES), lambda bb, i: (bb, i, 0)),
            pl.BlockSpec((1, s, IDX_DIM), lambda bb, i: (bb, 0, 0)),
            pl.BlockSpec((1, s, DSA_QX), lambda bb, i: (bb, 0, 0)),
            pl.BlockSpec((1, s // kc, DSA_KV_RANK + SUM_ROWS, kc), lambda bb, i: (bb, 0, 0, 0)),
            pl.BlockSpec((DSA_HEADS, DSA_QPAD, DSA_QX), lambda bb, i: (0, 0, 0)),
            pl.BlockSpec((DSA_HEADS, DSA_KV_RANK, DSA_V_DIM), lambda bb, i: (0, 0, 0)),
            pl.BlockSpec((kc, kc), lambda bb, i: (0, 0)),
        ],
        out_specs=pl.BlockSpec((1, Q_BLOCK, DSA_HEADS * DSA_V_DIM), lambda bb, i: (bb, i, 0)),
        out_shape=jax.ShapeDtypeStruct((b, s, DSA_HEADS * DSA_V_DIM), BF16),
        scratch_shapes=[
            pltpu.VMEM((nrow, DSA_QX), BF16),
            pltpu.VMEM((IDX_HEADS * Q_BLOCK, IDX_DIM), BF16),
            pltpu.VMEM((s // kc, kc, Q_BLOCK), F32),
            pltpu.VMEM((1, nrow), F32),
            pltpu.VMEM((DSA_KV_RANK + SUM_ROWS, nrow), F32),
        ],
        compiler_params=_params("arbitrary", "arbitrary"),
        name="dsa_attn",
    )(qall, qall, wi_l, tabs, kidx, kext, ct, wk, wuv, jnp.tril(jnp.ones((kc, kc), BF16)))


DSA_PROJ_W = (DSA_Q_RANK, DSA_KV_RANK, LANES, LANES, LANES)


def _dsa_kprep_body(ckv_ref, kpe_ref, kid_ref, wid_ref, tab_ref, kvn_ref, lng_ref, lnb_ref,
                    kext_ref, kidx_ref, ct_ref, wi_ref, *, tm):
    tab = tab_ref[...]
    cf, sa, sb = tab[:, 0:LANES], tab[:, LANES:2 * LANES], tab[:, 2 * LANES:3 * LANES]
    cfi, sai, sbi = tab[:, 3 * LANES:4 * LANES], tab[:, 4 * LANES:5 * LANES], tab[:, 5 * LANES:6 * LANES]
    c_n = _rms(ckv_ref[...], kvn_ref[...])
    kext_ref[:, :DSA_KV_RANK] = c_n.astype(BF16)
    kext_ref[:, DSA_KV_RANK:] = _rope_first_vreg(kpe_ref[...], cf, sa, sb, DSA_ROPE // 2).astype(BF16)
    ct_ref[0, 0, :DSA_KV_RANK, :] = c_n.T.astype(BF16)
    ct_ref[0, 0, DSA_KV_RANK:, :] = jnp.ones((SUM_ROWS, tm), BF16)
    x = kid_ref[...]
    mu = jnp.mean(x, axis=-1, keepdims=True)
    var = jnp.mean(jnp.square(x - mu), axis=-1, keepdims=True)
    k_ln = (x - mu) * lax.rsqrt(var + NORM_EPS) * lng_ref[...] + lnb_ref[...]
    kidx_ref[...] = _rope_first_vreg(k_ln, cfi, sai, sbi, IDX_ROPE // 2).astype(BF16)
    w = wid_ref[...] * (IDX_HEADS ** -0.5 * IDX_DIM ** -0.5)
    for r in range(tm // Q_BLOCK):
        wi_ref[0, r] = w[r * Q_BLOCK:(r + 1) * Q_BLOCK, :].T[:IDX_HEADS, :]


def _dsa_kprep(proj, tabs, kv_norm, ln_g, ln_b, bsz, seq, kc):
    t = proj.shape[0]
    tm = kc
    nt = seq // tm
    col = [int(c) for c in np.cumsum((0,) + DSA_PROJ_W)]
    assert all(c % w == 0 for c, w in zip(col[1:-1], DSA_PROJ_W[1:]))

    def row(v):
        return v.reshape(1, -1)

    return pl.pallas_call(
        functools.partial(_dsa_kprep_body, tm=tm),
        grid=(t // tm,),
        in_specs=[
            pl.BlockSpec((tm, DSA_KV_RANK), lambda i: (i, col[1] // DSA_KV_RANK)),
            pl.BlockSpec((tm, LANES), lambda i: (i, col[2] // LANES)),
            pl.BlockSpec((tm, LANES), lambda i: (i, col[3] // LANES)),
            pl.BlockSpec((tm, LANES), lambda i: (i, col[4] // LANES)),
            pl.BlockSpec((tm, 6 * LANES), lambda i: (i, 0)),
            pl.BlockSpec((1, DSA_KV_RANK), lambda i: (0, 0)),
            pl.BlockSpec((1, IDX_DIM), lambda i: (0, 0)),
            pl.BlockSpec((1, IDX_DIM), lambda i: (0, 0)),
        ],
        out_specs=[
            pl.BlockSpec((tm, DSA_QX), lambda i: (i, 0)),
            pl.BlockSpec((tm, IDX_DIM), lambda i: (i, 0)),
            pl.BlockSpec((1, 1, DSA_KV_RANK + SUM_ROWS, tm), lambda i: (i // nt, i % nt, 0, 0)),
            pl.BlockSpec((1, tm // Q_BLOCK, IDX_HEADS, Q_BLOCK), lambda i: (i // nt, i % nt, 0, 0)),
        ],
        out_shape=[
            jax.ShapeDtypeStruct((t, DSA_QX), BF16),
            jax.ShapeDtypeStruct((t, IDX_DIM), BF16),
            jax.ShapeDtypeStruct((bsz, nt, DSA_KV_RANK + SUM_ROWS, tm), BF16),
            jax.ShapeDtypeStruct((bsz, seq // Q_BLOCK, IDX_HEADS, Q_BLOCK), F32),
        ],
        compiler_params=_params("arbitrary"),
        name="dsa_kprep",
    )(proj, proj, proj, proj, tabs, row(kv_norm), row(ln_g), row(ln_b))


def _dsa_mixer(h2, bsz, seq, positions, attn_norm, w_in, q_norm, kv_norm, idx_ln_g, idx_ln_b,
               w_qup, w_uk, w_uv, w_out, *, kc=512):
    t = h2.shape[0]
    kc = min(kc, seq)
    sizes = (DSA_Q_RANK, DSA_KV_RANK, DSA_ROPE, IDX_DIM, IDX_HEADS)
    cuts = np.cumsum((0,) + sizes)
    w_in_p = jnp.concatenate(
        [jnp.pad(w_in[:, int(a):int(b)], ((0, 0), (0, wp - (int(b) - int(a)))))
         for a, b, wp in zip(cuts[:-1], cuts[1:], DSA_PROJ_W)], axis=1).astype(BF16)
    proj = _mm(h2, w_in_p, gain=attn_norm, tm=1024, tn=w_in_p.shape[1])

    nq = DSA_HEADS * DSA_QK_DIM
    w_q = w_qup[:, :nq].reshape(DSA_Q_RANK, DSA_HEADS, DSA_QK_DIM)
    w_q = jnp.pad(w_q, ((0, 0), (0, 0), (0, DSA_QPAD - DSA_QK_DIM))).reshape(DSA_Q_RANK, DSA_HEADS * DSA_QPAD)
    w_qp = jnp.concatenate([w_q, w_qup[:, nq:]], axis=1).astype(BF16)
    qall = _mm(proj, w_qp, gain=q_norm, out_dtype=BF16, tn=w_qp.shape[1])

    pos = positions.reshape(t)
    tabs = jnp.concatenate([_rope_lane_tables(pos, DSA_ROPE), _rope_lane_tables(pos, IDX_ROPE)], -1)
    kext, kidx, ct, wi = _dsa_kprep(proj, tabs, kv_norm, idx_ln_g, idx_ln_b, bsz, seq, kc)

    wk = jnp.zeros((DSA_HEADS, DSA_QPAD, DSA_QX), F32)
    wk = wk.at[:, DSA_ROPE:DSA_QK_DIM, :DSA_KV_RANK].set(jnp.swapaxes(w_uk, 1, 2))
    eye = jnp.eye(DSA_ROPE, dtype=F32)
    wk = wk.at[:, :DSA_ROPE, DSA_KV_RANK:DSA_KV_RANK + DSA_ROPE].set(jnp.broadcast_to(eye, (DSA_HEADS,) + eye.shape))

    o = _dsa_attention(
        qall.reshape(bsz, seq, -1), wi, tabs.reshape(bsz, seq, -1),
        kidx.reshape(bsz, seq, IDX_DIM), kext.reshape(bsz, seq, DSA_QX), ct,
        wk.astype(BF16), w_uv.astype(BF16), kc=kc)
    return _mm(o.reshape(t, -1), w_out.astype(BF16), res=h2, tm=1024)


FOX_KX = MXU_DIM
FOX_HEAD_PAIR = 2


def _fox_body(q_ref, k_ref, v_ref, c3_ref, e_ref, o_ref, qx_ref, kx_ref, vt_ref, m_ref, acc_ref, *, tq, kc):
    i = pl.program_id(2)
    q0 = i * tq
    n_full = q0 // kc
    n_diag = tq // kc

    @pl.when(i == 0)
    def _():
        for a in range(FOX_HEAD_PAIR):
            kx_ref[a, :, :FOX_DIM] = k_ref[0, :, a * FOX_DIM:(a + 1) * FOX_DIM]
            kx_ref[a, :, FOX_DIM:] = _tn(c3_ref[0, a], e_ref[...]).astype(BF16)
            for c in range(vt_ref.shape[1]):
                vc = v_ref[0, c * kc:(c + 1) * kc, a * FOX_DIM:(a + 1) * FOX_DIM]
                vt_ref[a, c, :FOX_DIM, :] = vc.astype(F32).T.astype(BF16)
                vt_ref[a, c, FOX_DIM:, :] = jnp.ones((SUM_ROWS, kc), BF16)

    lane = lax.broadcasted_iota(jnp.int32, (tq, FOX_KX - FOX_DIM), 1)
    for a in range(FOX_HEAD_PAIR):
        qx_ref[a, :, :FOX_DIM] = q_ref[0, :, a * FOX_DIM:(a + 1) * FOX_DIM]
        qx_ref[a, :, FOX_DIM:] = jnp.where(lane < 3, 1.0, 0.0).astype(BF16)
    m_ref[...] = jnp.full_like(m_ref, MASK_NEG)
    acc_ref[...] = jnp.zeros_like(acc_ref)
    key_i = lax.broadcasted_iota(jnp.int32, (kc, tq), 0)
    t_l = q0 + lax.broadcasted_iota(jnp.int32, (kc, tq), 1)

    def step(c, masked):
        k0 = pl.multiple_of(c * kc, kc)

        def logits(a):
            s = _nt(kx_ref[a, pl.ds(k0, kc), :], qx_ref[a])
            return jnp.where((k0 + key_i) <= t_l, s, MASK_NEG) if masked else s

        s = logits(0)
        for a in range(FOX_HEAD_PAIR):
            s_next = logits(a + 1) if a + 1 < FOX_HEAD_PAIR else None
            _online_softmax_step(s, vt_ref[a, c], m_ref, acc_ref, slice(a * tq, (a + 1) * tq))
            s = s_next

    def full_chunk(c, _):
        step(c, False)
        return 0

    def diag_chunk(c, _):
        step(c, True)
        return 0

    lax.fori_loop(0, n_full, full_chunk, 0)
    lax.fori_loop(n_full, n_full + n_diag, diag_chunk, 0)
    for a in range(FOX_HEAD_PAIR):
        cols = slice(a * tq, (a + 1) * tq)
        o_a = acc_ref[:FOX_DIM, cols] / acc_ref[FOX_DIM:FOX_DIM + 1, cols]
        o_ref[0, :, a * FOX_DIM:(a + 1) * FOX_DIM] = o_a.T.astype(o_ref.dtype)


def _fox_attention(qkv, c3, *, tq, kc):
    b, s, _ = qkv.shape
    hp = FOX_HEAD_PAIR
    npair = FOX_HEADS // hp
    assert s % tq == 0 and tq % kc == 0 and FOX_HEADS % hp == 0
    e = jnp.eye(SUBLANES, FOX_KX - FOX_DIM, dtype=BF16)
    return pl.pallas_call(
        functools.partial(_fox_body, tq=tq, kc=kc),
        grid=(b, npair, s // tq),
        in_specs=[
            pl.BlockSpec((1, tq, hp * FOX_DIM), lambda bb, h, i: (bb, i, h)),
            pl.BlockSpec((1, s, hp * FOX_DIM), lambda bb, h, i: (bb, 0, npair + h)),
            pl.BlockSpec((1, s, hp * FOX_DIM), lambda bb, h, i: (bb, 0, 2 * npair + h)),
            pl.BlockSpec((1, hp, SUBLANES, s), lambda bb, h, i: (bb, h, 0, 0)),
            pl.BlockSpec((SUBLANES, FOX_KX - FOX_DIM), lambda bb, h, i: (0, 0)),
        ],
        out_specs=pl.BlockSpec((1, tq, hp * FOX_DIM), lambda bb, h, i: (bb, i, h)),
        out_shape=jax.ShapeDtypeStruct((b, s, FOX_HEADS * FOX_DIM), BF16),
        scratch_shapes=[pltpu.VMEM((hp, tq, FOX_KX), BF16), pltpu.VMEM((hp, s, FOX_KX), BF16),
                        pltpu.VMEM((hp, s // kc, FOX_DIM + SUM_ROWS, kc), BF16),
                        pltpu.VMEM((1, hp * tq), F32), pltpu.VMEM((FOX_DIM + SUM_ROWS, hp * tq), F32)],
        compiler_params=_params("arbitrary", "arbitrary", "arbitrary"),
        name="fox_attn",
    )(qkv, qkv, qkv, c3, e)


def _fox_mixer(h2, bsz, seq, attn_norm, w_in, b_f, w_out, *, tq=512, kc=512):
    t = h2.shape[0]
    hd = FOX_HEADS * FOX_DIM
    tq = min(tq, seq)
    kc = min(kc, tq)
    scale = FOX_DIM ** -0.5 * LOG2E
    w_qkv = jnp.concatenate([w_in[:, :hd] * scale, w_in[:, hd:3 * hd]], axis=1).astype(BF16)
    qkv = _mm(h2, w_qkv, gain=attn_norm, out_dtype=BF16, tm=1024, tn=1024)
    w_f = jnp.pad(w_in[:, 3 * hd:], ((0, 0), (0, LANES - FOX_HEADS))).astype(BF16)
    f_logit = _mm(h2, w_f, gain=attn_norm)[:, :FOX_HEADS]
    log_f = jax.nn.log_sigmoid(f_logit + b_f)
    cum = jnp.cumsum(log_f.reshape(bsz, seq, FOX_HEADS), axis=1)
    c3 = jnp.stack(_split3_bits(-cum * LOG2E), axis=0).transpose(1, 3, 0, 2)
    c3 = jnp.pad(c3, ((0, 0), (0, 0), (0, SUBLANES - 3), (0, 0)))
    o = _fox_attention(qkv.reshape(bsz, seq, 3 * hd), c3, tq=tq, kc=kc)
    return _mm(o.reshape(t, hd), w_out.astype(BF16), res=h2, tm=1024)


NSA_HEAD_GROUP = 4


def _cmp_body(x_ref, pe_ref, w1_ref, b1_ref, w2_ref, o_ref, *, half):
    x = x_ref[0, 0]
    w_lo = w1_ref[:half, :]
    w_hi = w1_ref[half:, :]
    a = jnp.dot(x, w_lo, preferred_element_type=F32)
    bnext = jnp.dot(x, w_hi, preferred_element_type=F32)
    n = x.shape[0]
    bnext = pltpu.roll(bnext, n - 1, 0)
    pe = pe_ref[...]
    pe_b = (jnp.dot(pe[:, :half], w_lo, preferred_element_type=F32)
            + jnp.dot(pe[:, half:], w_hi, preferred_element_type=F32))[0:1, :]
    hid = jax.nn.gelu(a + bnext + pe_b + b1_ref[...])
    o_ref[0, 0] = jnp.dot(hid.astype(BF16), w2_ref[...], preferred_element_type=F32).astype(o_ref.dtype)


def _compress(x, pe, w1, b1, w2):
    b, g, n, kd = x.shape
    dout = w2.shape[1]
    pe8 = jnp.broadcast_to(pe.reshape(1, 2 * kd), (8, 2 * kd)).astype(BF16)
    return pl.pallas_call(
        functools.partial(_cmp_body, half=kd),
        grid=(b, g),
        in_specs=[
            pl.BlockSpec((1, 1, n, kd), lambda bb, gg: (bb, gg, 0, 0)),
            pl.BlockSpec((8, 2 * kd), lambda bb, gg: (0, 0)),
            pl.BlockSpec((2 * kd, CMP_HIDDEN), lambda bb, gg: (0, 0)),
            pl.BlockSpec((1, CMP_HIDDEN), lambda bb, gg: (0, 0)),
            pl.BlockSpec((CMP_HIDDEN, dout), lambda bb, gg: (0, 0)),
        ],
        out_specs=pl.BlockSpec((1, 1, n, dout), lambda bb, gg: (bb, gg, 0, 0)),
        out_shape=jax.ShapeDtypeStruct((b, g, n, dout), BF16),
        compiler_params=_params("arbitrary", "arbitrary"),
        name="nsa_compress",
    )(x, pe8, w1.astype(BF16), b1.reshape(1, CMP_HIDDEN), w2.astype(BF16))


def _nsa_body(q_ref, tab_ref, gate_ref, kc_ref, vct_ref, ks_ref, vst_ref, kw_ref, vwt_ref, ov_ref, exp_ref,
              o_ref, qs_ref, m_ref, acc_ref, *, kc, n_cmp, n_slc, n_sel):
    i = pl.program_id(2)
    q0 = i * Q_BLOCK
    nc = (q0 + Q_BLOCK + kc - 1) // kc
    jh = NSA_HPG
    scale = NSA_QK_DIM ** -0.5 * LOG2E
    tab = tab_ref[0]
    cf, sa, sb = tab[:, 0:LANES], tab[:, LANES:2 * LANES], tab[:, 2 * LANES:3 * LANES]

    for j in range(jh):
        xj = q_ref[0, :, j * NSA_QK_DIM:(j + 1) * NSA_QK_DIM].astype(F32)
        r0 = _rope_first_vreg(xj[:, :LANES], cf, sa, sb, NSA_ROPE // 2)
        qs_ref[j * Q_BLOCK:(j + 1) * Q_BLOCK, :] = (jnp.concatenate([r0, xj[:, LANES:]], axis=1) * scale).astype(BF16)
    qs = qs_ref[...]

    n_id = lax.broadcasted_iota(jnp.int32, (n_cmp, Q_BLOCK), 0)
    t_c = q0 + lax.broadcasted_iota(jnp.int32, (n_cmp, Q_BLOCK), 1)
    cbias = jnp.where((n_id * CMP_STRIDE + (CMP_BLOCK - 1)) <= t_c, 0.0, MASK_NEG)
    pc = _softmax_cols(_nt(kc_ref[0, 0], qs) + jnp.tile(cbias, (1, jh)))
    o_c = jnp.dot(vct_ref[0, 0], pc.astype(BF16), preferred_element_type=F32)

    pcs = pc[:, 0:Q_BLOCK]
    for j in range(1, jh):
        pcs = pcs + pc[:, j * Q_BLOCK:(j + 1) * Q_BLOCK]
    ov = ov_ref[...]
    imp = sum(jnp.dot(ov, term, preferred_element_type=F32) for term in _split3(pcs))

    blk = lax.broadcasted_iota(jnp.int32, (n_slc, Q_BLOCK), 0)
    t_b = q0 + lax.broadcasted_iota(jnp.int32, (n_slc, Q_BLOCK), 1)
    cur = lax.shift_right_logical(t_b, int(np.log2(SLC_BLOCK)))
    causal_blk = blk * SLC_BLOCK <= t_b
    forced = (blk == 0) | (blk == cur) | (blk == cur - 1)
    val = jnp.where(causal_blk, jnp.where(forced, jnp.inf, imp), -jnp.inf)
    rank = jnp.zeros((n_slc, Q_BLOCK), F32)
    for mp in range(n_slc):
        vrow = val[mp:mp + 1, :]
        before = (vrow > val) | ((vrow == val) & (blk > mp))
        rank = rank + jnp.where(before, 1.0, 0.0)
    sel = jnp.where((rank < n_sel) & causal_blk, 1.0, 0.0)
    sel_p = jnp.concatenate([sel, jnp.zeros((LANES - n_slc, Q_BLOCK), F32)], axis=0).astype(BF16)

    m_ref[...] = jnp.full_like(m_ref, MASK_NEG)
    acc_ref[...] = jnp.zeros_like(acc_ref)
    key_i = lax.broadcasted_iota(jnp.int32, (kc, Q_BLOCK), 0)
    t_k = q0 + lax.broadcasted_iota(jnp.int32, (kc, Q_BLOCK), 1)
    gl = NSA_HEAD_GROUP * Q_BLOCK

    def sel_chunk(c, _):
        k0 = pl.multiple_of(c * kc, kc)
        hit = jnp.dot(exp_ref[pl.ds(k0, kc), :], sel_p, preferred_element_type=F32)
        bias = jnp.where((hit > 0.5) & ((k0 + key_i) <= t_k), 0.0, MASK_NEG)
        bias = jnp.tile(bias, (1, NSA_HEAD_GROUP))
        ksc = ks_ref[0, 0, pl.ds(k0, kc), :]
        vt = vst_ref[0, 0, c]
        n_slab = jh // NSA_HEAD_GROUP
        s = _nt(ksc, qs_ref[0:gl, :]) + bias
        for hg in range(n_slab):
            s_next = _nt(ksc, qs_ref[(hg + 1) * gl:(hg + 2) * gl, :]) + bias if hg + 1 < n_slab else None
            _online_softmax_step(s, vt, m_ref, acc_ref, slice(hg * gl, (hg + 1) * gl))
            s = s_next
        return 0

    lax.fori_loop(0, nc, sel_chunk, 0)
    o_s = acc_ref[:NSA_V_DIM, :] / acc_ref[NSA_V_DIM:NSA_V_DIM + 1, :]

    wl = WINDOW + Q_BLOCK
    kstart = pl.multiple_of(q0, Q_BLOCK)
    s_pos = q0 - WINDOW + lax.broadcasted_iota(jnp.int32, (wl, Q_BLOCK), 0)
    t_w = q0 + lax.broadcasted_iota(jnp.int32, (wl, Q_BLOCK), 1)
    wbias = jnp.where((s_pos >= 0) & (s_pos <= t_w) & (s_pos > t_w - WINDOW), 0.0, MASK_NEG)
    pw = _softmax_cols(_nt(kw_ref[0, 0, pl.ds(kstart, wl), :], qs) + jnp.tile(wbias, (1, jh))).astype(BF16)
    o_w = jnp.dot(vwt_ref[0, 0, i], pw[0:Q_BLOCK, :], preferred_element_type=F32)
    for c in range(1, wl // Q_BLOCK):
        o_w = o_w + jnp.dot(vwt_ref[0, 0, i + c], pw[c * Q_BLOCK:(c + 1) * Q_BLOCK, :],
                            preferred_element_type=F32)

    g = gate_ref[0, 0]
    for j in range(jh):
        cj = slice(j * Q_BLOCK, (j + 1) * Q_BLOCK)
        out = (g[j:j + 1, :] * o_c[:, cj] + g[jh + j:jh + j + 1, :] * o_s[:, cj]
               + g[2 * jh + j:2 * jh + j + 1, :] * o_w[:, cj])
        o_ref[0, :, j * NSA_V_DIM:(j + 1) * NSA_V_DIM] = out.T.astype(o_ref.dtype)


def _nsa_attention(q, tabs, gates, kcmp, vcmp_t, ks, vs_t, kw, vw_t, *, kc):
    b, s, _ = q.shape
    g = NSA_GROUPS
    assert s % kc == 0 and kc % SLC_BLOCK == 0
    n_cmp = kcmp.shape[2]
    n_slc = s // SLC_BLOCK
    assert n_slc <= LANES
    n_sel = min(SLC_TOPK, n_slc)
    cmp_start = np.arange(n_cmp) * CMP_STRIDE
    slc_start = np.arange(n_slc) * SLC_BLOCK
    ov = ((cmp_start[None, :] < slc_start[:, None] + SLC_BLOCK)
          & (cmp_start[None, :] + CMP_BLOCK > slc_start[:, None])).astype(np.float32)
    expand = (np.arange(s)[:, None] // SLC_BLOCK == np.arange(LANES)[None, :]).astype(np.float32)
    qw = NSA_HPG * NSA_QK_DIM
    ow = NSA_HPG * NSA_V_DIM
    nrow = NSA_HPG * Q_BLOCK
    wl = s + WINDOW
    return pl.pallas_call(
        functools.partial(_nsa_body, kc=kc, n_cmp=n_cmp, n_slc=n_slc, n_sel=float(n_sel)),
        grid=(b, g, s // Q_BLOCK),
        in_specs=[
            pl.BlockSpec((1, Q_BLOCK, qw), lambda bb, gg, i: (bb, i, gg)),
            pl.BlockSpec((1, Q_BLOCK, 3 * LANES), lambda bb, gg, i: (bb, i, 0)),
            pl.BlockSpec((1, 1, 3 * NSA_HPG, Q_BLOCK), lambda bb, gg, i: (bb, gg, 0, i)),
            pl.BlockSpec((1, 1, n_cmp, NSA_QK_DIM), lambda bb, gg, i: (bb, gg, 0, 0)),
            pl.BlockSpec((1, 1, NSA_V_DIM, n_cmp), lambda bb, gg, i: (bb, gg, 0, 0)),
            pl.BlockSpec((1, 1, s, NSA_QK_DIM), lambda bb, gg, i: (bb, gg, 0, 0)),
            pl.BlockSpec((1, 1, s // kc, NSA_V_DIM + SUM_ROWS, kc), lambda bb, gg, i: (bb, gg, 0, 0, 0)),
            pl.BlockSpec((1, 1, wl, NSA_QK_DIM), lambda bb, gg, i: (bb, gg, 0, 0)),
            pl.BlockSpec((1, 1, wl // Q_BLOCK, NSA_V_DIM, Q_BLOCK), lambda bb, gg, i: (bb, gg, 0, 0, 0)),
            pl.BlockSpec((n_slc, n_cmp), lambda bb, gg, i: (0, 0)),
            pl.BlockSpec((s, LANES), lambda bb, gg, i: (0, 0)),
        ],
        out_specs=pl.BlockSpec((1, Q_BLOCK, ow), lambda bb, gg, i: (bb, i, gg)),
        out_shape=jax.ShapeDtypeStruct((b, s, NSA_HEADS * NSA_V_DIM), BF16),
        scratch_shapes=[
            pltpu.VMEM((nrow, NSA_QK_DIM), BF16),
            pltpu.VMEM((1, nrow), F32),
            pltpu.VMEM((NSA_V_DIM + SUM_ROWS, nrow), F32),
        ],
        compiler_params=_params("arbitrary", "arbitrary", "arbitrary"),
        name="nsa_attn",
    )(q, tabs, gates, kcmp, vcmp_t, ks, vs_t, kw, vw_t, jnp.asarray(ov, BF16), jnp.asarray(expand, BF16))


def _nsa_mixer(h2, bsz, seq, positions, attn_norm, w_in, k_pe, k_w1, k_b1, k_w2, v_pe, v_w1, v_b1, v_w2, w_out,
               *, kc=1024):
    t = h2.shape[0]
    kc = min(kc, seq)
    g, jh, dk, dv = NSA_GROUPS, NSA_HPG, NSA_QK_DIM, NSA_V_DIM
    nq = NSA_HEADS * dk
    n_rest = w_in.shape[1] - nq
    n_rest_p = -(-n_rest // 512) * 512
    q = _mm(h2, w_in[:, :nq].astype(BF16), gain=attn_norm, out_dtype=BF16, tm=1024, tn=1024)
    w_rest = jnp.pad(w_in[:, nq:], ((0, 0), (0, n_rest_p - n_rest))).astype(BF16)
    rest = _mm(h2, w_rest, gain=attn_norm, tm=1024, tn=1024)
    offs = np.cumsum([0, NSA_KD, NSA_VD, NSA_KD, NSA_VD, NSA_KD, NSA_VD, 3 * NSA_HEADS])
    k_c, v_c, k_s, v_s, k_w, v_w, g_logit = [rest[:, int(a):int(b)] for a, b in zip(offs[:-1], offs[1:])]

    pos = positions.reshape(t)
    cos, sin = _rope_cos_sin(pos, NSA_ROPE)

    def rope_k(k):
        return _rope_glue(k.reshape(t, g, dk), cos[:, None, :], sin[:, None, :])

    def per_group(x, d):
        return x.reshape(bsz, seq, g, d).transpose(0, 2, 1, 3).astype(BF16)

    def chunks(x, d):
        x = x.reshape(bsz, seq // CMP_STRIDE, CMP_STRIDE, g, d).transpose(0, 3, 1, 2, 4)
        return x.reshape(bsz, g, seq // CMP_STRIDE, CMP_STRIDE * d).astype(BF16)

    kcmp = _compress(chunks(rope_k(k_c), dk), k_pe, k_w1, k_b1, k_w2)
    vcmp_t = jnp.swapaxes(_compress(chunks(v_c.reshape(t, g, dv), dv), v_pe, v_w1, v_b1, v_w2), -1, -2)
    ks = per_group(rope_k(k_s), dk)
    vs_t = _with_sum_rows(_chunks_t(per_group(v_s.reshape(t, g, dv), dv), kc))
    pad = ((0, 0), (0, 0), (WINDOW, 0), (0, 0))
    kw = jnp.pad(per_group(rope_k(k_w), dk), pad)
    vw_t = _chunks_t(jnp.pad(per_group(v_w.reshape(t, g, dv), dv), pad), Q_BLOCK)
    gates = jax.nn.sigmoid(g_logit).reshape(bsz, seq, 3, g, jh).transpose(0, 3, 2, 4, 1).reshape(bsz, g, 3 * jh, seq)
    tabs = _rope_lane_tables(pos, NSA_ROPE).reshape(bsz, seq, 3 * LANES)

    o = _nsa_attention(q.reshape(bsz, seq, nq), tabs, gates, kcmp, vcmp_t, ks, vs_t, kw, vw_t, kc=kc)
    return _mm(o.reshape(t, NSA_HEADS * dv), w_out.astype(BF16), res=h2, tm=1024)


def _norm_body(x_ref, g_ref, o_ref):
    o_ref[...] = _rms(x_ref[...], g_ref[...])


def _final_norm(h2, gain, *, tm=512):
    t, d = h2.shape
    tm = min(tm, t)
    return pl.pallas_call(
        _norm_body,
        grid=(t // tm,),
        in_specs=[pl.BlockSpec((tm, d), lambda i: (i, 0)), pl.BlockSpec((1, d), lambda i: (0, 0))],
        out_specs=pl.BlockSpec((tm, d), lambda i: (i, 0)),
        out_shape=jax.ShapeDtypeStruct((t, d), F32),
        compiler_params=_params("arbitrary"),
        name="final_norm",
    )(h2, gain.reshape(1, d))


def _ffn_layer(h2, seq, ffn_norm, w_up, conv_w, conv_b, w_down):
    return _ffn(h2, seq, ffn_norm, w_up.astype(BF16), conv_w, conv_b, w_down.astype(BF16))


def kernel(x, positions, l0_attn_norm, l0_dsa_w_in, l0_dsa_q_norm, l0_dsa_kv_norm, l0_dsa_idx_ln_g, l0_dsa_idx_ln_b, l0_dsa_w_qup, l0_dsa_w_uk, l0_dsa_w_uv, l0_dsa_w_out, l0_ffn_norm, l0_ffn_up, l0_ffn_conv_w, l0_ffn_conv_b, l0_ffn_down, l1_attn_norm, l1_fox_w_in, l1_fox_b_f, l1_fox_w_out, l1_ffn_norm, l1_ffn_up, l1_ffn_conv_w, l1_ffn_conv_b, l1_ffn_down, l2_attn_norm, l2_nsa_w_in, l2_nsa_k_pe, l2_nsa_k_w1, l2_nsa_k_b1, l2_nsa_k_w2, l2_nsa_v_pe, l2_nsa_v_w1, l2_nsa_v_b1, l2_nsa_v_w2, l2_nsa_w_out, l2_ffn_norm, l2_ffn_up, l2_ffn_conv_w, l2_ffn_conv_b, l2_ffn_down, l3_attn_norm, l3_dsa_w_in, l3_dsa_q_norm, l3_dsa_kv_norm, l3_dsa_idx_ln_g, l3_dsa_idx_ln_b, l3_dsa_w_qup, l3_dsa_w_uk, l3_dsa_w_uv, l3_dsa_w_out, l3_ffn_norm, l3_ffn_up, l3_ffn_conv_w, l3_ffn_conv_b, l3_ffn_down, final_norm):
    bsz, seq, d = x.shape
    h = x.reshape(bsz * seq, d)
    h = _dsa_mixer(h, bsz, seq, positions, l0_attn_norm, l0_dsa_w_in, l0_dsa_q_norm, l0_dsa_kv_norm,
                   l0_dsa_idx_ln_g, l0_dsa_idx_ln_b, l0_dsa_w_qup, l0_dsa_w_uk, l0_dsa_w_uv, l0_dsa_w_out)
    h = _ffn_layer(h, seq, l0_ffn_norm, l0_ffn_up, l0_ffn_conv_w, l0_ffn_conv_b, l0_ffn_down)
    h = _fox_mixer(h, bsz, seq, l1_attn_norm, l1_fox_w_in, l1_fox_b_f, l1_fox_w_out)
    h = _ffn_layer(h, seq, l1_ffn_norm, l1_ffn_up, l1_ffn_conv_w, l1_ffn_conv_b, l1_ffn_down)
    h = _nsa_mixer(h, bsz, seq, positions, l2_attn_norm, l2_nsa_w_in, l2_nsa_k_pe, l2_nsa_k_w1, l2_nsa_k_b1,
                   l2_nsa_k_w2, l2_nsa_v_pe, l2_nsa_v_w1, l2_nsa_v_b1, l2_nsa_v_w2, l2_nsa_w_out)
    h = _ffn_layer(h, seq, l2_ffn_norm, l2_ffn_up, l2_ffn_conv_w, l2_ffn_conv_b, l2_ffn_down)
    h = _dsa_mixer(h, bsz, seq, positions, l3_attn_norm, l3_dsa_w_in, l3_dsa_q_norm, l3_dsa_kv_norm,
                   l3_dsa_idx_ln_g, l3_dsa_idx_ln_b, l3_dsa_w_qup, l3_dsa_w_uk, l3_dsa_w_uv, l3_dsa_w_out)
    h = _ffn_layer(h, seq, l3_ffn_norm, l3_ffn_up, l3_ffn_conv_w, l3_ffn_conv_b, l3_ffn_down)
    return _final_norm(h, final_norm).reshape(bsz, seq, d)
```

```python
import functools

import jax
import jax.numpy as jnp
import numpy as np
from jax import lax
from jax.experimental import pallas as pl
from jax.experimental.pallas import tpu as pltpu

F32 = jnp.float32
BF16 = jnp.bfloat16

ROPE_THETA = 500000.0
NORM_EPS = 1e-6
Q_BLOCK = 128

DSA_HEADS = 32
DSA_Q_RANK = 512
DSA_KV_RANK = 256
DSA_QK_DIM = 192
DSA_ROPE = 48
DSA_NOPE = DSA_QK_DIM - DSA_ROPE
DSA_V_DIM = 128
IDX_HEADS = 16
IDX_DIM = 128
IDX_ROPE = 32
DSA_TOPK_MAX = 256
DSA_IN = DSA_Q_RANK + DSA_KV_RANK + DSA_ROPE + IDX_DIM + IDX_HEADS

FOX_HEADS = 16
FOX_DIM = 128

NSA_HEADS = 48
NSA_GROUPS = 4
NSA_HPG = NSA_HEADS // NSA_GROUPS
NSA_QK_DIM = 192
NSA_ROPE = 48
NSA_V_DIM = 128
CMP_BLOCK = 32
CMP_STRIDE = 16
CMP_HIDDEN = 256
SLC_BLOCK = 64
SLC_TOPK = 16
WINDOW = 512
NSA_KD = NSA_GROUPS * NSA_QK_DIM
NSA_VD = NSA_GROUPS * NSA_V_DIM

CONV_WIDTH = 3

LANES = 128
SUBLANES = 8
BF16_SUBLANES = 16
MXU_DIM = 256
VMEM_LIMIT = 56 * 1024 * 1024

MASK_NEG = -1e30
SUM_ROWS = BF16_SUBLANES
LOG2E = 1.4426950408889634
BISECT_ITERS = 32
BISECT_UNROLL = 4


def _params(*sem):
    return pltpu.CompilerParams(dimension_semantics=sem, vmem_limit_bytes=VMEM_LIMIT)


def _nt(a, b):
    return lax.dot_general(a, b, (((1,), (1,)), ((), ())), preferred_element_type=F32)


def _tn(a, b):
    return lax.dot_general(a, b, (((0,), (0,)), ((), ())), preferred_element_type=F32)


def _rms(x, g):
    return x * lax.rsqrt(jnp.mean(x * x, axis=-1, keepdims=True) + NORM_EPS) * g


def _rope_first_vreg(x0, cf, sa, sb, half):
    return (x0 * cf + pltpu.roll(x0, LANES - half, 1) * sa + pltpu.roll(x0, half, 1) * sb)


def _split3(x):
    hi = x.astype(BF16)
    r1 = x - hi.astype(F32)
    mid = r1.astype(BF16)
    lo = (r1 - mid.astype(F32)).astype(BF16)
    return hi, mid, lo


def _split3_bits(x):
    def trunc(v):
        bits = lax.bitcast_convert_type(v, jnp.uint32) & jnp.uint32(0xFFFF0000)
        return lax.bitcast_convert_type(bits, F32)

    hi = trunc(x)
    r1 = x - hi
    mid = trunc(r1)
    lo = r1 - mid
    return hi.astype(BF16), mid.astype(BF16), lo.astype(BF16)


def _online_softmax_step(s, pv_lhs, m_ref, acc_ref, cols):
    m_prev = m_ref[:, cols]
    m_new = jnp.maximum(m_prev, s.max(axis=0, keepdims=True))
    alpha = jnp.exp2(m_prev - m_new)
    p = jnp.exp2(s - m_new).astype(BF16)
    acc_ref[:, cols] = alpha * acc_ref[:, cols] + jnp.dot(pv_lhs, p, preferred_element_type=F32)
    m_ref[:, cols] = m_new


def _softmax_cols(z):
    mx = z.max(axis=0, keepdims=True)
    e = jnp.exp2(z - mx)
    den = e.sum(axis=0, keepdims=True)
    return e * jnp.where(mx > 0.5 * MASK_NEG, 1.0 / den, 0.0)


def _mm_body(*refs, has_gain, has_res):
    x_ref, w_ref = refs[0], refs[1]
    k = 2
    g_ref = r_ref = None
    if has_gain:
        g_ref = refs[k]
        k += 1
    if has_res:
        r_ref = refs[k]
        k += 1
    o_ref = refs[k]
    if has_gain:
        xn_ref = refs[k + 1]

        @pl.when(pl.program_id(1) == 0)
        def _():
            xn_ref[...] = _rms(x_ref[...].astype(F32), g_ref[...]).astype(BF16)

        a = xn_ref[...]
    else:
        a = x_ref[...].astype(BF16)
    acc = jnp.dot(a, w_ref[...], preferred_element_type=F32)
    if has_res:
        acc = acc + r_ref[...]
    o_ref[...] = acc.astype(o_ref.dtype)


def _mm(x, w, *, gain=None, res=None, out_dtype=F32, x_col_block=0, tm=512, tn=512):
    m = x.shape[0]
    k, n = w.shape
    tm = min(tm, m)
    tn = min(tn, n)
    assert m % tm == 0 and n % tn == 0 and x.shape[1] >= (x_col_block + 1) * k
    in_specs = [pl.BlockSpec((tm, k), lambda i, j: (i, x_col_block)),
                pl.BlockSpec((k, tn), lambda i, j: (0, j))]
    args = [x, w]
    scratch = []
    if gain is not None:
        in_specs.append(pl.BlockSpec((1, k), lambda i, j: (0, 0)))
        args.append(gain.reshape(1, k).astype(F32))
        scratch.append(pltpu.VMEM((tm, k), BF16))
    if res is not None:
        in_specs.append(pl.BlockSpec((tm, tn), lambda i, j: (i, j)))
        args.append(res)
    return pl.pallas_call(
        functools.partial(_mm_body, has_gain=gain is not None, has_res=res is not None),
        grid=(m // tm, n // tn),
        in_specs=in_specs,
        out_specs=pl.BlockSpec((tm, tn), lambda i, j: (i, j)),
        out_shape=jax.ShapeDtypeStruct((m, n), out_dtype),
        scratch_shapes=scratch,
        compiler_params=_params("arbitrary", "arbitrary"),
        name="mm",
    )(*args)


FFN_HALO = BF16_SUBLANES


def _ffn_body(h_ref, hp_ref, g_ref, wg_ref, wv_ref, cwg_ref, cwv_ref, cbg_ref, cbv_ref, wd_ref,
              o_ref, xn_ref, ug_ref, uv_ref, acc_ref, *, tm, seq):
    i = pl.program_id(0)
    j = pl.program_id(1)

    @pl.when(j == 0)
    def _():
        xn_ref[FFN_HALO:, :] = _rms(h_ref[...], g_ref[...]).astype(BF16)
        prev = _rms(hp_ref[...], g_ref[...])
        seq_start = (i * tm) % seq == 0
        xn_ref[:FFN_HALO, :] = jnp.where(seq_start, 0.0, prev).astype(BF16)
        acc_ref[...] = jnp.zeros_like(acc_ref)

    xn = xn_ref[...]
    ug_ref[...] = jnp.dot(xn, wg_ref[...], preferred_element_type=F32)
    uv_ref[...] = jnp.dot(xn, wv_ref[...], preferred_element_type=F32)

    def conv(u_ref, cw_ref, cb_ref):
        y = cb_ref[...]
        for t in range(CONV_WIDTH):
            off = FFN_HALO - (CONV_WIDTH - 1) + t
            y = y + cw_ref[t:t + 1, :] * u_ref[off:off + tm, :]
        return y

    yg = conv(ug_ref, cwg_ref, cbg_ref)
    yv = conv(uv_ref, cwv_ref, cbv_ref)
    a = (jax.nn.silu(yg) * yv).astype(BF16)
    acc_ref[...] += jnp.dot(a, wd_ref[...], preferred_element_type=F32)

    @pl.when(j == pl.num_programs(1) - 1)
    def _():
        o_ref[...] = h_ref[...] + acc_ref[...]


def _ffn(h, seq, gain, w_up, conv_w, conv_b, w_down, *, tm=512, tf=512):
    t, d = h.shape
    dff = w_down.shape[0]
    tm = min(tm, seq)
    assert seq % tm == 0 and dff % tf == 0 and tm % FFN_HALO == 0
    nf = dff // tf
    hb = tm // FFN_HALO
    conv_b2 = conv_b.reshape(1, 2 * dff)
    return pl.pallas_call(
        functools.partial(_ffn_body, tm=tm, seq=seq),
        grid=(t // tm, nf),
        in_specs=[
            pl.BlockSpec((tm, d), lambda i, j: (i, 0)),
            pl.BlockSpec((FFN_HALO, d), lambda i, j: (jnp.maximum(i * hb - 1, 0), 0)),
            pl.BlockSpec((1, d), lambda i, j: (0, 0)),
            pl.BlockSpec((d, tf), lambda i, j: (0, j)),
            pl.BlockSpec((d, tf), lambda i, j: (0, nf + j)),
            pl.BlockSpec((CONV_WIDTH, tf), lambda i, j: (0, j)),
            pl.BlockSpec((CONV_WIDTH, tf), lambda i, j: (0, nf + j)),
            pl.BlockSpec((1, tf), lambda i, j: (0, j)),
            pl.BlockSpec((1, tf), lambda i, j: (0, nf + j)),
            pl.BlockSpec((tf, d), lambda i, j: (j, 0)),
        ],
        out_specs=pl.BlockSpec((tm, d), lambda i, j: (i, 0)),
        out_shape=jax.ShapeDtypeStruct((t, d), F32),
        scratch_shapes=[
            pltpu.VMEM((tm + FFN_HALO, d), BF16),
            pltpu.VMEM((tm + FFN_HALO, tf), F32),
            pltpu.VMEM((tm + FFN_HALO, tf), F32),
            pltpu.VMEM((tm, d), F32),
        ],
        compiler_params=_params("arbitrary", "arbitrary"),
        name="ffn",
    )(h, h, gain.reshape(1, d), w_up, w_up, conv_w, conv_w, conv_b2, conv_b2, w_down)


def _rope_cos_sin(positions, rot):
    inv = ROPE_THETA ** (-jnp.arange(0, rot, 2, dtype=F32) / rot)
    ang = positions.astype(F32)[..., None] * inv
    return jnp.cos(ang), jnp.sin(ang)


def _rope_lane_tables(positions, rot):
    c, s = _rope_cos_sin(positions, rot)
    half = rot // 2
    shp = c.shape[:-1]
    cf = jnp.concatenate([c, c, jnp.ones(shp + (LANES - rot,), F32)], -1)
    sa = jnp.concatenate([-s, jnp.zeros(shp + (LANES - half,), F32)], -1)
    sb = jnp.concatenate([jnp.zeros(shp + (half,), F32), s, jnp.zeros(shp + (LANES - rot,), F32)], -1)
    return jnp.concatenate([cf, sa, sb], -1)


def _rope_glue(x, cos, sin):
    half = cos.shape[-1]
    x1, x2 = x[..., :half], x[..., half:2 * half]
    return jnp.concatenate([x1 * cos - x2 * sin, x1 * sin + x2 * cos, x[..., 2 * half:]], -1)


DSA_QX = 384
DSA_QPAD = MXU_DIM
DSA_HEAD_GROUP = 16
DSA_CNT_ROWS = 4 * SUBLANES


def _dsa_body(q_ref, qi_ref, wi_ref, tab_ref, kidx_ref, kext_ref, ct_ref, wk_ref, wuv_ref, tri_ref, o_ref,
              qx_ref, qis_ref, sc_ref, m_ref, acc_ref, *, kc, n_keep):
    i = pl.program_id(1)
    q0 = i * Q_BLOCK
    nc = (q0 + Q_BLOCK + kc - 1) // kc
    scale = DSA_QK_DIM ** -0.5 * LOG2E
    tab = tab_ref[0]
    cf, sa, sb = tab[:, 0:LANES], tab[:, LANES:2 * LANES], tab[:, 2 * LANES:3 * LANES]
    cfi, sai, sbi = tab[:, 3 * LANES:4 * LANES], tab[:, 4 * LANES:5 * LANES], tab[:, 5 * LANES:6 * LANES]

    for h in range(DSA_HEADS):
        xh = q_ref[0, :, h * DSA_QPAD:(h + 1) * DSA_QPAD].astype(F32)
        r0 = _rope_first_vreg(xh[:, :LANES], cf, sa, sb, DSA_ROPE // 2)
        qh = jnp.concatenate([r0, xh[:, LANES:]], axis=1).astype(BF16)
        qx = jnp.dot(qh, wk_ref[h], preferred_element_type=F32) * scale
        qx_ref[h * Q_BLOCK:(h + 1) * Q_BLOCK, :] = qx.astype(BF16)
    for h in range(IDX_HEADS):
        xi = qi_ref[0, :, h * IDX_DIM:(h + 1) * IDX_DIM].astype(F32)
        qis_ref[h * Q_BLOCK:(h + 1) * Q_BLOCK, :] = _rope_first_vreg(xi, cfi, sai, sbi, IDX_ROPE // 2).astype(BF16)

    wi = wi_ref[0, 0]
    key_i = lax.broadcasted_iota(jnp.int32, (kc, Q_BLOCK), 0)
    t_l = q0 + lax.broadcasted_iota(jnp.int32, (kc, Q_BLOCK), 1)

    def score_chunk(c, carry):
        lo, hi = carry
        k0 = pl.multiple_of(c * kc, kc)
        d = _nt(kidx_ref[0, pl.ds(k0, kc), :], qis_ref[...])
        d = jnp.maximum(d, 0.0)
        sc = d[:, 0:Q_BLOCK] * wi[0:1, :]
        for h in range(1, IDX_HEADS):
            sc = sc + d[:, h * Q_BLOCK:(h + 1) * Q_BLOCK] * wi[h:h + 1, :]
        sc = sc + 0.0
        causal = (k0 + key_i) <= t_l
        sc_ref[c] = jnp.where(causal, sc, -jnp.inf)
        lo = jnp.minimum(lo, jnp.where(causal, sc, jnp.inf).min(axis=0, keepdims=True))
        hi = jnp.maximum(hi, jnp.where(causal, sc, -jnp.inf).max(axis=0, keepdims=True))
        return lo, hi

    lo, hi = lax.fori_loop(0, nc, score_chunk,
                           (jnp.full((1, Q_BLOCK), jnp.inf, F32), jnp.full((1, Q_BLOCK), -jnp.inf, F32)))

    def bisect_more(carry):
        it, _, _, n_lo = carry
        return (it < BISECT_ITERS) & (jnp.max(n_lo) > n_keep)

    def bisect(carry):
        it, lo, hi, n_lo = carry

        def count_chunk_at(mid):
            def count_chunk(c, cnt):
                ge = jnp.where(sc_ref[c] >= mid, 1.0, 0.0)
                return cnt + ge.reshape(kc // DSA_CNT_ROWS, DSA_CNT_ROWS, Q_BLOCK).sum(axis=0)
            return count_chunk

        for _ in range(BISECT_UNROLL):
            mid = lo + (hi - lo) * 0.5
            cnt = lax.fori_loop(0, nc, count_chunk_at(mid), jnp.zeros((DSA_CNT_ROWS, Q_BLOCK), F32))
            cnt = cnt.sum(axis=0, keepdims=True)
            ge = cnt >= n_keep
            lo, hi, n_lo = jnp.where(ge, mid, lo), jnp.where(ge, hi, mid), jnp.where(ge, cnt, n_lo)
        return it + BISECT_UNROLL, lo, hi, n_lo

    n_causal = (q0 + 1 + lax.broadcasted_iota(jnp.int32, (1, Q_BLOCK), 1)).astype(F32)
    it_end, lo, hi, n_lo = lax.while_loop(bisect_more, bisect, (0, lo, hi, n_causal))

    @pl.when(it_end >= BISECT_ITERS)
    def _():
        def min_chunk(c, v):
            x = sc_ref[c]
            return jnp.minimum(v, jnp.where(x >= lo, x, jnp.inf).min(axis=0, keepdims=True))

        vth = lax.fori_loop(0, nc, min_chunk, jnp.full((1, Q_BLOCK), jnp.inf, F32))

        def gt_chunk(c, n):
            return n + jnp.where(sc_ref[c] > vth, 1.0, 0.0).sum(axis=0, keepdims=True)

        room = n_keep - lax.fori_loop(0, nc, gt_chunk, jnp.zeros((1, Q_BLOCK), F32))
        over = n_lo > n_keep

        def drop_chunk(c, seen):
            x = sc_ref[c]
            eq = jnp.where(x == vth, 1.0, 0.0)
            rank = seen + jnp.dot(tri_ref[...], eq.astype(BF16), preferred_element_type=F32)
            sc_ref[c] = jnp.where((eq > 0.5) & (rank > room) & over, -jnp.inf, x)
            return seen + eq.sum(axis=0, keepdims=True)

        lax.fori_loop(0, nc, drop_chunk, jnp.zeros((1, Q_BLOCK), F32))

    def bias_chunk(c, _):
        sc_ref[c] = jnp.where(sc_ref[c] >= lo, 0.0, MASK_NEG)
        return 0

    lax.fori_loop(0, nc, bias_chunk, 0)

    m_ref[...] = jnp.full_like(m_ref, MASK_NEG)
    acc_ref[...] = jnp.zeros_like(acc_ref)
    gl = DSA_HEAD_GROUP * Q_BLOCK

    def attn_chunk(c, _):
        k0 = pl.multiple_of(c * kc, kc)
        kx = kext_ref[0, pl.ds(k0, kc), :]
        ct = ct_ref[0, c]
        bias = jnp.tile(sc_ref[c], (1, DSA_HEAD_GROUP))
        n_slab = DSA_HEADS // DSA_HEAD_GROUP
        s = _nt(kx, qx_ref[0:gl, :]) + bias
        for g in range(n_slab):
            s_next = _nt(kx, qx_ref[(g + 1) * gl:(g + 2) * gl, :]) + bias if g + 1 < n_slab else None
            _online_softmax_step(s, ct, m_ref, acc_ref, slice(g * gl, (g + 1) * gl))
            s = s_next
        return 0

    lax.fori_loop(0, nc, attn_chunk, 0)

    for h in range(DSA_HEADS):
        cols = slice(h * Q_BLOCK, (h + 1) * Q_BLOCK)
        o_lat = (acc_ref[:DSA_KV_RANK, cols] / acc_ref[DSA_KV_RANK:DSA_KV_RANK + 1, cols]).astype(BF16)
        o_ref[0, :, h * DSA_V_DIM:(h + 1) * DSA_V_DIM] = _tn(o_lat, wuv_ref[h]).astype(o_ref.dtype)


def _dsa_attention(qall, wi_l, tabs, kidx, kext, ct, wk, wuv, *, kc):
    b, s, _ = qall.shape
    assert s % kc == 0 and kc % Q_BLOCK == 0
    n_keep = min(DSA_TOPK_MAX, s // 4)
    nq = DSA_HEADS * DSA_QPAD
    ni = IDX_HEADS * IDX_DIM
    assert nq % ni == 0
    nrow = DSA_HEADS * Q_BLOCK
    return pl.pallas_call(
        functools.partial(_dsa_body, kc=kc, n_keep=float(n_keep)),
        grid=(b, s // Q_BLOCK),
        in_specs=[
            pl.BlockSpec((1, Q_BLOCK, nq), lambda bb, i: (bb, i, 0)),
            pl.BlockSpec((1, Q_BLOCK, ni), lambda bb, i: (bb, i, nq // ni)),
            pl.BlockSpec((1, 1, IDX_HEADS, Q_BLOCK), lambda bb, i: (bb, i, 0, 0)),
            pl.BlockSpec((1, Q_BLOCK, 6 * LANES), lambda bb, i: (bb, i, 0)),
            pl.BlockSpec((1, s, IDX_DIM), lambda bb, i: (bb, 0, 0)),
            pl.BlockSpec((1, s, DSA_QX), lambda bb, i: (bb, 0, 0)),
            pl.BlockSpec((1, s // kc, DSA_KV_RANK + SUM_ROWS, kc), lambda bb, i: (bb, 0, 0, 0)),
            pl.BlockSpec((DSA_HEADS, DSA_QPAD, DSA_QX), lambda bb, i: (0, 0, 0)),
            pl.BlockSpec((DSA_HEADS, DSA_KV_RANK, DSA_V_DIM), lambda bb, i: (0, 0, 0)),
            pl.BlockSpec((kc, kc), lambda bb, i: (0, 0)),
        ],
        out_specs=pl.BlockSpec((1, Q_BLOCK, DSA_HEADS * DSA_V_DIM), lambda bb, i: (bb, i, 0)),
        out_shape=jax.ShapeDtypeStruct((b, s, DSA_HEADS * DSA_V_DIM), BF16),
        scratch_shapes=[
            pltpu.VMEM((nrow, DSA_QX), BF16),
            pltpu.VMEM((IDX_HEADS * Q_BLOCK, IDX_DIM), BF16),
            pltpu.VMEM((s // kc, kc, Q_BLOCK), F32),
            pltpu.VMEM((1, nrow), F32),
            pltpu.VMEM((DSA_KV_RANK + SUM_ROWS, nrow), F32),
        ],
        compiler_params=_params("arbitrary", "arbitrary"),
        name="dsa_attn",
    )(qall, qall, wi_l, tabs, kidx, kext, ct, wk, wuv, jnp.tril(jnp.ones((kc, kc), BF16)))


DSA_PROJ_W = (DSA_Q_RANK, DSA_KV_RANK, LANES, LANES, LANES)


def _dsa_kprep_body(ckv_ref, kpe_ref, kid_ref, wid_ref, tab_ref, kvn_ref, lng_ref, lnb_ref,
                    kext_ref, kidx_ref, ct_ref, wi_ref, *, tm):
    tab = tab_ref[...]
    cf, sa, sb = tab[:, 0:LANES], tab[:, LANES:2 * LANES], tab[:, 2 * LANES:3 * LANES]
    cfi, sai, sbi = tab[:, 3 * LANES:4 * LANES], tab[:, 4 * LANES:5 * LANES], tab[:, 5 * LANES:6 * LANES]
    c_n = _rms(ckv_ref[...], kvn_ref[...])
    kext_ref[:, :DSA_KV_RANK] = c_n.astype(BF16)
    kext_ref[:, DSA_KV_RANK:] = _rope_first_vreg(kpe_ref[...], cf, sa, sb, DSA_ROPE // 2).astype(BF16)
    ct_ref[0, 0, :DSA_KV_RANK, :] = c_n.T.astype(BF16)
    ct_ref[0, 0, DSA_KV_RANK:, :] = jnp.ones((SUM_ROWS, tm), BF16)
    x = kid_ref[...]
    mu = jnp.mean(x, axis=-1, keepdims=True)
    var = jnp.mean(jnp.square(x - mu), axis=-1, keepdims=True)
    k_ln = (x - mu) * lax.rsqrt(var + NORM_EPS) * lng_ref[...] + lnb_ref[...]
    kidx_ref[...] = _rope_first_vreg(k_ln, cfi, sai, sbi, IDX_ROPE // 2).astype(BF16)
    w = wid_ref[...] * (IDX_HEADS ** -0.5 * IDX_DIM ** -0.5)
    for r in range(tm // Q_BLOCK):
        wi_ref[0, r] = w[r * Q_BLOCK:(r + 1) * Q_BLOCK, :].T[:IDX_HEADS, :]


def _dsa_kprep(proj, tabs, kv_norm, ln_g, ln_b, bsz, seq, kc):
    t = proj.shape[0]
    tm = kc
    nt = seq // tm
    col = [int(c) for c in np.cumsum((0,) + DSA_PROJ_W)]
    assert all(c % w == 0 for c, w in zip(col[1:-1], DSA_PROJ_W[1:]))

    def row(v):
        return v.reshape(1, -1)

    return pl.pallas_call(
        functools.partial(_dsa_kprep_body, tm=tm),
        grid=(t // tm,),
        in_specs=[
            pl.BlockSpec((tm, DSA_KV_RANK), lambda i: (i, col[1] // DSA_KV_RANK)),
            pl.BlockSpec((tm, LANES), lambda i: (i, col[2] // LANES)),
            pl.BlockSpec((tm, LANES), lambda i: (i, col[3] // LANES)),
            pl.BlockSpec((tm, LANES), lambda i: (i, col[4] // LANES)),
            pl.BlockSpec((tm, 6 * LANES), lambda i: (i, 0)),
            pl.BlockSpec((1, DSA_KV_RANK), lambda i: (0, 0)),
            pl.BlockSpec((1, IDX_DIM), lambda i: (0, 0)),
            pl.BlockSpec((1, IDX_DIM), lambda i: (0, 0)),
        ],
        out_specs=[
            pl.BlockSpec((tm, DSA_QX), lambda i: (i, 0)),
            pl.BlockSpec((tm, IDX_DIM), lambda i: (i, 0)),
            pl.BlockSpec((1, 1, DSA_KV_RANK + SUM_ROWS, tm), lambda i: (i // nt, i % nt, 0, 0)),
            pl.BlockSpec((1, tm // Q_BLOCK, IDX_HEADS, Q_BLOCK), lambda i: (i // nt, i % nt, 0, 0)),
        ],
        out_shape=[
            jax.ShapeDtypeStruct((t, DSA_QX), BF16),
            jax.ShapeDtypeStruct((t, IDX_DIM), BF16),
            jax.ShapeDtypeStruct((bsz, nt, DSA_KV_RANK + SUM_ROWS, tm), BF16),
            jax.ShapeDtypeStruct((bsz, seq // Q_BLOCK, IDX_HEADS, Q_BLOCK), F32),
        ],
        compiler_params=_params("arbitrary"),
        name="dsa_kprep",
    )(proj, proj, proj, proj, tabs, row(kv_norm), row(ln_g), row(ln_b))


def _dsa_mixer(h2, bsz, seq, positions, attn_norm, w_in, q_norm, kv_norm, idx_ln_g, idx_ln_b,
               w_qup, w_uk, w_uv, w_out, *, kc=512):
    t = h2.shape[0]
    kc = min(kc, seq)
    sizes = (DSA_Q_RANK, DSA_KV_RANK, DSA_ROPE, IDX_DIM, IDX_HEADS)
    cuts = np.cumsum((0,) + sizes)
    w_in_p = jnp.concatenate(
        [jnp.pad(w_in[:, int(a):int(b)], ((0, 0), (0, wp - (int(b) - int(a)))))
         for a, b, wp in zip(cuts[:-1], cuts[1:], DSA_PROJ_W)], axis=1).astype(BF16)
    proj = _mm(h2, w_in_p, gain=attn_norm, tm=1024, tn=w_in_p.shape[1])

    nq = DSA_HEADS * DSA_QK_DIM
    w_q = w_qup[:, :nq].reshape(DSA_Q_RANK, DSA_HEADS, DSA_QK_DIM)
    w_q = jnp.pad(w_q, ((0, 0), (0, 0), (0, DSA_QPAD - DSA_QK_DIM))).reshape(DSA_Q_RANK, DSA_HEADS * DSA_QPAD)
    w_qp = jnp.concatenate([w_q, w_qup[:, nq:]], axis=1).astype(BF16)
    qall = _mm(proj, w_qp, gain=q_norm, out_dtype=BF16, tn=w_qp.shape[1])

    pos = positions.reshape(t)
    tabs = jnp.concatenate([_rope_lane_tables(pos, DSA_ROPE), _rope_lane_tables(pos, IDX_ROPE)], -1)
    kext, kidx, ct, wi = _dsa_kprep(proj, tabs, kv_norm, idx_ln_g, idx_ln_b, bsz, seq, kc)

    wk = jnp.zeros((DSA_HEADS, DSA_QPAD, DSA_QX), F32)
    wk = wk.at[:, DSA_ROPE:DSA_QK_DIM, :DSA_KV_RANK].set(jnp.swapaxes(w_uk, 1, 2))
    eye = jnp.eye(DSA_ROPE, dtype=F32)
    wk = wk.at[:, :DSA_ROPE, DSA_KV_RANK:DSA_KV_RANK + DSA_ROPE].set(jnp.broadcast_to(eye, (DSA_HEADS,) + eye.shape))

    o = _dsa_attention(
        qall.reshape(bsz, seq, -1), wi, tabs.reshape(bsz, seq, -1),
        kidx.reshape(bsz, seq, IDX_DIM), kext.reshape(bsz, seq, DSA_QX), ct,
        wk.astype(BF16), w_uv.astype(BF16), kc=kc)
    return _mm(o.reshape(t, -1), w_out.astype(BF16), res=h2, tm=1024)


FOX_KX = MXU_DIM
FOX_HEAD_PAIR = 2


def _fox_body(q_ref, k_ref, v_ref, c3_ref, e_ref, o_ref, qx_ref, kx_ref, vt_ref, m_ref, acc_ref, *, tq, kc):
    i = pl.program_id(2)
    q0 = i * tq
    n_full = q0 // kc
    n_diag = tq // kc

    @pl.when(i == 0)
    def _():
        for a in range(FOX_HEAD_PAIR):
            kx_ref[a, :, :FOX_DIM] = k_ref[0, :, a * FOX_DIM:(a + 1) * FOX_DIM]
            kx_ref[a, :, FOX_DIM:] = _tn(c3_ref[0, a], e_ref[...]).astype(BF16)
            for c in range(vt_ref.shape[1]):
                vc = v_ref[0, c * kc:(c + 1) * kc, a * FOX_DIM:(a + 1) * FOX_DIM]
                vt_ref[a, c, :FOX_DIM, :] = vc.astype(F32).T.astype(BF16)
                vt_ref[a, c, FOX_DIM:, :] = jnp.ones((SUM_ROWS, kc), BF16)

    lane = lax.broadcasted_iota(jnp.int32, (tq, FOX_KX - FOX_DIM), 1)
    for a in range(FOX_HEAD_PAIR):
        qx_ref[a, :, :FOX_DIM] = q_ref[0, :, a * FOX_DIM:(a + 1) * FOX_DIM]
        qx_ref[a, :, FOX_DIM:] = jnp.where(lane < 3, 1.0, 0.0).astype(BF16)
    m_ref[...] = jnp.full_like(m_ref, MASK_NEG)
    acc_ref[...] = jnp.zeros_like(acc_ref)
    key_i = lax.broadcasted_iota(jnp.int32, (kc, tq), 0)
    t_l = q0 + lax.broadcasted_iota(jnp.int32, (kc, tq), 1)

    def step(c, masked):
        k0 = pl.multiple_of(c * kc, kc)

        def logits(a):
            s = _nt(kx_ref[a, pl.ds(k0, kc), :], qx_ref[a])
            return jnp.where((k0 + key_i) <= t_l, s, MASK_NEG) if masked else s

        s = logits(0)
        for a in range(FOX_HEAD_PAIR):
            s_next = logits(a + 1) if a + 1 < FOX_HEAD_PAIR else None
            _online_softmax_step(s, vt_ref[a, c], m_ref, acc_ref, slice(a * tq, (a + 1) * tq))
            s = s_next

    def full_chunk(c, _):
        step(c, False)
        return 0

    def diag_chunk(c, _):
        step(c, True)
        return 0

    lax.fori_loop(0, n_full, full_chunk, 0)
    lax.fori_loop(n_full, n_full + n_diag, diag_chunk, 0)
    for a in range(FOX_HEAD_PAIR):
        cols = slice(a * tq, (a + 1) * tq)
        o_a = acc_ref[:FOX_DIM, cols] / acc_ref[FOX_DIM:FOX_DIM + 1, cols]
        o_ref[0, :, a * FOX_DIM:(a + 1) * FOX_DIM] = o_a.T.astype(o_ref.dtype)


def _fox_attention(qkv, c3, *, tq, kc):
    b, s, _ = qkv.shape
    hp = FOX_HEAD_PAIR
    npair = FOX_HEADS // hp
    assert s % tq == 0 and tq % kc == 0 and FOX_HEADS % hp == 0
    e = jnp.eye(SUBLANES, FOX_KX - FOX_DIM, dtype=BF16)
    return pl.pallas_call(
        functools.partial(_fox_body, tq=tq, kc=kc),
        grid=(b, npair, s // tq),
        in_specs=[
            pl.BlockSpec((1, tq, hp * FOX_DIM), lambda bb, h, i: (bb, i, h)),
            pl.BlockSpec((1, s, hp * FOX_DIM), lambda bb, h, i: (bb, 0, npair + h)),
            pl.BlockSpec((1, s, hp * FOX_DIM), lambda bb, h, i: (bb, 0, 2 * npair + h)),
            pl.BlockSpec((1, hp, SUBLANES, s), lambda bb, h, i: (bb, h, 0, 0)),
            pl.BlockSpec((SUBLANES, FOX_KX - FOX_DIM), lambda bb, h, i: (0, 0)),
        ],
        out_specs=pl.BlockSpec((1, tq, hp * FOX_DIM), lambda bb, h, i: (bb, i, h)),
        out_shape=jax.ShapeDtypeStruct((b, s, FOX_HEADS * FOX_DIM), BF16),
        scratch_shapes=[pltpu.VMEM((hp, tq, FOX_KX), BF16), pltpu.VMEM((hp, s, FOX_KX), BF16),
                        pltpu.VMEM((hp, s // kc, FOX_DIM + SUM_ROWS, kc), BF16),
                        pltpu.VMEM((1, hp * tq), F32), pltpu.VMEM((FOX_DIM + SUM_ROWS, hp * tq), F32)],
        compiler_params=_params("arbitrary", "arbitrary", "arbitrary"),
        name="fox_attn",
    )(qkv, qkv, qkv, c3, e)


def _fox_mixer(h2, bsz, seq, attn_norm, w_in, b_f, w_out, *, tq=512, kc=512):
    t = h2.shape[0]
    hd = FOX_HEADS * FOX_DIM
    tq = min(tq, seq)
    kc = min(kc, tq)
    scale = FOX_DIM ** -0.5 * LOG2E
    w_qkv = jnp.concatenate([w_in[:, :hd] * scale, w_in[:, hd:3 * hd]], axis=1).astype(BF16)
    qkv = _mm(h2, w_qkv, gain=attn_norm, out_dtype=BF16, tm=1024, tn=1024)
    w_f = jnp.pad(w_in[:, 3 * hd:], ((0, 0), (0, LANES - FOX_HEADS))).astype(BF16)
    f_logit = _mm(h2, w_f, gain=attn_norm)[:, :FOX_HEADS]
    log_f = jax.nn.log_sigmoid(f_logit + b_f)
    cum = jnp.cumsum(log_f.reshape(bsz, seq, FOX_HEADS), axis=1)
    c3 = jnp.stack(_split3_bits(-cum * LOG2E), axis=0).transpose(1, 3, 0, 2)
    c3 = jnp.pad(c3, ((0, 0), (0, 0), (0, SUBLANES - 3), (0, 0)))
    o = _fox_attention(qkv.reshape(bsz, seq, 3 * hd), c3, tq=tq, kc=kc)
    return _mm(o.reshape(t, hd), w_out.astype(BF16), res=h2, tm=1024)


NSA_HEAD_GROUP = 4


def _cmp_body(x_ref, pe_ref, w1_ref, b1_ref, w2_ref, o_ref, *, half):
    x = x_ref[0, 0]
    w_lo = w1_ref[:half, :]
    w_hi = w1_ref[half:, :]
    a = jnp.dot(x, w_lo, preferred_element_type=F32)
    bnext = jnp.dot(x, w_hi, preferred_element_type=F32)
    n = x.shape[0]
    bnext = pltpu.roll(bnext, n - 1, 0)
    pe = pe_ref[...]
    pe_b = (jnp.dot(pe[:, :half], w_lo, preferred_element_type=F32)
            + jnp.dot(pe[:, half:], w_hi, preferred_element_type=F32))[0:1, :]
    hid = jax.nn.gelu(a + bnext + pe_b + b1_ref[...])
    o_ref[0, 0] = jnp.dot(hid.astype(BF16), w2_ref[...], preferred_element_type=F32).astype(o_ref.dtype)


def _compress(x, pe, w1, b1, w2):
    b, g, n, kd = x.shape
    dout = w2.shape[1]
    pe8 = jnp.broadcast_to(pe.reshape(1, 2 * kd), (8, 2 * kd)).astype(BF16)
    return pl.pallas_call(
        functools.partial(_cmp_body, half=kd),
        grid=(b, g),
        in_specs=[
            pl.BlockSpec((1, 1, n, kd), lambda bb, gg: (bb, gg, 0, 0)),
            pl.BlockSpec((8, 2 * kd), lambda bb, gg: (0, 0)),
            pl.BlockSpec((2 * kd, CMP_HIDDEN), lambda bb, gg: (0, 0)),
            pl.BlockSpec((1, CMP_HIDDEN), lambda bb, gg: (0, 0)),
            pl.BlockSpec((CMP_HIDDEN, dout), lambda bb, gg: (0, 0)),
        ],
        out_specs=pl.BlockSpec((1, 1, n, dout), lambda bb, gg: (bb, gg, 0, 0)),
        out_shape=jax.ShapeDtypeStruct((b, g, n, dout), BF16),
        compiler_params=_params("arbitrary", "arbitrary"),
        name="nsa_compress",
    )(x, pe8, w1.astype(BF16), b1.reshape(1, CMP_HIDDEN), w2.astype(BF16))


def _nsa_body(q_ref, tab_ref, gate_ref, kc_ref, vct_ref, ks_ref, vs_ref, kw_ref, vw_ref, ov_ref, exp_ref,
              o_ref, qs_ref, vst_ref, vwt_ref, m_ref, acc_ref, *, kc, n_cmp, n_slc, n_sel):
    i = pl.program_id(2)

    @pl.when(i == 0)
    def _():
        for c in range(vst_ref.shape[0]):
            vst_ref[c, :NSA_V_DIM, :] = vs_ref[0, c * kc:(c + 1) * kc, :].T.astype(BF16)
            vst_ref[c, NSA_V_DIM:, :] = jnp.ones((SUM_ROWS, kc), BF16)
        npad = WINDOW // Q_BLOCK
        for c in range(vwt_ref.shape[0]):
            if c < npad:
                vwt_ref[c] = jnp.zeros((NSA_V_DIM, Q_BLOCK), BF16)
            else:
                vwt_ref[c] = vw_ref[0, (c - npad) * Q_BLOCK:(c - npad + 1) * Q_BLOCK, :].T.astype(BF16)

    q0 = i * Q_BLOCK
    nc = (q0 + Q_BLOCK + kc - 1) // kc
    jh = NSA_HPG
    scale = NSA_QK_DIM ** -0.5 * LOG2E
    tab = tab_ref[0]
    cf, sa, sb = tab[:, 0:LANES], tab[:, LANES:2 * LANES], tab[:, 2 * LANES:3 * LANES]

    for j in range(jh):
        xj = q_ref[0, :, j * NSA_QK_DIM:(j + 1) * NSA_QK_DIM].astype(F32)
        r0 = _rope_first_vreg(xj[:, :LANES], cf, sa, sb, NSA_ROPE // 2)
        qs_ref[j * Q_BLOCK:(j + 1) * Q_BLOCK, :] = (jnp.concatenate([r0, xj[:, LANES:]], axis=1) * scale).astype(BF16)
    qs = qs_ref[...]

    n_id = lax.broadcasted_iota(jnp.int32, (n_cmp, Q_BLOCK), 0)
    t_c = q0 + lax.broadcasted_iota(jnp.int32, (n_cmp, Q_BLOCK), 1)
    cbias = jnp.where((n_id * CMP_STRIDE + (CMP_BLOCK - 1)) <= t_c, 0.0, MASK_NEG)
    pc = _softmax_cols(_nt(kc_ref[0, 0], qs) + jnp.tile(cbias, (1, jh)))
    o_c = jnp.dot(vct_ref[0, 0], pc.astype(BF16), preferred_element_type=F32)

    pcs = pc[:, 0:Q_BLOCK]
    for j in range(1, jh):
        pcs = pcs + pc[:, j * Q_BLOCK:(j + 1) * Q_BLOCK]
    ov = ov_ref[...]
    imp = sum(jnp.dot(ov, term, preferred_element_type=F32) for term in _split3(pcs))

    blk = lax.broadcasted_iota(jnp.int32, (n_slc, Q_BLOCK), 0)
    t_b = q0 + lax.broadcasted_iota(jnp.int32, (n_slc, Q_BLOCK), 1)
    cur = lax.shift_right_logical(t_b, int(np.log2(SLC_BLOCK)))
    causal_blk = blk * SLC_BLOCK <= t_b
    forced = (blk == 0) | (blk == cur) | (blk == cur - 1)
    val = jnp.where(causal_blk, jnp.where(forced, jnp.inf, imp), -jnp.inf)
    rank = jnp.zeros((n_slc, Q_BLOCK), F32)
    for mp in range(n_slc):
        vrow = val[mp:mp + 1, :]
        before = (vrow > val) | ((vrow == val) & (blk > mp))
        rank = rank + jnp.where(before, 1.0, 0.0)
    sel = jnp.where((rank < n_sel) & causal_blk, 1.0, 0.0)
    sel_p = jnp.concatenate([sel, jnp.zeros((LANES - n_slc, Q_BLOCK), F32)], axis=0).astype(BF16)

    m_ref[...] = jnp.full_like(m_ref, MASK_NEG)
    acc_ref[...] = jnp.zeros_like(acc_ref)
    key_i = lax.broadcasted_iota(jnp.int32, (kc, Q_BLOCK), 0)
    t_k = q0 + lax.broadcasted_iota(jnp.int32, (kc, Q_BLOCK), 1)
    gl = NSA_HEAD_GROUP * Q_BLOCK

    def sel_chunk(c, _):
        k0 = pl.multiple_of(c * kc, kc)
        hit = jnp.dot(exp_ref[pl.ds(k0, kc), :], sel_p, preferred_element_type=F32)
        bias = jnp.where((hit > 0.5) & ((k0 + key_i) <= t_k), 0.0, MASK_NEG)
        bias = jnp.tile(bias, (1, NSA_HEAD_GROUP))
        ksc = ks_ref[0, 0, pl.ds(k0, kc), :]
        vt = vst_ref[c]
        n_slab = jh // NSA_HEAD_GROUP
        s = _nt(ksc, qs_ref[0:gl, :]) + bias
        for hg in range(n_slab):
            s_next = _nt(ksc, qs_ref[(hg + 1) * gl:(hg + 2) * gl, :]) + bias if hg + 1 < n_slab else None
            _online_softmax_step(s, vt, m_ref, acc_ref, slice(hg * gl, (hg + 1) * gl))
            s = s_next
        return 0

    lax.fori_loop(0, nc, sel_chunk, 0)
    o_s = acc_ref[:NSA_V_DIM, :] / acc_ref[NSA_V_DIM:NSA_V_DIM + 1, :]

    wl = WINDOW + Q_BLOCK
    kstart = pl.multiple_of(q0, Q_BLOCK)
    s_pos = q0 - WINDOW + lax.broadcasted_iota(jnp.int32, (wl, Q_BLOCK), 0)
    t_w = q0 + lax.broadcasted_iota(jnp.int32, (wl, Q_BLOCK), 1)
    wbias = jnp.where((s_pos >= 0) & (s_pos <= t_w) & (s_pos > t_w - WINDOW), 0.0, MASK_NEG)
    pw = _softmax_cols(_nt(kw_ref[0, 0, pl.ds(kstart, wl), :], qs) + jnp.tile(wbias, (1, jh))).astype(BF16)
    o_w = jnp.dot(vwt_ref[i], pw[0:Q_BLOCK, :], preferred_element_type=F32)
    for c in range(1, wl // Q_BLOCK):
        o_w = o_w + jnp.dot(vwt_ref[i + c], pw[c * Q_BLOCK:(c + 1) * Q_BLOCK, :],
                            preferred_element_type=F32)

    g = gate_ref[0, 0]
    for j in range(jh):
        cj = slice(j * Q_BLOCK, (j + 1) * Q_BLOCK)
        out = (g[j:j + 1, :] * o_c[:, cj] + g[jh + j:jh + j + 1, :] * o_s[:, cj]
               + g[2 * jh + j:2 * jh + j + 1, :] * o_w[:, cj])
        o_ref[0, :, j * NSA_V_DIM:(j + 1) * NSA_V_DIM] = out.T.astype(o_ref.dtype)


def _nsa_attention(q, tabs, gates, kcmp, vcmp_t, ks, kw, rest, vs_col, vw_col, *, kc):
    b, s, _ = q.shape
    g = NSA_GROUPS
    assert s % kc == 0 and kc % SLC_BLOCK == 0
    n_cmp = kcmp.shape[2]
    n_slc = s // SLC_BLOCK
    assert n_slc <= LANES
    n_sel = min(SLC_TOPK, n_slc)
    cmp_start = np.arange(n_cmp) * CMP_STRIDE
    slc_start = np.arange(n_slc) * SLC_BLOCK
    ov = ((cmp_start[None, :] < slc_start[:, None] + SLC_BLOCK)
          & (cmp_start[None, :] + CMP_BLOCK > slc_start[:, None])).astype(np.float32)
    expand = (np.arange(s)[:, None] // SLC_BLOCK == np.arange(LANES)[None, :]).astype(np.float32)
    qw = NSA_HPG * NSA_QK_DIM
    ow = NSA_HPG * NSA_V_DIM
    nrow = NSA_HPG * Q_BLOCK
    wl = s + WINDOW
    return pl.pallas_call(
        functools.partial(_nsa_body, kc=kc, n_cmp=n_cmp, n_slc=n_slc, n_sel=float(n_sel)),
        grid=(b, g, s // Q_BLOCK),
        in_specs=[
            pl.BlockSpec((1, Q_BLOCK, qw), lambda bb, gg, i: (bb, i, gg)),
            pl.BlockSpec((1, Q_BLOCK, 3 * LANES), lambda bb, gg, i: (bb, i, 0)),
            pl.BlockSpec((1, 1, 3 * NSA_HPG, Q_BLOCK), lambda bb, gg, i: (bb, gg, 0, i)),
            pl.BlockSpec((1, 1, n_cmp, NSA_QK_DIM), lambda bb, gg, i: (bb, gg, 0, 0)),
            pl.BlockSpec((1, 1, NSA_V_DIM, n_cmp), lambda bb, gg, i: (bb, gg, 0, 0)),
            pl.BlockSpec((1, 1, s, NSA_QK_DIM), lambda bb, gg, i: (bb, gg, 0, 0)),
            pl.BlockSpec((1, s, NSA_V_DIM), lambda bb, gg, i: (bb, 0, vs_col + gg)),
            pl.BlockSpec((1, 1, wl, NSA_QK_DIM), lambda bb, gg, i: (bb, gg, 0, 0)),
            pl.BlockSpec((1, s, NSA_V_DIM), lambda bb, gg, i: (bb, 0, vw_col + gg)),
            pl.BlockSpec((n_slc, n_cmp), lambda bb, gg, i: (0, 0)),
            pl.BlockSpec((s, LANES), lambda bb, gg, i: (0, 0)),
        ],
        out_specs=pl.BlockSpec((1, Q_BLOCK, ow), lambda bb, gg, i: (bb, i, gg)),
        out_shape=jax.ShapeDtypeStruct((b, s, NSA_HEADS * NSA_V_DIM), BF16),
        scratch_shapes=[
            pltpu.VMEM((nrow, NSA_QK_DIM), BF16),
            pltpu.VMEM((s // kc, NSA_V_DIM + SUM_ROWS, kc), BF16),
            pltpu.VMEM((wl // Q_BLOCK, NSA_V_DIM, Q_BLOCK), BF16),
            pltpu.VMEM((1, nrow), F32),
            pltpu.VMEM((NSA_V_DIM + SUM_ROWS, nrow), F32),
        ],
        compiler_params=_params("arbitrary", "arbitrary", "arbitrary"),
        name="nsa_attn",
    )(q, tabs, gates, kcmp, vcmp_t, ks, rest, kw, rest, jnp.asarray(ov, BF16), jnp.asarray(expand, BF16))


def _nsa_mixer(h2, bsz, seq, positions, attn_norm, w_in, k_pe, k_w1, k_b1, k_w2, v_pe, v_w1, v_b1, v_w2, w_out,
               *, kc=1024):
    t = h2.shape[0]
    kc = min(kc, seq)
    g, jh, dk, dv = NSA_GROUPS, NSA_HPG, NSA_QK_DIM, NSA_V_DIM
    nq = NSA_HEADS * dk
    n_rest = w_in.shape[1] - nq
    n_rest_p = -(-n_rest // 512) * 512
    q = _mm(h2, w_in[:, :nq].astype(BF16), gain=attn_norm, out_dtype=BF16, tm=1024, tn=1024)
    w_rest = jnp.pad(w_in[:, nq:], ((0, 0), (0, n_rest_p - n_rest))).astype(BF16)
    rest = _mm(h2, w_rest, gain=attn_norm, tm=1024, tn=1024)
    offs = np.cumsum([0, NSA_KD, NSA_VD, NSA_KD, NSA_VD, NSA_KD, NSA_VD, 3 * NSA_HEADS])
    k_c, v_c, k_s, v_s, k_w, v_w, g_logit = [rest[:, int(a):int(b)] for a, b in zip(offs[:-1], offs[1:])]

    pos = positions.reshape(t)
    cos, sin = _rope_cos_sin(pos, NSA_ROPE)

    def rope_k(k):
        return _rope_glue(k.reshape(t, g, dk), cos[:, None, :], sin[:, None, :])

    def per_group(x, d):
        return x.reshape(bsz, seq, g, d).transpose(0, 2, 1, 3).astype(BF16)

    def chunks(x, d):
        x = x.reshape(bsz, seq // CMP_STRIDE, CMP_STRIDE, g, d).transpose(0, 3, 1, 2, 4)
        return x.reshape(bsz, g, seq // CMP_STRIDE, CMP_STRIDE * d).astype(BF16)

    kcmp = _compress(chunks(rope_k(k_c), dk), k_pe, k_w1, k_b1, k_w2)
    vcmp_t = jnp.swapaxes(_compress(chunks(v_c.reshape(t, g, dv), dv), v_pe, v_w1, v_b1, v_w2), -1, -2)
    ks = per_group(rope_k(k_s), dk)
    pad = ((0, 0), (0, 0), (WINDOW, 0), (0, 0))
    kw = jnp.pad(per_group(rope_k(k_w), dk), pad)
    gates = jax.nn.sigmoid(g_logit).reshape(bsz, seq, 3, g, jh).transpose(0, 3, 2, 4, 1).reshape(bsz, g, 3 * jh, seq)
    tabs = _rope_lane_tables(pos, NSA_ROPE).reshape(bsz, seq, 3 * LANES)

    assert int(offs[3]) % dv == 0 and int(offs[5]) % dv == 0
    o = _nsa_attention(q.reshape(bsz, seq, nq), tabs, gates, kcmp, vcmp_t, ks, kw, rest.reshape(bsz, seq, -1),
                       int(offs[3]) // dv, int(offs[5]) // dv, kc=kc)
    return _mm(o.reshape(t, NSA_HEADS * dv), w_out.astype(BF16), res=h2, tm=1024)


def _norm_body(x_ref, g_ref, o_ref):
    o_ref[...] = _rms(x_ref[...], g_ref[...])


def _final_norm(h2, gain, *, tm=512):
    t, d = h2.shape
    tm = min(tm, t)
    return pl.pallas_call(
        _norm_body,
        grid=(t // tm,),
        in_specs=[pl.BlockSpec((tm, d), lambda i: (i, 0)), pl.BlockSpec((1, d), lambda i: (0, 0))],
        out_specs=pl.BlockSpec((tm, d), lambda i: (i, 0)),
        out_shape=jax.ShapeDtypeStruct((t, d), F32),
        compiler_params=_params("arbitrary"),
        name="final_norm",
    )(h2, gain.reshape(1, d))


def _ffn_layer(h2, seq, ffn_norm, w_up, conv_w, conv_b, w_down):
    return _ffn(h2, seq, ffn_norm, w_up.astype(BF16), conv_w, conv_b, w_down.astype(BF16))


def kernel(x, positions, l0_attn_norm, l0_dsa_w_in, l0_dsa_q_norm, l0_dsa_kv_norm, l0_dsa_idx_ln_g, l0_dsa_idx_ln_b, l0_dsa_w_qup, l0_dsa_w_uk, l0_dsa_w_uv, l0_dsa_w_out, l0_ffn_norm, l0_ffn_up, l0_ffn_conv_w, l0_ffn_conv_b, l0_ffn_down, l1_attn_norm, l1_fox_w_in, l1_fox_b_f, l1_fox_w_out, l1_ffn_norm, l1_ffn_up, l1_ffn_conv_w, l1_ffn_conv_b, l1_ffn_down, l2_attn_norm, l2_nsa_w_in, l2_nsa_k_pe, l2_nsa_k_w1, l2_nsa_k_b1, l2_nsa_k_w2, l2_nsa_v_pe, l2_nsa_v_w1, l2_nsa_v_b1, l2_nsa_v_w2, l2_nsa_w_out, l2_ffn_norm, l2_ffn_up, l2_ffn_conv_w, l2_ffn_conv_b, l2_ffn_down, l3_attn_norm, l3_dsa_w_in, l3_dsa_q_norm, l3_dsa_kv_norm, l3_dsa_idx_ln_g, l3_dsa_idx_ln_b, l3_dsa_w_qup, l3_dsa_w_uk, l3_dsa_w_uv, l3_dsa_w_out, l3_ffn_norm, l3_ffn_up, l3_ffn_conv_w, l3_ffn_conv_b, l3_ffn_down, final_norm):
    bsz, seq, d = x.shape
    h = x.reshape(bsz * seq, d)
    h = _dsa_mixer(h, bsz, seq, positions, l0_attn_norm, l0_dsa_w_in, l0_dsa_q_norm, l0_dsa_kv_norm,
                   l0_dsa_idx_ln_g, l0_dsa_idx_ln_b, l0_dsa_w_qup, l0_dsa_w_uk, l0_dsa_w_uv, l0_dsa_w_out)
    h = _ffn_layer(h, seq, l0_ffn_norm, l0_ffn_up, l0_ffn_conv_w, l0_ffn_conv_b, l0_ffn_down)
    h = _fox_mixer(h, bsz, seq, l1_attn_norm, l1_fox_w_in, l1_fox_b_f, l1_fox_w_out)
    h = _ffn_layer(h, seq, l1_ffn_norm, l1_ffn_up, l1_ffn_conv_w, l1_ffn_conv_b, l1_ffn_down)
    h = _nsa_mixer(h, bsz, seq, positions, l2_attn_norm, l2_nsa_w_in, l2_nsa_k_pe, l2_nsa_k_w1, l2_nsa_k_b1,
                   l2_nsa_k_w2, l2_nsa_v_pe, l2_nsa_v_w1, l2_nsa_v_b1, l2_nsa_v_w2, l2_nsa_w_out)
    h = _ffn_layer(h, seq, l2_ffn_norm, l2_ffn_up, l2_ffn_conv_w, l2_ffn_conv_b, l2_ffn_down)
    h = _dsa_mixer(h, bsz, seq, positions, l3_attn_norm, l3_dsa_w_in, l3_dsa_q_norm, l3_dsa_kv_norm,
                   l3_dsa_idx_ln_g, l3_dsa_idx_ln_b, l3_dsa_w_qup, l3_dsa_w_uk, l3_dsa_w_uv, l3_dsa_w_out)
    h = _ffn_layer(h, seq, l3_ffn_norm, l3_ffn_up, l3_ffn_conv_w, l3_ffn_conv_b, l3_ffn_down)
    return _final_norm(h, final_norm).reshape(bsz, seq, d)
```

```python
import functools

import jax
import jax.numpy as jnp
import numpy as np
from jax import lax
from jax.experimental import pallas as pl
from jax.experimental.pallas import tpu as pltpu

F32 = jnp.float32
BF16 = jnp.bfloat16

ROPE_THETA = 500000.0
NORM_EPS = 1e-6
Q_BLOCK = 128

DSA_HEADS = 32
DSA_Q_RANK = 512
DSA_KV_RANK = 256
DSA_QK_DIM = 192
DSA_ROPE = 48
DSA_NOPE = DSA_QK_DIM - DSA_ROPE
DSA_V_DIM = 128
IDX_HEADS = 16
IDX_DIM = 128
IDX_ROPE = 32
DSA_TOPK_MAX = 256
DSA_IN = DSA_Q_RANK + DSA_KV_RANK + DSA_ROPE + IDX_DIM + IDX_HEADS

FOX_HEADS = 16
FOX_DIM = 128

NSA_HEADS = 48
NSA_GROUPS = 4
NSA_HPG = NSA_HEADS // NSA_GROUPS
NSA_QK_DIM = 192
NSA_ROPE = 48
NSA_V_DIM = 128
CMP_BLOCK = 32
CMP_STRIDE = 16
CMP_HIDDEN = 256
SLC_BLOCK = 64
SLC_TOPK = 16
WINDOW = 512
NSA_KD = NSA_GROUPS * NSA_QK_DIM
NSA_VD = NSA_GROUPS * NSA_V_DIM

CONV_WIDTH = 3

LANES = 128
SUBLANES = 8
BF16_SUBLANES = 16
MXU_DIM = 256
VMEM_LIMIT = 56 * 1024 * 1024

MASK_NEG = -1e30
SUM_ROWS = BF16_SUBLANES
LOG2E = 1.4426950408889634
BISECT_ITERS = 32
BISECT_UNROLL = 4


def _params(*sem):
    return pltpu.CompilerParams(dimension_semantics=sem, vmem_limit_bytes=VMEM_LIMIT)


def _nt(a, b):
    return lax.dot_general(a, b, (((1,), (1,)), ((), ())), preferred_element_type=F32)


def _tn(a, b):
    return lax.dot_general(a, b, (((0,), (0,)), ((), ())), preferred_element_type=F32)


def _rms(x, g):
    return x * lax.rsqrt(jnp.mean(x * x, axis=-1, keepdims=True) + NORM_EPS) * g


def _rope_first_vreg(x0, cf, sa, sb, half):
    return (x0 * cf + pltpu.roll(x0, LANES - half, 1) * sa + pltpu.roll(x0, half, 1) * sb)


def _split3(x):
    hi = x.astype(BF16)
    r1 = x - hi.astype(F32)
    mid = r1.astype(BF16)
    lo = (r1 - mid.astype(F32)).astype(BF16)
    return hi, mid, lo


def _split3_bits(x):
    def trunc(v):
        bits = lax.bitcast_convert_type(v, jnp.uint32) & jnp.uint32(0xFFFF0000)
        return lax.bitcast_convert_type(bits, F32)

    hi = trunc(x)
    r1 = x - hi
    mid = trunc(r1)
    lo = r1 - mid
    return hi.astype(BF16), mid.astype(BF16), lo.astype(BF16)


def _online_softmax_step(s, pv_lhs, m_ref, acc_ref, cols):
    m_prev = m_ref[:, cols]
    m_new = jnp.maximum(m_prev, s.max(axis=0, keepdims=True))
    alpha = jnp.exp2(m_prev - m_new)
    p = jnp.exp2(s - m_new).astype(BF16)
    acc_ref[:, cols] = alpha * acc_ref[:, cols] + jnp.dot(pv_lhs, p, preferred_element_type=F32)
    m_ref[:, cols] = m_new


def _softmax_cols(z):
    mx = z.max(axis=0, keepdims=True)
    e = jnp.exp2(z - mx)
    den = e.sum(axis=0, keepdims=True)
    return e * jnp.where(mx > 0.5 * MASK_NEG, 1.0 / den, 0.0)


def _mm_body(*refs, has_gain, has_res):
    x_ref, w_ref = refs[0], refs[1]
    k = 2
    g_ref = r_ref = None
    if has_gain:
        g_ref = refs[k]
        k += 1
    if has_res:
        r_ref = refs[k]
        k += 1
    o_ref = refs[k]
    if has_gain:
        xn_ref = refs[k + 1]

        @pl.when(pl.program_id(1) == 0)
        def _():
            xn_ref[...] = _rms(x_ref[...].astype(F32), g_ref[...]).astype(BF16)

        a = xn_ref[...]
    else:
        a = x_ref[...].astype(BF16)
    acc = jnp.dot(a, w_ref[...], preferred_element_type=F32)
    if has_res:
        acc = acc + r_ref[...]
    o_ref[...] = acc.astype(o_ref.dtype)


def _mm(x, w, *, gain=None, res=None, out_dtype=F32, x_col_block=0, tm=512, tn=512):
    m = x.shape[0]
    k, n = w.shape
    tm = min(tm, m)
    tn = min(tn, n)
    assert m % tm == 0 and n % tn == 0 and x.shape[1] >= (x_col_block + 1) * k
    in_specs = [pl.BlockSpec((tm, k), lambda i, j: (i, x_col_block)),
                pl.BlockSpec((k, tn), lambda i, j: (0, j))]
    args = [x, w]
    scratch = []
    if gain is not None:
        in_specs.append(pl.BlockSpec((1, k), lambda i, j: (0, 0)))
        args.append(gain.reshape(1, k).astype(F32))
        scratch.append(pltpu.VMEM((tm, k), BF16))
    if res is not None:
        in_specs.append(pl.BlockSpec((tm, tn), lambda i, j: (i, j)))
        args.append(res)
    return pl.pallas_call(
        functools.partial(_mm_body, has_gain=gain is not None, has_res=res is not None),
        grid=(m // tm, n // tn),
        in_specs=in_specs,
        out_specs=pl.BlockSpec((tm, tn), lambda i, j: (i, j)),
        out_shape=jax.ShapeDtypeStruct((m, n), out_dtype),
        scratch_shapes=scratch,
        compiler_params=_params("arbitrary", "arbitrary"),
        name="mm",
    )(*args)


FFN_HALO = BF16_SUBLANES


def _ffn_body(h_ref, hp_ref, g_ref, wg_ref, wv_ref, cwg_ref, cwv_ref, cbg_ref, cbv_ref, wd_ref,
              o_ref, xn_ref, ug_ref, uv_ref, acc_ref, *, tm, seq):
    i = pl.program_id(0)
    j = pl.program_id(1)

    @pl.when(j == 0)
    def _():
        xn_ref[FFN_HALO:, :] = _rms(h_ref[...], g_ref[...]).astype(BF16)
        prev = _rms(hp_ref[...], g_ref[...])
        seq_start = (i * tm) % seq == 0
        xn_ref[:FFN_HALO, :] = jnp.where(seq_start, 0.0, prev).astype(BF16)
        acc_ref[...] = jnp.zeros_like(acc_ref)

    xn = xn_ref[...]
    ug_ref[...] = jnp.dot(xn, wg_ref[...], preferred_element_type=F32)
    uv_ref[...] = jnp.dot(xn, wv_ref[...], preferred_element_type=F32)

    def conv(u_ref, cw_ref, cb_ref):
        y = cb_ref[...]
        for t in range(CONV_WIDTH):
            off = FFN_HALO - (CONV_WIDTH - 1) + t
            y = y + cw_ref[t:t + 1, :] * u_ref[off:off + tm, :]
        return y

    yg = conv(ug_ref, cwg_ref, cbg_ref)
    yv = conv(uv_ref, cwv_ref, cbv_ref)
    a = (jax.nn.silu(yg) * yv).astype(BF16)
    acc_ref[...] += jnp.dot(a, wd_ref[...], preferred_element_type=F32)

    @pl.when(j == pl.num_programs(1) - 1)
    def _():
        o_ref[...] = h_ref[...] + acc_ref[...]


def _ffn(h, seq, gain, w_up, conv_w, conv_b, w_down, *, tm=512, tf=512):
    t, d = h.shape
    dff = w_down.shape[0]
    tm = min(tm, seq)
    assert seq % tm == 0 and dff % tf == 0 and tm % FFN_HALO == 0
    nf = dff // tf
    hb = tm // FFN_HALO
    conv_b2 = conv_b.reshape(1, 2 * dff)
    return pl.pallas_call(
        functools.partial(_ffn_body, tm=tm, seq=seq),
        grid=(t // tm, nf),
        in_specs=[
            pl.BlockSpec((tm, d), lambda i, j: (i, 0)),
            pl.BlockSpec((FFN_HALO, d), lambda i, j: (jnp.maximum(i * hb - 1, 0), 0)),
            pl.BlockSpec((1, d), lambda i, j: (0, 0)),
            pl.BlockSpec((d, tf), lambda i, j: (0, j)),
            pl.BlockSpec((d, tf), lambda i, j: (0, nf + j)),
            pl.BlockSpec((CONV_WIDTH, tf), lambda i, j: (0, j)),
            pl.BlockSpec((CONV_WIDTH, tf), lambda i, j: (0, nf + j)),
            pl.BlockSpec((1, tf), lambda i, j: (0, j)),
            pl.BlockSpec((1, tf), lambda i, j: (0, nf + j)),
            pl.BlockSpec((tf, d), lambda i, j: (j, 0)),
        ],
        out_specs=pl.BlockSpec((tm, d), lambda i, j: (i, 0)),
        out_shape=jax.ShapeDtypeStruct((t, d), F32),
        scratch_shapes=[
            pltpu.VMEM((tm + FFN_HALO, d), BF16),
            pltpu.VMEM((tm + FFN_HALO, tf), F32),
            pltpu.VMEM((tm + FFN_HALO, tf), F32),
            pltpu.VMEM((tm, d), F32),
        ],
        compiler_params=_params("arbitrary", "arbitrary"),
        name="ffn",
    )(h, h, gain.reshape(1, d), w_up, w_up, conv_w, conv_w, conv_b2, conv_b2, w_down)


def _rope_cos_sin(positions, rot):
    inv = ROPE_THETA ** (-jnp.arange(0, rot, 2, dtype=F32) / rot)
    ang = positions.astype(F32)[..., None] * inv
    return jnp.cos(ang), jnp.sin(ang)


def _rope_lane_tables(positions, rot):
    c, s = _rope_cos_sin(positions, rot)
    half = rot // 2
    shp = c.shape[:-1]
    cf = jnp.concatenate([c, c, jnp.ones(shp + (LANES - rot,), F32)], -1)
    sa = jnp.concatenate([-s, jnp.zeros(shp + (LANES - half,), F32)], -1)
    sb = jnp.concatenate([jnp.zeros(shp + (half,), F32), s, jnp.zeros(shp + (LANES - rot,), F32)], -1)
    return jnp.concatenate([cf, sa, sb], -1)


def _rope_glue(x, cos, sin):
    half = cos.shape[-1]
    x1, x2 = x[..., :half], x[..., half:2 * half]
    return jnp.concatenate([x1 * cos - x2 * sin, x1 * sin + x2 * cos, x[..., 2 * half:]], -1)


DSA_QX = 384
DSA_QPAD = MXU_DIM
DSA_HEAD_GROUP = 16
DSA_CNT_ROWS = 4 * SUBLANES


def _dsa_body(q_ref, qi_ref, wi_ref, tab_ref, kidx_ref, kext_ref, ct_ref, wk_ref, wuv_ref, tri_ref, o_ref,
              qx_ref, qis_ref, sc_ref, m_ref, acc_ref, *, kc, n_keep):
    i = pl.program_id(1)
    q0 = i * Q_BLOCK
    nc = (q0 + Q_BLOCK + kc - 1) // kc
    scale = DSA_QK_DIM ** -0.5 * LOG2E
    tab = tab_ref[0]
    cf, sa, sb = tab[:, 0:LANES], tab[:, LANES:2 * LANES], tab[:, 2 * LANES:3 * LANES]
    cfi, sai, sbi = tab[:, 3 * LANES:4 * LANES], tab[:, 4 * LANES:5 * LANES], tab[:, 5 * LANES:6 * LANES]

    for h in range(DSA_HEADS):
        xh = q_ref[0, :, h * DSA_QPAD:(h + 1) * DSA_QPAD].astype(F32)
        r0 = _rope_first_vreg(xh[:, :LANES], cf, sa, sb, DSA_ROPE // 2)
        qh = jnp.concatenate([r0, xh[:, LANES:]], axis=1).astype(BF16)
        qx = jnp.dot(qh, wk_ref[h], preferred_element_type=F32) * scale
        qx_ref[h * Q_BLOCK:(h + 1) * Q_BLOCK, :] = qx.astype(BF16)
    for h in range(IDX_HEADS):
        xi = qi_ref[0, :, h * IDX_DIM:(h + 1) * IDX_DIM].astype(F32)
        qis_ref[h * Q_BLOCK:(h + 1) * Q_BLOCK, :] = _rope_first_vreg(xi, cfi, sai, sbi, IDX_ROPE // 2).astype(BF16)

    wi = wi_ref[0, 0]
    key_i = lax.broadcasted_iota(jnp.int32, (kc, Q_BLOCK), 0)
    t_l = q0 + lax.broadcasted_iota(jnp.int32, (kc, Q_BLOCK), 1)

    def score_chunk(c, carry):
        lo, hi = carry
        k0 = pl.multiple_of(c * kc, kc)
        d = _nt(kidx_ref[0, pl.ds(k0, kc), :], qis_ref[...])
        d = jnp.maximum(d, 0.0)
        sc = d[:, 0:Q_BLOCK] * wi[0:1, :]
        for h in range(1, IDX_HEADS):
            sc = sc + d[:, h * Q_BLOCK:(h + 1) * Q_BLOCK] * wi[h:h + 1, :]
        sc = sc + 0.0
        causal = (k0 + key_i) <= t_l
        sc_ref[c] = jnp.where(causal, sc, -jnp.inf)
        lo = jnp.minimum(lo, jnp.where(causal, sc, jnp.inf).min(axis=0, keepdims=True))
        hi = jnp.maximum(hi, jnp.where(causal, sc, -jnp.inf).max(axis=0, keepdims=True))
        return lo, hi

    lo, hi = lax.fori_loop(0, nc, score_chunk,
                           (jnp.full((1, Q_BLOCK), jnp.inf, F32), jnp.full((1, Q_BLOCK), -jnp.inf, F32)))

    def bisect_more(carry):
        it, _, _, n_lo = carry
        return (it < BISECT_ITERS) & (jnp.max(n_lo) > n_keep)

    def bisect(carry):
        it, lo, hi, n_lo = carry

        def count_chunk_at(mid):
            def count_chunk(c, cnt):
                ge = jnp.where(sc_ref[c] >= mid, 1.0, 0.0)
                return cnt + ge.reshape(kc // DSA_CNT_ROWS, DSA_CNT_ROWS, Q_BLOCK).sum(axis=0)
            return count_chunk

        for _ in range(BISECT_UNROLL):
            mid = lo + (hi - lo) * 0.5
            cnt = lax.fori_loop(0, nc, count_chunk_at(mid), jnp.zeros((DSA_CNT_ROWS, Q_BLOCK), F32))
            cnt = cnt.sum(axis=0, keepdims=True)
            ge = cnt >= n_keep
            lo, hi, n_lo = jnp.where(ge, mid, lo), jnp.where(ge, hi, mid), jnp.where(ge, cnt, n_lo)
        return it + BISECT_UNROLL, lo, hi, n_lo

    n_causal = (q0 + 1 + lax.broadcasted_iota(jnp.int32, (1, Q_BLOCK), 1)).astype(F32)
    it_end, lo, hi, n_lo = lax.while_loop(bisect_more, bisect, (0, lo, hi, n_causal))

    @pl.when(it_end >= BISECT_ITERS)
    def _():
        def min_chunk(c, v):
            x = sc_ref[c]
            return jnp.minimum(v, jnp.where(x >= lo, x, jnp.inf).min(axis=0, keepdims=True))

        vth = lax.fori_loop(0, nc, min_chunk, jnp.full((1, Q_BLOCK), jnp.inf, F32))

        def gt_chunk(c, n):
            return n + jnp.where(sc_ref[c] > vth, 1.0, 0.0).sum(axis=0, keepdims=True)

        room = n_keep - lax.fori_loop(0, nc, gt_chunk, jnp.zeros((1, Q_BLOCK), F32))
        over = n_lo > n_keep

        def drop_chunk(c, seen):
            x = sc_ref[c]
            eq = jnp.where(x == vth, 1.0, 0.0)
            rank = seen + jnp.dot(tri_ref[...], eq.astype(BF16), preferred_element_type=F32)
            sc_ref[c] = jnp.where((eq > 0.5) & (rank > room) & over, -jnp.inf, x)
            return seen + eq.sum(axis=0, keepdims=True)

        lax.fori_loop(0, nc, drop_chunk, jnp.zeros((1, Q_BLOCK), F32))

    def bias_chunk(c, _):
        sc_ref[c] = jnp.where(sc_ref[c] >= lo, 0.0, MASK_NEG)
        return 0

    lax.fori_loop(0, nc, bias_chunk, 0)

    m_ref[...] = jnp.full_like(m_ref, MASK_NEG)
    acc_ref[...] = jnp.zeros_like(acc_ref)
    gl = DSA_HEAD_GROUP * Q_BLOCK

    def attn_chunk(c, _):
        k0 = pl.multiple_of(c * kc, kc)
        kx = kext_ref[0, pl.ds(k0, kc), :]
        ct = ct_ref[0, c]
        bias = jnp.tile(sc_ref[c], (1, DSA_HEAD_GROUP))
        n_slab = DSA_HEADS // DSA_HEAD_GROUP
        s = _nt(kx, qx_ref[0:gl, :]) + bias
        for g in range(n_slab):
            s_next = _nt(kx, qx_ref[(g + 1) * gl:(g + 2) * gl, :]) + bias if g + 1 < n_slab else None
            _online_softmax_step(s, ct, m_ref, acc_ref, slice(g * gl, (g + 1) * gl))
            s = s_next
        return 0

    lax.fori_loop(0, nc, attn_chunk, 0)

    for h in range(DSA_HEADS):
        cols = slice(h * Q_BLOCK, (h + 1) * Q_BLOCK)
        o_lat = (acc_ref[:DSA_KV_RANK, cols] / acc_ref[DSA_KV_RANK:DSA_KV_RANK + 1, cols]).astype(BF16)
        o_ref[0, :, h * DSA_V_DIM:(h + 1) * DSA_V_DIM] = _tn(o_lat, wuv_ref[h]).astype(o_ref.dtype)


def _dsa_attention(qall, wi_l, tabs, kidx, kext, ct, wk, wuv, *, kc):
    b, s, _ = qall.shape
    assert s % kc == 0 and kc % Q_BLOCK == 0
    n_keep = min(DSA_TOPK_MAX, s // 4)
    nq = DSA_HEADS * DSA_QPAD
    ni = IDX_HEADS * IDX_DIM
    assert nq % ni == 0
    nrow = DSA_HEADS * Q_BLOCK
    return pl.pallas_call(
        functools.partial(_dsa_body, kc=kc, n_keep=float(n_keep)),
        grid=(b, s // Q_BLOCK),
        in_specs=[
            pl.BlockSpec((1, Q_BLOCK, nq), lambda bb, i: (bb, i, 0)),
            pl.BlockSpec((1, Q_BLOCK, ni), lambda bb, i: (bb, i, nq // ni)),
            pl.BlockSpec((1, 1, IDX_HEADS, Q_BLOCK), lambda bb, i: (bb, i, 0, 0)),
            pl.BlockSpec((1, Q_BLOCK, 6 * LANES), lambda bb, i: (bb, i, 0)),
            pl.BlockSpec((1, s, IDX_DIM), lambda bb, i: (bb, 0, 0)),
            pl.BlockSpec((1, s, DSA_QX), lambda bb, i: (bb, 0, 0)),
            pl.BlockSpec((1, s // kc, DSA_KV_RANK + SUM_ROWS, kc), lambda bb, i: (bb, 0, 0, 0)),
            pl.BlockSpec((DSA_HEADS, DSA_QPAD, DSA_QX), lambda bb, i: (0, 0, 0)),
            pl.BlockSpec((DSA_HEADS, DSA_KV_RANK, DSA_V_DIM), lambda bb, i: (0, 0, 0)),
            pl.BlockSpec((kc, kc), lambda bb, i: (0, 0)),
        ],
        out_specs=pl.BlockSpec((1, Q_BLOCK, DSA_HEADS * DSA_V_DIM), lambda bb, i: (bb, i, 0)),
        out_shape=jax.ShapeDtypeStruct((b, s, DSA_HEADS * DSA_V_DIM), BF16),
        scratch_shapes=[
            pltpu.VMEM((nrow, DSA_QX), BF16),
            pltpu.VMEM((IDX_HEADS * Q_BLOCK, IDX_DIM), BF16),
            pltpu.VMEM((s // kc, kc, Q_BLOCK), F32),
            pltpu.VMEM((1, nrow), F32),
            pltpu.VMEM((DSA_KV_RANK + SUM_ROWS, nrow), F32),
        ],
        compiler_params=_params("arbitrary", "arbitrary"),
        name="dsa_attn",
    )(qall, qall, wi_l, tabs, kidx, kext, ct, wk, wuv, jnp.tril(jnp.ones((kc, kc), BF16)))


DSA_PROJ_W = (DSA_Q_RANK, DSA_KV_RANK, LANES, LANES, LANES)


def _dsa_kprep_body(ckv_ref, kpe_ref, kid_ref, wid_ref, tab_ref, kvn_ref, lng_ref, lnb_ref,
                    kext_ref, kidx_ref, ct_ref, wi_ref, *, tm):
    tab = tab_ref[...]
    cf, sa, sb = tab[:, 0:LANES], tab[:, LANES:2 * LANES], tab[:, 2 * LANES:3 * LANES]
    cfi, sai, sbi = tab[:, 3 * LANES:4 * LANES], tab[:, 4 * LANES:5 * LANES], tab[:, 5 * LANES:6 * LANES]
    c_n = _rms(ckv_ref[...], kvn_ref[...])
    kext_ref[:, :DSA_KV_RANK] = c_n.astype(BF16)
    kext_ref[:, DSA_KV_RANK:] = _rope_first_vreg(kpe_ref[...], cf, sa, sb, DSA_ROPE // 2).astype(BF16)
    ct_ref[0, 0, :DSA_KV_RANK, :] = c_n.T.astype(BF16)
    ct_ref[0, 0, DSA_KV_RANK:, :] = jnp.ones((SUM_ROWS, tm), BF16)
    x = kid_ref[...]
    mu = jnp.mean(x, axis=-1, keepdims=True)
    var = jnp.mean(jnp.square(x - mu), axis=-1, keepdims=True)
    k_ln = (x - mu) * lax.rsqrt(var + NORM_EPS) * lng_ref[...] + lnb_ref[...]
    kidx_ref[...] = _rope_first_vreg(k_ln, cfi, sai, sbi, IDX_ROPE // 2).astype(BF16)
    w = wid_ref[...] * (IDX_HEADS ** -0.5 * IDX_DIM ** -0.5)
    for r in range(tm // Q_BLOCK):
        wi_ref[0, r] = w[r * Q_BLOCK:(r + 1) * Q_BLOCK, :].T[:IDX_HEADS, :]


def _dsa_kprep(proj, tabs, kv_norm, ln_g, ln_b, bsz, seq, kc):
    t = proj.shape[0]
    tm = kc
    nt = seq // tm
    col = [int(c) for c in np.cumsum((0,) + DSA_PROJ_W)]
    assert all(c % w == 0 for c, w in zip(col[1:-1], DSA_PROJ_W[1:]))

    def row(v):
        return v.reshape(1, -1)

    return pl.pallas_call(
        functools.partial(_dsa_kprep_body, tm=tm),
        grid=(t // tm,),
        in_specs=[
            pl.BlockSpec((tm, DSA_KV_RANK), lambda i: (i, col[1] // DSA_KV_RANK)),
            pl.BlockSpec((tm, LANES), lambda i: (i, col[2] // LANES)),
            pl.BlockSpec((tm, LANES), lambda i: (i, col[3] // LANES)),
            pl.BlockSpec((tm, LANES), lambda i: (i, col[4] // LANES)),
            pl.BlockSpec((tm, 6 * LANES), lambda i: (i, 0)),
            pl.BlockSpec((1, DSA_KV_RANK), lambda i: (0, 0)),
            pl.BlockSpec((1, IDX_DIM), lambda i: (0, 0)),
            pl.BlockSpec((1, IDX_DIM), lambda i: (0, 0)),
        ],
        out_specs=[
            pl.BlockSpec((tm, DSA_QX), lambda i: (i, 0)),
            pl.BlockSpec((tm, IDX_DIM), lambda i: (i, 0)),
            pl.BlockSpec((1, 1, DSA_KV_RANK + SUM_ROWS, tm), lambda i: (i // nt, i % nt, 0, 0)),
            pl.BlockSpec((1, tm // Q_BLOCK, IDX_HEADS, Q_BLOCK), lambda i: (i // nt, i % nt, 0, 0)),
        ],
        out_shape=[
            jax.ShapeDtypeStruct((t, DSA_QX), BF16),
            jax.ShapeDtypeStruct((t, IDX_DIM), BF16),
            jax.ShapeDtypeStruct((bsz, nt, DSA_KV_RANK + SUM_ROWS, tm), BF16),
            jax.ShapeDtypeStruct((bsz, seq // Q_BLOCK, IDX_HEADS, Q_BLOCK), F32),
        ],
        compiler_params=_params("arbitrary"),
        name="dsa_kprep",
    )(proj, proj, proj, proj, tabs, row(kv_norm), row(ln_g), row(ln_b))


def _dsa_mixer(h2, bsz, seq, positions, attn_norm, w_in, q_norm, kv_norm, idx_ln_g, idx_ln_b,
               w_qup, w_uk, w_uv, w_out, *, kc=512):
    t = h2.shape[0]
    kc = min(kc, seq)
    sizes = (DSA_Q_RANK, DSA_KV_RANK, DSA_ROPE, IDX_DIM, IDX_HEADS)
    cuts = np.cumsum((0,) + sizes)
    w_in_p = jnp.concatenate(
        [jnp.pad(w_in[:, int(a):int(b)], ((0, 0), (0, wp - (int(b) - int(a)))))
         for a, b, wp in zip(cuts[:-1], cuts[1:], DSA_PROJ_W)], axis=1).astype(BF16)
    proj = _mm(h2, w_in_p, gain=attn_norm, tm=1024, tn=w_in_p.shape[1])

    nq = DSA_HEADS * DSA_QK_DIM
    w_q = w_qup[:, :nq].reshape(DSA_Q_RANK, DSA_HEADS, DSA_QK_DIM)
    w_q = jnp.pad(w_q, ((0, 0), (0, 0), (0, DSA_QPAD - DSA_QK_DIM))).reshape(DSA_Q_RANK, DSA_HEADS * DSA_QPAD)
    w_qp = jnp.concatenate([w_q, w_qup[:, nq:]], axis=1).astype(BF16)
    qall = _mm(proj, w_qp, gain=q_norm, out_dtype=BF16, tn=w_qp.shape[1])

    pos = positions.reshape(t)
    tabs = jnp.concatenate([_rope_lane_tables(pos, DSA_ROPE), _rope_lane_tables(pos, IDX_ROPE)], -1)
    kext, kidx, ct, wi = _dsa_kprep(proj, tabs, kv_norm, idx_ln_g, idx_ln_b, bsz, seq, kc)

    wk = jnp.zeros((DSA_HEADS, DSA_QPAD, DSA_QX), F32)
    wk = wk.at[:, DSA_ROPE:DSA_QK_DIM, :DSA_KV_RANK].set(jnp.swapaxes(w_uk, 1, 2))
    eye = jnp.eye(DSA_ROPE, dtype=F32)
    wk = wk.at[:, :DSA_ROPE, DSA_KV_RANK:DSA_KV_RANK + DSA_ROPE].set(jnp.broadcast_to(eye, (DSA_HEADS,) + eye.shape))

    o = _dsa_attention(
        qall.reshape(bsz, seq, -1), wi, tabs.reshape(bsz, seq, -1),
        kidx.reshape(bsz, seq, IDX_DIM), kext.reshape(bsz, seq, DSA_QX), ct,
        wk.astype(BF16), w_uv.astype(BF16), kc=kc)
    return _mm(o.reshape(t, -1), w_out.astype(BF16), res=h2, tm=1024)


FOX_KX = MXU_DIM
FOX_HEAD_PAIR = 2


def _fox_body(q_ref, k_ref, v_ref, c3_ref, e_ref, o_ref, qx_ref, kx_ref, vt_ref, m_ref, acc_ref, *, tq, kc):
    i = pl.program_id(2)
    q0 = i * tq
    n_full = q0 // kc
    n_diag = tq // kc

    @pl.when(i == 0)
    def _():
        for a in range(FOX_HEAD_PAIR):
            kx_ref[a, :, :FOX_DIM] = k_ref[0, :, a * FOX_DIM:(a + 1) * FOX_DIM]
            kx_ref[a, :, FOX_DIM:] = _tn(c3_ref[0, a], e_ref[...]).astype(BF16)
            for c in range(vt_ref.shape[1]):
                vc = v_ref[0, c * kc:(c + 1) * kc, a * FOX_DIM:(a + 1) * FOX_DIM]
                vt_ref[a, c, :FOX_DIM, :] = vc.astype(F32).T.astype(BF16)
                vt_ref[a, c, FOX_DIM:, :] = jnp.ones((SUM_ROWS, kc), BF16)

    lane = lax.broadcasted_iota(jnp.int32, (tq, FOX_KX - FOX_DIM), 1)
    for a in range(FOX_HEAD_PAIR):
        qx_ref[a, :, :FOX_DIM] = q_ref[0, :, a * FOX_DIM:(a + 1) * FOX_DIM]
        qx_ref[a, :, FOX_DIM:] = jnp.where(lane < 3, 1.0, 0.0).astype(BF16)
    m_ref[...] = jnp.full_like(m_ref, MASK_NEG)
    acc_ref[...] = jnp.zeros_like(acc_ref)
    key_i = lax.broadcasted_iota(jnp.int32, (kc, tq), 0)
    t_l = q0 + lax.broadcasted_iota(jnp.int32, (kc, tq), 1)

    def step(c, masked):
        k0 = pl.multiple_of(c * kc, kc)

        def logits(a):
            s = _nt(kx_ref[a, pl.ds(k0, kc), :], qx_ref[a])
            return jnp.where((k0 + key_i) <= t_l, s, MASK_NEG) if masked else s

        s = logits(0)
        for a in range(FOX_HEAD_PAIR):
            s_next = logits(a + 1) if a + 1 < FOX_HEAD_PAIR else None
            _online_softmax_step(s, vt_ref[a, c], m_ref, acc_ref, slice(a * tq, (a + 1) * tq))
            s = s_next

    def full_chunk(c, _):
        step(c, False)
        return 0

    def diag_chunk(c, _):
        step(c, True)
        return 0

    lax.fori_loop(0, n_full, full_chunk, 0)
    lax.fori_loop(n_full, n_full + n_diag, diag_chunk, 0)
    for a in range(FOX_HEAD_PAIR):
        cols = slice(a * tq, (a + 1) * tq)
        o_a = acc_ref[:FOX_DIM, cols] / acc_ref[FOX_DIM:FOX_DIM + 1, cols]
        o_ref[0, :, a * FOX_DIM:(a + 1) * FOX_DIM] = o_a.T.astype(o_ref.dtype)


def _fox_attention(qkv, c3, *, tq, kc):
    b, s, _ = qkv.shape
    hp = FOX_HEAD_PAIR
    npair = FOX_HEADS // hp
    assert s % tq == 0 and tq % kc == 0 and FOX_HEADS % hp == 0
    e = jnp.eye(SUBLANES, FOX_KX - FOX_DIM, dtype=BF16)
    return pl.pallas_call(
        functools.partial(_fox_body, tq=tq, kc=kc),
        grid=(b, npair, s // tq),
        in_specs=[
            pl.BlockSpec((1, tq, hp * FOX_DIM), lambda bb, h, i: (bb, i, h)),
            pl.BlockSpec((1, s, hp * FOX_DIM), lambda bb, h, i: (bb, 0, npair + h)),
            pl.BlockSpec((1, s, hp * FOX_DIM), lambda bb, h, i: (bb, 0, 2 * npair + h)),
            pl.BlockSpec((1, hp, SUBLANES, s), lambda bb, h, i: (bb, h, 0, 0)),
            pl.BlockSpec((SUBLANES, FOX_KX - FOX_DIM), lambda bb, h, i: (0, 0)),
        ],
        out_specs=pl.BlockSpec((1, tq, hp * FOX_DIM), lambda bb, h, i: (bb, i, h)),
        out_shape=jax.ShapeDtypeStruct((b, s, FOX_HEADS * FOX_DIM), BF16),
        scratch_shapes=[pltpu.VMEM((hp, tq, FOX_KX), BF16), pltpu.VMEM((hp, s, FOX_KX), BF16),
                        pltpu.VMEM((hp, s // kc, FOX_DIM + SUM_ROWS, kc), BF16),
                        pltpu.VMEM((1, hp * tq), F32), pltpu.VMEM((FOX_DIM + SUM_ROWS, hp * tq), F32)],
        compiler_params=_params("arbitrary", "arbitrary", "arbitrary"),
        name="fox_attn",
    )(qkv, qkv, qkv, c3, e)


def _fox_mixer(h2, bsz, seq, attn_norm, w_in, b_f, w_out, *, tq=1024, kc=1024):
    t = h2.shape[0]
    hd = FOX_HEADS * FOX_DIM
    tq = min(tq, seq)
    kc = min(kc, tq)
    scale = FOX_DIM ** -0.5 * LOG2E
    w_qkv = jnp.concatenate([w_in[:, :hd] * scale, w_in[:, hd:3 * hd]], axis=1).astype(BF16)
    qkv = _mm(h2, w_qkv, gain=attn_norm, out_dtype=BF16, tm=1024, tn=1024)
    w_f = jnp.pad(w_in[:, 3 * hd:], ((0, 0), (0, LANES - FOX_HEADS))).astype(BF16)
    f_logit = _mm(h2, w_f, gain=attn_norm)[:, :FOX_HEADS]
    log_f = jax.nn.log_sigmoid(f_logit + b_f)
    cum = jnp.cumsum(log_f.reshape(bsz, seq, FOX_HEADS), axis=1)
    c3 = jnp.stack(_split3_bits(-cum * LOG2E), axis=0).transpose(1, 3, 0, 2)
    c3 = jnp.pad(c3, ((0, 0), (0, 0), (0, SUBLANES - 3), (0, 0)))
    o = _fox_attention(qkv.reshape(bsz, seq, 3 * hd), c3, tq=tq, kc=kc)
    return _mm(o.reshape(t, hd), w_out.astype(BF16), res=h2, tm=1024)


NSA_HEAD_GROUP = 4


def _cmp_body(x_ref, pe_ref, w1_ref, b1_ref, w2_ref, o_ref, *, half):
    x = x_ref[0, 0]
    w_lo = w1_ref[:half, :]
    w_hi = w1_ref[half:, :]
    a = jnp.dot(x, w_lo, preferred_element_type=F32)
    bnext = jnp.dot(x, w_hi, preferred_element_type=F32)
    n = x.shape[0]
    bnext = pltpu.roll(bnext, n - 1, 0)
    pe = pe_ref[...]
    pe_b = (jnp.dot(pe[:, :half], w_lo, preferred_element_type=F32)
            + jnp.dot(pe[:, half:], w_hi, preferred_element_type=F32))[0:1, :]
    hid = jax.nn.gelu(a + bnext + pe_b + b1_ref[...])
    o_ref[0, 0] = jnp.dot(hid.astype(BF16), w2_ref[...], preferred_element_type=F32).astype(o_ref.dtype)


def _compress(x, pe, w1, b1, w2):
    b, g, n, kd = x.shape
    dout = w2.shape[1]
    pe8 = jnp.broadcast_to(pe.reshape(1, 2 * kd), (8, 2 * kd)).astype(BF16)
    return pl.pallas_call(
        functools.partial(_cmp_body, half=kd),
        grid=(b, g),
        in_specs=[
            pl.BlockSpec((1, 1, n, kd), lambda bb, gg: (bb, gg, 0, 0)),
            pl.BlockSpec((8, 2 * kd), lambda bb, gg: (0, 0)),
            pl.BlockSpec((2 * kd, CMP_HIDDEN), lambda bb, gg: (0, 0)),
            pl.BlockSpec((1, CMP_HIDDEN), lambda bb, gg: (0, 0)),
            pl.BlockSpec((CMP_HIDDEN, dout), lambda bb, gg: (0, 0)),
        ],
        out_specs=pl.BlockSpec((1, 1, n, dout), lambda bb, gg: (bb, gg, 0, 0)),
        out_shape=jax.ShapeDtypeStruct((b, g, n, dout), BF16),
        compiler_params=_params("arbitrary", "arbitrary"),
        name="nsa_compress",
    )(x, pe8, w1.astype(BF16), b1.reshape(1, CMP_HIDDEN), w2.astype(BF16))


def _nsa_body(q_ref, tab_ref, gate_ref, kc_ref, vct_ref, ks_ref, vs_ref, kw_ref, vw_ref, ov_ref, exp_ref,
              o_ref, qs_ref, vst_ref, vwt_ref, m_ref, acc_ref, *, kc, n_cmp, n_slc, n_sel):
    i = pl.program_id(2)

    @pl.when(i == 0)
    def _():
        for c in range(vst_ref.shape[0]):
            vst_ref[c, :NSA_V_DIM, :] = vs_ref[0, c * kc:(c + 1) * kc, :].T.astype(BF16)
            vst_ref[c, NSA_V_DIM:, :] = jnp.ones((SUM_ROWS, kc), BF16)
        npad = WINDOW // Q_BLOCK
        for c in range(vwt_ref.shape[0]):
            if c < npad:
                vwt_ref[c] = jnp.zeros((NSA_V_DIM, Q_BLOCK), BF16)
            else:
                vwt_ref[c] = vw_ref[0, (c - npad) * Q_BLOCK:(c - npad + 1) * Q_BLOCK, :].T.astype(BF16)

    q0 = i * Q_BLOCK
    nc = (q0 + Q_BLOCK + kc - 1) // kc
    jh = NSA_HPG
    scale = NSA_QK_DIM ** -0.5 * LOG2E
    tab = tab_ref[0]
    cf, sa, sb = tab[:, 0:LANES], tab[:, LANES:2 * LANES], tab[:, 2 * LANES:3 * LANES]

    for j in range(jh):
        xj = q_ref[0, :, j * NSA_QK_DIM:(j + 1) * NSA_QK_DIM].astype(F32)
        r0 = _rope_first_vreg(xj[:, :LANES], cf, sa, sb, NSA_ROPE // 2)
        qs_ref[j * Q_BLOCK:(j + 1) * Q_BLOCK, :] = (jnp.concatenate([r0, xj[:, LANES:]], axis=1) * scale).astype(BF16)
    qs = qs_ref[...]

    n_id = lax.broadcasted_iota(jnp.int32, (n_cmp, Q_BLOCK), 0)
    t_c = q0 + lax.broadcasted_iota(jnp.int32, (n_cmp, Q_BLOCK), 1)
    cbias = jnp.where((n_id * CMP_STRIDE + (CMP_BLOCK - 1)) <= t_c, 0.0, MASK_NEG)
    pc = _softmax_cols(_nt(kc_ref[0, 0], qs) + jnp.tile(cbias, (1, jh)))
    o_c = jnp.dot(vct_ref[0, 0], pc.astype(BF16), preferred_element_type=F32)

    pcs = pc[:, 0:Q_BLOCK]
    for j in range(1, jh):
        pcs = pcs + pc[:, j * Q_BLOCK:(j + 1) * Q_BLOCK]
    ov = ov_ref[...]
    imp = sum(jnp.dot(ov, term, preferred_element_type=F32) for term in _split3(pcs))

    blk = lax.broadcasted_iota(jnp.int32, (n_slc, Q_BLOCK), 0)
    t_b = q0 + lax.broadcasted_iota(jnp.int32, (n_slc, Q_BLOCK), 1)
    cur = lax.shift_right_logical(t_b, int(np.log2(SLC_BLOCK)))
    causal_blk = blk * SLC_BLOCK <= t_b
    forced = (blk == 0) | (blk == cur) | (blk == cur - 1)
    val = jnp.where(causal_blk, jnp.where(forced, jnp.inf, imp), -jnp.inf)
    rank = jnp.zeros((n_slc, Q_BLOCK), F32)
    for mp in range(n_slc):
        vrow = val[mp:mp + 1, :]
        before = (vrow > val) | ((vrow == val) & (blk > mp))
        rank = rank + jnp.where(before, 1.0, 0.0)
    sel = jnp.where((rank < n_sel) & causal_blk, 1.0, 0.0)
    sel_p = jnp.concatenate([sel, jnp.zeros((LANES - n_slc, Q_BLOCK), F32)], axis=0).astype(BF16)

    m_ref[...] = jnp.full_like(m_ref, MASK_NEG)
    acc_ref[...] = jnp.zeros_like(acc_ref)
    key_i = lax.broadcasted_iota(jnp.int32, (kc, Q_BLOCK), 0)
    t_k = q0 + lax.broadcasted_iota(jnp.int32, (kc, Q_BLOCK), 1)
    gl = NSA_HEAD_GROUP * Q_BLOCK

    def sel_chunk(c, _):
        k0 = pl.multiple_of(c * kc, kc)
        hit = jnp.dot(exp_ref[pl.ds(k0, kc), :], sel_p, preferred_element_type=F32)
        bias = jnp.where((hit > 0.5) & ((k0 + key_i) <= t_k), 0.0, MASK_NEG)
        bias = jnp.tile(bias, (1, NSA_HEAD_GROUP))
        ksc = ks_ref[0, 0, pl.ds(k0, kc), :]
        vt = vst_ref[c]
        n_slab = jh // NSA_HEAD_GROUP
        s = _nt(ksc, qs_ref[0:gl, :]) + bias
        for hg in range(n_slab):
            s_next = _nt(ksc, qs_ref[(hg + 1) * gl:(hg + 2) * gl, :]) + bias if hg + 1 < n_slab else None
            _online_softmax_step(s, vt, m_ref, acc_ref, slice(hg * gl, (hg + 1) * gl))
            s = s_next
        return 0

    lax.fori_loop(0, nc, sel_chunk, 0)
    o_s = acc_ref[:NSA_V_DIM, :] / acc_ref[NSA_V_DIM:NSA_V_DIM + 1, :]

    wl = WINDOW + Q_BLOCK
    kstart = pl.multiple_of(q0, Q_BLOCK)
    s_pos = q0 - WINDOW + lax.broadcasted_iota(jnp.int32, (wl, Q_BLOCK), 0)
    t_w = q0 + lax.broadcasted_iota(jnp.int32, (wl, Q_BLOCK), 1)
    wbias = jnp.where((s_pos >= 0) & (s_pos <= t_w) & (s_pos > t_w - WINDOW), 0.0, MASK_NEG)
    pw = _softmax_cols(_nt(kw_ref[0, 0, pl.ds(kstart, wl), :], qs) + jnp.tile(wbias, (1, jh))).astype(BF16)
    o_w = jnp.dot(vwt_ref[i], pw[0:Q_BLOCK, :], preferred_element_type=F32)
    for c in range(1, wl // Q_BLOCK):
        o_w = o_w + jnp.dot(vwt_ref[i + c], pw[c * Q_BLOCK:(c + 1) * Q_BLOCK, :],
                            preferred_element_type=F32)

    g = gate_ref[0, 0]
    for j in range(jh):
        cj = slice(j * Q_BLOCK, (j + 1) * Q_BLOCK)
        out = (g[j:j + 1, :] * o_c[:, cj] + g[jh + j:jh + j + 1, :] * o_s[:, cj]
               + g[2 * jh + j:2 * jh + j + 1, :] * o_w[:, cj])
        o_ref[0, :, j * NSA_V_DIM:(j + 1) * NSA_V_DIM] = out.T.astype(o_ref.dtype)


def _nsa_attention(q, tabs, gates, kcmp, vcmp_t, ks, kw, rest, vs_col, vw_col, *, kc):
    b, s, _ = q.shape
    g = NSA_GROUPS
    assert s % kc == 0 and kc % SLC_BLOCK == 0
    n_cmp = kcmp.shape[2]
    n_slc = s // SLC_BLOCK
    assert n_slc <= LANES
    n_sel = min(SLC_TOPK, n_slc)
    cmp_start = np.arange(n_cmp) * CMP_STRIDE
    slc_start = np.arange(n_slc) * SLC_BLOCK
    ov = ((cmp_start[None, :] < slc_start[:, None] + SLC_BLOCK)
          & (cmp_start[None, :] + CMP_BLOCK > slc_start[:, None])).astype(np.float32)
    expand = (np.arange(s)[:, None] // SLC_BLOCK == np.arange(LANES)[None, :]).astype(np.float32)
    qw = NSA_HPG * NSA_QK_DIM
    ow = NSA_HPG * NSA_V_DIM
    nrow = NSA_HPG * Q_BLOCK
    wl = s + WINDOW
    return pl.pallas_call(
        functools.partial(_nsa_body, kc=kc, n_cmp=n_cmp, n_slc=n_slc, n_sel=float(n_sel)),
        grid=(b, g, s // Q_BLOCK),
        in_specs=[
            pl.BlockSpec((1, Q_BLOCK, qw), lambda bb, gg, i: (bb, i, gg)),
            pl.BlockSpec((1, Q_BLOCK, 3 * LANES), lambda bb, gg, i: (bb, i, 0)),
            pl.BlockSpec((1, 1, 3 * NSA_HPG, Q_BLOCK), lambda bb, gg, i: (bb, gg, 0, i)),
            pl.BlockSpec((1, 1, n_cmp, NSA_QK_DIM), lambda bb, gg, i: (bb, gg, 0, 0)),
            pl.BlockSpec((1, 1, NSA_V_DIM, n_cmp), lambda bb, gg, i: (bb, gg, 0, 0)),
            pl.BlockSpec((1, 1, s, NSA_QK_DIM), lambda bb, gg, i: (bb, gg, 0, 0)),
            pl.BlockSpec((1, s, NSA_V_DIM), lambda bb, gg, i: (bb, 0, vs_col + gg)),
            pl.BlockSpec((1, 1, wl, NSA_QK_DIM), lambda bb, gg, i: (bb, gg, 0, 0)),
            pl.BlockSpec((1, s, NSA_V_DIM), lambda bb, gg, i: (bb, 0, vw_col + gg)),
            pl.BlockSpec((n_slc, n_cmp), lambda bb, gg, i: (0, 0)),
            pl.BlockSpec((s, LANES), lambda bb, gg, i: (0, 0)),
        ],
        out_specs=pl.BlockSpec((1, Q_BLOCK, ow), lambda bb, gg, i: (bb, i, gg)),
        out_shape=jax.ShapeDtypeStruct((b, s, NSA_HEADS * NSA_V_DIM), BF16),
        scratch_shapes=[
            pltpu.VMEM((nrow, NSA_QK_DIM), BF16),
            pltpu.VMEM((s // kc, NSA_V_DIM + SUM_ROWS, kc), BF16),
            pltpu.VMEM((wl // Q_BLOCK, NSA_V_DIM, Q_BLOCK), BF16),
            pltpu.VMEM((1, nrow), F32),
            pltpu.VMEM((NSA_V_DIM + SUM_ROWS, nrow), F32),
        ],
        compiler_params=_params("arbitrary", "arbitrary", "arbitrary"),
        name="nsa_attn",
    )(q, tabs, gates, kcmp, vcmp_t, ks, rest, kw, rest, jnp.asarray(ov, BF16), jnp.asarray(expand, BF16))


def _nsa_mixer(h2, bsz, seq, positions, attn_norm, w_in, k_pe, k_w1, k_b1, k_w2, v_pe, v_w1, v_b1, v_w2, w_out,
               *, kc=1024):
    t = h2.shape[0]
    kc = min(kc, seq)
    g, jh, dk, dv = NSA_GROUPS, NSA_HPG, NSA_QK_DIM, NSA_V_DIM
    nq = NSA_HEADS * dk
    n_rest = w_in.shape[1] - nq
    n_rest_p = -(-n_rest // 512) * 512
    q = _mm(h2, w_in[:, :nq].astype(BF16), gain=attn_norm, out_dtype=BF16, tm=1024, tn=1024)
    w_rest = jnp.pad(w_in[:, nq:], ((0, 0), (0, n_rest_p - n_rest))).astype(BF16)
    rest = _mm(h2, w_rest, gain=attn_norm, tm=1024, tn=1024)
    offs = np.cumsum([0, NSA_KD, NSA_VD, NSA_KD, NSA_VD, NSA_KD, NSA_VD, 3 * NSA_HEADS])
    k_c, v_c, k_s, v_s, k_w, v_w, g_logit = [rest[:, int(a):int(b)] for a, b in zip(offs[:-1], offs[1:])]

    pos = positions.reshape(t)
    cos, sin = _rope_cos_sin(pos, NSA_ROPE)

    def rope_k(k):
        return _rope_glue(k.reshape(t, g, dk), cos[:, None, :], sin[:, None, :])

    def per_group(x, d):
        return x.reshape(bsz, seq, g, d).transpose(0, 2, 1, 3).astype(BF16)

    def chunks(x, d):
        x = x.reshape(bsz, seq // CMP_STRIDE, CMP_STRIDE, g, d).transpose(0, 3, 1, 2, 4)
        return x.reshape(bsz, g, seq // CMP_STRIDE, CMP_STRIDE * d).astype(BF16)

    kcmp = _compress(chunks(rope_k(k_c), dk), k_pe, k_w1, k_b1, k_w2)
    vcmp_t = jnp.swapaxes(_compress(chunks(v_c.reshape(t, g, dv), dv), v_pe, v_w1, v_b1, v_w2), -1, -2)
    ks = per_group(rope_k(k_s), dk)
    pad = ((0, 0), (0, 0), (WINDOW, 0), (0, 0))
    kw = jnp.pad(per_group(rope_k(k_w), dk), pad)
    gates = jax.nn.sigmoid(g_logit).reshape(bsz, seq, 3, g, jh).transpose(0, 3, 2, 4, 1).reshape(bsz, g, 3 * jh, seq)
    tabs = _rope_lane_tables(pos, NSA_ROPE).reshape(bsz, seq, 3 * LANES)

    assert int(offs[3]) % dv == 0 and int(offs[5]) % dv == 0
    o = _nsa_attention(q.reshape(bsz, seq, nq), tabs, gates, kcmp, vcmp_t, ks, kw, rest.reshape(bsz, seq, -1),
                       int(offs[3]) // dv, int(offs[5]) // dv, kc=kc)
    return _mm(o.reshape(t, NSA_HEADS * dv), w_out.astype(BF16), res=h2, tm=1024)


def _norm_body(x_ref, g_ref, o_ref):
    o_ref[...] = _rms(x_ref[...], g_ref[...])


def _final_norm(h2, gain, *, tm=512):
    t, d = h2.shape
    tm = min(tm, t)
    return pl.pallas_call(
        _norm_body,
        grid=(t // tm,),
        in_specs=[pl.BlockSpec((tm, d), lambda i: (i, 0)), pl.BlockSpec((1, d), lambda i: (0, 0))],
        out_specs=pl.BlockSpec((tm, d), lambda i: (i, 0)),
        out_shape=jax.ShapeDtypeStruct((t, d), F32),
        compiler_params=_params("arbitrary"),
        name="final_norm",
    )(h2, gain.reshape(1, d))


def _ffn_layer(h2, seq, ffn_norm, w_up, conv_w, conv_b, w_down):
    return _ffn(h2, seq, ffn_norm, w_up.astype(BF16), conv_w, conv_b, w_down.astype(BF16))


def kernel(x, positions, l0_attn_norm, l0_dsa_w_in, l0_dsa_q_norm, l0_dsa_kv_norm, l0_dsa_idx_ln_g, l0_dsa_idx_ln_b, l0_dsa_w_qup, l0_dsa_w_uk, l0_dsa_w_uv, l0_dsa_w_out, l0_ffn_norm, l0_ffn_up, l0_ffn_conv_w, l0_ffn_conv_b, l0_ffn_down, l1_attn_norm, l1_fox_w_in, l1_fox_b_f, l1_fox_w_out, l1_ffn_norm, l1_ffn_up, l1_ffn_conv_w, l1_ffn_conv_b, l1_ffn_down, l2_attn_norm, l2_nsa_w_in, l2_nsa_k_pe, l2_nsa_k_w1, l2_nsa_k_b1, l2_nsa_k_w2, l2_nsa_v_pe, l2_nsa_v_w1, l2_nsa_v_b1, l2_nsa_v_w2, l2_nsa_w_out, l2_ffn_norm, l2_ffn_up, l2_ffn_conv_w, l2_ffn_conv_b, l2_ffn_down, l3_attn_norm, l3_dsa_w_in, l3_dsa_q_norm, l3_dsa_kv_norm, l3_dsa_idx_ln_g, l3_dsa_idx_ln_b, l3_dsa_w_qup, l3_dsa_w_uk, l3_dsa_w_uv, l3_dsa_w_out, l3_ffn_norm, l3_ffn_up, l3_ffn_conv_w, l3_ffn_conv_b, l3_ffn_down, final_norm):
    bsz, seq, d = x.shape
    h = x.reshape(bsz * seq, d)
    h = _dsa_mixer(h, bsz, seq, positions, l0_attn_norm, l0_dsa_w_in, l0_dsa_q_norm, l0_dsa_kv_norm,
                   l0_dsa_idx_ln_g, l0_dsa_idx_ln_b, l0_dsa_w_qup, l0_dsa_w_uk, l0_dsa_w_uv, l0_dsa_w_out)
    h = _ffn_layer(h, seq, l0_ffn_norm, l0_ffn_up, l0_ffn_conv_w, l0_ffn_conv_b, l0_ffn_down)
    h = _fox_mixer(h, bsz, seq, l1_attn_norm, l1_fox_w_in, l1_fox_b_f, l1_fox_w_out)
    h = _ffn_layer(h, seq, l1_ffn_norm, l1_ffn_up, l1_ffn_conv_w, l1_ffn_conv_b, l1_ffn_down)
    h = _nsa_mixer(h, bsz, seq, positions, l2_attn_norm, l2_nsa_w_in, l2_nsa_k_pe, l2_nsa_k_w1, l2_nsa_k_b1,
                   l2_nsa_k_w2, l2_nsa_v_pe, l2_nsa_v_w1, l2_nsa_v_b1, l2_nsa_v_w2, l2_nsa_w_out)
    h = _ffn_layer(h, seq, l2_ffn_norm, l2_ffn_up, l2_ffn_conv_w, l2_ffn_conv_b, l2_ffn_down)
    h = _dsa_mixer(h, bsz, seq, positions, l3_attn_norm, l3_dsa_w_in, l3_dsa_q_norm, l3_dsa_kv_norm,
                   l3_dsa_idx_ln_g, l3_dsa_idx_ln_b, l3_dsa_w_qup, l3_dsa_w_uk, l3_dsa_w_uv, l3_dsa_w_out)
    h = _ffn_layer(h, seq, l3_ffn_norm, l3_ffn_up, l3_ffn_conv_w, l3_ffn_conv_b, l3_ffn_down)
    return _final_norm(h, final_norm).reshape(bsz, seq, d)
```

```python
import functools

import jax
import jax.numpy as jnp
import numpy as np
from jax import lax
from jax.experimental import pallas as pl
from jax.experimental.pallas import tpu as pltpu

F32 = jnp.float32
BF16 = jnp.bfloat16

ROPE_THETA = 500000.0
NORM_EPS = 1e-6
Q_BLOCK = 128

DSA_HEADS = 32
DSA_Q_RANK = 512
DSA_KV_RANK = 256
DSA_QK_DIM = 192
DSA_ROPE = 48
DSA_NOPE = DSA_QK_DIM - DSA_ROPE
DSA_V_DIM = 128
IDX_HEADS = 16
IDX_DIM = 128
IDX_ROPE = 32
DSA_TOPK_MAX = 256
DSA_IN = DSA_Q_RANK + DSA_KV_RANK + DSA_ROPE + IDX_DIM + IDX_HEADS

FOX_HEADS = 16
FOX_DIM = 128

NSA_HEADS = 48
NSA_GROUPS = 4
NSA_HPG = NSA_HEADS // NSA_GROUPS
NSA_QK_DIM = 192
NSA_ROPE = 48
NSA_V_DIM = 128
CMP_BLOCK = 32
CMP_STRIDE = 16
CMP_HIDDEN = 256
SLC_BLOCK = 64
SLC_TOPK = 16
WINDOW = 512
NSA_KD = NSA_GROUPS * NSA_QK_DIM
NSA_VD = NSA_GROUPS * NSA_V_DIM

CONV_WIDTH = 3

LANES = 128
SUBLANES = 8
BF16_SUBLANES = 16
MXU_DIM = 256
VMEM_LIMIT = 56 * 1024 * 1024

MASK_NEG = -1e30
SUM_ROWS = BF16_SUBLANES
LOG2E = 1.4426950408889634
BISECT_ITERS = 32
BISECT_UNROLL = 4


def _params(*sem):
    return pltpu.CompilerParams(dimension_semantics=sem, vmem_limit_bytes=VMEM_LIMIT)


def _nt(a, b):
    return lax.dot_general(a, b, (((1,), (1,)), ((), ())), preferred_element_type=F32)


def _tn(a, b):
    return lax.dot_general(a, b, (((0,), (0,)), ((), ())), preferred_element_type=F32)


def _rms(x, g):
    return x * lax.rsqrt(jnp.mean(x * x, axis=-1, keepdims=True) + NORM_EPS) * g


def _rope_first_vreg(x0, cf, sa, sb, half):
    return (x0 * cf + pltpu.roll(x0, LANES - half, 1) * sa + pltpu.roll(x0, half, 1) * sb)


def _split3(x):
    hi = x.astype(BF16)
    r1 = x - hi.astype(F32)
    mid = r1.astype(BF16)
    lo = (r1 - mid.astype(F32)).astype(BF16)
    return hi, mid, lo


def _split3_bits(x):
    def trunc(v):
        bits = lax.bitcast_convert_type(v, jnp.uint32) & jnp.uint32(0xFFFF0000)
        return lax.bitcast_convert_type(bits, F32)

    hi = trunc(x)
    r1 = x - hi
    mid = trunc(r1)
    lo = r1 - mid
    return hi.astype(BF16), mid.astype(BF16), lo.astype(BF16)


def _online_softmax_step(s, pv_lhs, m_ref, acc_ref, cols):
    m_prev = m_ref[:, cols]
    m_new = jnp.maximum(m_prev, s.max(axis=0, keepdims=True))
    alpha = jnp.exp2(m_prev - m_new)
    p = jnp.exp2(s - m_new).astype(BF16)
    acc_ref[:, cols] = alpha * acc_ref[:, cols] + jnp.dot(pv_lhs, p, preferred_element_type=F32)
    m_ref[:, cols] = m_new


def _softmax_cols(z):
    mx = z.max(axis=0, keepdims=True)
    e = jnp.exp2(z - mx)
    den = e.sum(axis=0, keepdims=True)
    return e * jnp.where(mx > 0.5 * MASK_NEG, 1.0 / den, 0.0)


def _mm_body(*refs, has_gain, has_res):
    x_ref, w_ref = refs[0], refs[1]
    k = 2
    g_ref = r_ref = None
    if has_gain:
        g_ref = refs[k]
        k += 1
    if has_res:
        r_ref = refs[k]
        k += 1
    o_ref = refs[k]
    if has_gain:
        xn_ref = refs[k + 1]

        @pl.when(pl.program_id(1) == 0)
        def _():
            xn_ref[...] = _rms(x_ref[...].astype(F32), g_ref[...]).astype(BF16)

        a = xn_ref[...]
    else:
        a = x_ref[...].astype(BF16)
    acc = jnp.dot(a, w_ref[...], preferred_element_type=F32)
    if has_res:
        acc = acc + r_ref[...]
    o_ref[...] = acc.astype(o_ref.dtype)


def _mm(x, w, *, gain=None, res=None, out_dtype=F32, x_col_block=0, tm=512, tn=512):
    m = x.shape[0]
    k, n = w.shape
    tm = min(tm, m)
    tn = min(tn, n)
    assert m % tm == 0 and n % tn == 0 and x.shape[1] >= (x_col_block + 1) * k
    in_specs = [pl.BlockSpec((tm, k), lambda i, j: (i, x_col_block)),
                pl.BlockSpec((k, tn), lambda i, j: (0, j))]
    args = [x, w]
    scratch = []
    if gain is not None:
        in_specs.append(pl.BlockSpec((1, k), lambda i, j: (0, 0)))
        args.append(gain.reshape(1, k).astype(F32))
        scratch.append(pltpu.VMEM((tm, k), BF16))
    if res is not None:
        in_specs.append(pl.BlockSpec((tm, tn), lambda i, j: (i, j)))
        args.append(res)
    return pl.pallas_call(
        functools.partial(_mm_body, has_gain=gain is not None, has_res=res is not None),
        grid=(m // tm, n // tn),
        in_specs=in_specs,
        out_specs=pl.BlockSpec((tm, tn), lambda i, j: (i, j)),
        out_shape=jax.ShapeDtypeStruct((m, n), out_dtype),
        scratch_shapes=scratch,
        compiler_params=_params("arbitrary", "arbitrary"),
        name="mm",
    )(*args)


FFN_HALO = BF16_SUBLANES


def _ffn_body(h_ref, hp_ref, g_ref, wg_ref, wv_ref, cwg_ref, cwv_ref, cbg_ref, cbv_ref, wd_ref, og_ref,
              o_ref, xn_ref, ug_ref, uv_ref, acc_ref, *, tm, seq, norm_out):
    i = pl.program_id(0)
    j = pl.program_id(1)

    @pl.when(j == 0)
    def _():
        xn_ref[FFN_HALO:, :] = _rms(h_ref[...], g_ref[...]).astype(BF16)
        prev = _rms(hp_ref[...], g_ref[...])
        seq_start = (i * tm) % seq == 0
        xn_ref[:FFN_HALO, :] = jnp.where(seq_start, 0.0, prev).astype(BF16)
        acc_ref[...] = jnp.zeros_like(acc_ref)

    xn = xn_ref[...]
    ug_ref[...] = jnp.dot(xn, wg_ref[...], preferred_element_type=F32)
    uv_ref[...] = jnp.dot(xn, wv_ref[...], preferred_element_type=F32)

    def conv(u_ref, cw_ref, cb_ref):
        y = cb_ref[...]
        for t in range(CONV_WIDTH):
            off = FFN_HALO - (CONV_WIDTH - 1) + t
            y = y + cw_ref[t:t + 1, :] * u_ref[off:off + tm, :]
        return y

    yg = conv(ug_ref, cwg_ref, cbg_ref)
    yv = conv(uv_ref, cwv_ref, cbv_ref)
    a = (jax.nn.silu(yg) * yv).astype(BF16)
    acc_ref[...] += jnp.dot(a, wd_ref[...], preferred_element_type=F32)

    @pl.when(j == pl.num_programs(1) - 1)
    def _():
        y = h_ref[...] + acc_ref[...]
        o_ref[...] = _rms(y, og_ref[...]) if norm_out else y


def _ffn(h, seq, gain, w_up, conv_w, conv_b, w_down, *, out_gain=None, tm=512, tf=512):
    t, d = h.shape
    dff = w_down.shape[0]
    tm = min(tm, seq)
    assert seq % tm == 0 and dff % tf == 0 and tm % FFN_HALO == 0
    nf = dff // tf
    hb = tm // FFN_HALO
    conv_b2 = conv_b.reshape(1, 2 * dff)
    return pl.pallas_call(
        functools.partial(_ffn_body, tm=tm, seq=seq, norm_out=out_gain is not None),
        grid=(t // tm, nf),
        in_specs=[
            pl.BlockSpec((tm, d), lambda i, j: (i, 0)),
            pl.BlockSpec((FFN_HALO, d), lambda i, j: (jnp.maximum(i * hb - 1, 0), 0)),
            pl.BlockSpec((1, d), lambda i, j: (0, 0)),
            pl.BlockSpec((d, tf), lambda i, j: (0, j)),
            pl.BlockSpec((d, tf), lambda i, j: (0, nf + j)),
            pl.BlockSpec((CONV_WIDTH, tf), lambda i, j: (0, j)),
            pl.BlockSpec((CONV_WIDTH, tf), lambda i, j: (0, nf + j)),
            pl.BlockSpec((1, tf), lambda i, j: (0, j)),
            pl.BlockSpec((1, tf), lambda i, j: (0, nf + j)),
            pl.BlockSpec((tf, d), lambda i, j: (j, 0)),
            pl.BlockSpec((1, d), lambda i, j: (0, 0)),
        ],
        out_specs=pl.BlockSpec((tm, d), lambda i, j: (i, 0)),
        out_shape=jax.ShapeDtypeStruct((t, d), F32),
        scratch_shapes=[
            pltpu.VMEM((tm + FFN_HALO, d), BF16),
            pltpu.VMEM((tm + FFN_HALO, tf), F32),
            pltpu.VMEM((tm + FFN_HALO, tf), F32),
            pltpu.VMEM((tm, d), F32),
        ],
        compiler_params=_params("arbitrary", "arbitrary"),
        name="ffn",
    )(h, h, gain.reshape(1, d), w_up, w_up, conv_w, conv_w, conv_b2, conv_b2, w_down,
      (gain if out_gain is None else out_gain).reshape(1, d))


def _rope_cos_sin(positions, rot):
    inv = ROPE_THETA ** (-jnp.arange(0, rot, 2, dtype=F32) / rot)
    ang = positions.astype(F32)[..., None] * inv
    return jnp.cos(ang), jnp.sin(ang)


def _rope_lane_tables(positions, rot):
    c, s = _rope_cos_sin(positions, rot)
    half = rot // 2
    shp = c.shape[:-1]
    cf = jnp.concatenate([c, c, jnp.ones(shp + (LANES - rot,), F32)], -1)
    sa = jnp.concatenate([-s, jnp.zeros(shp + (LANES - half,), F32)], -1)
    sb = jnp.concatenate([jnp.zeros(shp + (half,), F32), s, jnp.zeros(shp + (LANES - rot,), F32)], -1)
    return jnp.concatenate([cf, sa, sb], -1)


def _rope_glue(x, cos, sin):
    half = cos.shape[-1]
    x1, x2 = x[..., :half], x[..., half:2 * half]
    return jnp.concatenate([x1 * cos - x2 * sin, x1 * sin + x2 * cos, x[..., 2 * half:]], -1)


DSA_QX = 384
DSA_QPAD = MXU_DIM
DSA_HEAD_GROUP = 16
DSA_CNT_ROWS = 4 * SUBLANES


def _dsa_body(q_ref, qi_ref, wi_ref, tab_ref, kidx_ref, kext_ref, ct_ref, wk_ref, wuv_ref, tri_ref, o_ref,
              qx_ref, qis_ref, sc_ref, m_ref, acc_ref, *, kc, n_keep):
    i = pl.program_id(1)
    q0 = i * Q_BLOCK
    nc = (q0 + Q_BLOCK + kc - 1) // kc
    scale = DSA_QK_DIM ** -0.5 * LOG2E
    tab = tab_ref[0]
    cf, sa, sb = tab[:, 0:LANES], tab[:, LANES:2 * LANES], tab[:, 2 * LANES:3 * LANES]
    cfi, sai, sbi = tab[:, 3 * LANES:4 * LANES], tab[:, 4 * LANES:5 * LANES], tab[:, 5 * LANES:6 * LANES]

    for h in range(DSA_HEADS):
        xh = q_ref[0, :, h * DSA_QPAD:(h + 1) * DSA_QPAD].astype(F32)
        r0 = _rope_first_vreg(xh[:, :LANES], cf, sa, sb, DSA_ROPE // 2)
        qh = jnp.concatenate([r0, xh[:, LANES:]], axis=1).astype(BF16)
        qx = jnp.dot(qh, wk_ref[h], preferred_element_type=F32) * scale
        qx_ref[h * Q_BLOCK:(h + 1) * Q_BLOCK, :] = qx.astype(BF16)
    for h in range(IDX_HEADS):
        xi = qi_ref[0, :, h * IDX_DIM:(h + 1) * IDX_DIM].astype(F32)
        qis_ref[h * Q_BLOCK:(h + 1) * Q_BLOCK, :] = _rope_first_vreg(xi, cfi, sai, sbi, IDX_ROPE // 2).astype(BF16)

    wi = wi_ref[0, 0]
    key_i = lax.broadcasted_iota(jnp.int32, (kc, Q_BLOCK), 0)
    t_l = q0 + lax.broadcasted_iota(jnp.int32, (kc, Q_BLOCK), 1)

    def score_chunk(c, carry):
        lo, hi = carry
        k0 = pl.multiple_of(c * kc, kc)
        d = _nt(kidx_ref[0, pl.ds(k0, kc), :], qis_ref[...])
        d = jnp.maximum(d, 0.0)
        sc = d[:, 0:Q_BLOCK] * wi[0:1, :]
        for h in range(1, IDX_HEADS):
            sc = sc + d[:, h * Q_BLOCK:(h + 1) * Q_BLOCK] * wi[h:h + 1, :]
        sc = sc + 0.0
        causal = (k0 + key_i) <= t_l
        sc_ref[c] = jnp.where(causal, sc, -jnp.inf)
        lo = jnp.minimum(lo, jnp.where(causal, sc, jnp.inf).min(axis=0, keepdims=True))
        hi = jnp.maximum(hi, jnp.where(causal, sc, -jnp.inf).max(axis=0, keepdims=True))
        return lo, hi

    lo, hi = lax.fori_loop(0, nc, score_chunk,
                           (jnp.full((1, Q_BLOCK), jnp.inf, F32), jnp.full((1, Q_BLOCK), -jnp.inf, F32)))

    def bisect_more(carry):
        it, _, _, n_lo = carry
        return (it < BISECT_ITERS) & (jnp.max(n_lo) > n_keep)

    def bisect(carry):
        it, lo, hi, n_lo = carry

        def count_chunk_at(mid):
            def count_chunk(c, cnt):
                ge = jnp.where(sc_ref[c] >= mid, 1.0, 0.0)
                return cnt + ge.reshape(kc // DSA_CNT_ROWS, DSA_CNT_ROWS, Q_BLOCK).sum(axis=0)
            return count_chunk

        for _ in range(BISECT_UNROLL):
            mid = lo + (hi - lo) * 0.5
            cnt = lax.fori_loop(0, nc, count_chunk_at(mid), jnp.zeros((DSA_CNT_ROWS, Q_BLOCK), F32))
            cnt = cnt.sum(axis=0, keepdims=True)
            ge = cnt >= n_keep
            lo, hi, n_lo = jnp.where(ge, mid, lo), jnp.where(ge, hi, mid), jnp.where(ge, cnt, n_lo)
        return it + BISECT_UNROLL, lo, hi, n_lo

    n_causal = (q0 + 1 + lax.broadcasted_iota(jnp.int32, (1, Q_BLOCK), 1)).astype(F32)
    it_end, lo, hi, n_lo = lax.while_loop(bisect_more, bisect, (0, lo, hi, n_causal))

    @pl.when(it_end >= BISECT_ITERS)
    def _():
        def min_chunk(c, v):
            x = sc_ref[c]
            return jnp.minimum(v, jnp.where(x >= lo, x, jnp.inf).min(axis=0, keepdims=True))

        vth = lax.fori_loop(0, nc, min_chunk, jnp.full((1, Q_BLOCK), jnp.inf, F32))

        def gt_chunk(c, n):
            return n + jnp.where(sc_ref[c] > vth, 1.0, 0.0).sum(axis=0, keepdims=True)

        room = n_keep - lax.fori_loop(0, nc, gt_chunk, jnp.zeros((1, Q_BLOCK), F32))
        over = n_lo > n_keep

        def drop_chunk(c, seen):
            x = sc_ref[c]
            eq = jnp.where(x == vth, 1.0, 0.0)
            rank = seen + jnp.dot(tri_ref[...], eq.astype(BF16), preferred_element_type=F32)
            sc_ref[c] = jnp.where((eq > 0.5) & (rank > room) & over, -jnp.inf, x)
            return seen + eq.sum(axis=0, keepdims=True)

        lax.fori_loop(0, nc, drop_chunk, jnp.zeros((1, Q_BLOCK), F32))

    def bias_chunk(c, _):
        sc_ref[c] = jnp.where(sc_ref[c] >= lo, 0.0, MASK_NEG)
        return 0

    lax.fori_loop(0, nc, bias_chunk, 0)

    m_ref[...] = jnp.full_like(m_ref, MASK_NEG)
    acc_ref[...] = jnp.zeros_like(acc_ref)
    gl = DSA_HEAD_GROUP * Q_BLOCK

    def attn_chunk(c, _):
        k0 = pl.multiple_of(c * kc, kc)
        kx = kext_ref[0, pl.ds(k0, kc), :]
        ct = ct_ref[0, c]
        bias = jnp.tile(sc_ref[c], (1, DSA_HEAD_GROUP))
        n_slab = DSA_HEADS // DSA_HEAD_GROUP
        s = _nt(kx, qx_ref[0:gl, :]) + bias
        for g in range(n_slab):
            s_next = _nt(kx, qx_ref[(g + 1) * gl:(g + 2) * gl, :]) + bias if g + 1 < n_slab else None
            _online_softmax_step(s, ct, m_ref, acc_ref, slice(g * gl, (g + 1) * gl))
            s = s_next
        return 0

    lax.fori_loop(0, nc, attn_chunk, 0)

    for h in range(DSA_HEADS):
        cols = slice(h * Q_BLOCK, (h + 1) * Q_BLOCK)
        o_lat = (acc_ref[:DSA_KV_RANK, cols] / acc_ref[DSA_KV_RANK:DSA_KV_RANK + 1, cols]).astype(BF16)
        o_ref[0, :, h * DSA_V_DIM:(h + 1) * DSA_V_DIM] = _tn(o_lat, wuv_ref[h]).astype(o_ref.dtype)


def _dsa_attention(qall, wi_l, tabs, kidx, kext, ct, wk, wuv, *, kc):
    b, s, _ = qall.shape
    assert s % kc == 0 and kc % Q_BLOCK == 0
    n_keep = min(DSA_TOPK_MAX, s // 4)
    nq = DSA_HEADS * DSA_QPAD
    ni = IDX_HEADS * IDX_DIM
    assert nq % ni == 0
    nrow = DSA_HEADS * Q_BLOCK
    return pl.pallas_call(
        functools.partial(_dsa_body, kc=kc, n_keep=float(n_keep)),
        grid=(b, s // Q_BLOCK),
        in_specs=[
            pl.BlockSpec((1, Q_BLOCK, nq), lambda bb, i: (bb, i, 0)),
            pl.BlockSpec((1, Q_BLOCK, ni), lambda bb, i: (bb, i, nq // ni)),
            pl.BlockSpec((1, 1, IDX_HEADS, Q_BLOCK), lambda bb, i: (bb, i, 0, 0)),
            pl.BlockSpec((1, Q_BLOCK, 6 * LANES), lambda bb, i: (bb, i, 0)),
            pl.BlockSpec((1, s, IDX_DIM), lambda bb, i: (bb, 0, 0)),
            pl.BlockSpec((1, s, DSA_QX), lambda bb, i: (bb, 0, 0)),
            pl.BlockSpec((1, s // kc, DSA_KV_RANK + SUM_ROWS, kc), lambda bb, i: (bb, 0, 0, 0)),
            pl.BlockSpec((DSA_HEADS, DSA_QPAD, DSA_QX), lambda bb, i: (0, 0, 0)),
            pl.BlockSpec((DSA_HEADS, DSA_KV_RANK, DSA_V_DIM), lambda bb, i: (0, 0, 0)),
            pl.BlockSpec((kc, kc), lambda bb, i: (0, 0)),
        ],
        out_specs=pl.BlockSpec((1, Q_BLOCK, DSA_HEADS * DSA_V_DIM), lambda bb, i: (bb, i, 0)),
        out_shape=jax.ShapeDtypeStruct((b, s, DSA_HEADS * DSA_V_DIM), BF16),
        scratch_shapes=[
            pltpu.VMEM((nrow, DSA_QX), BF16),
            pltpu.VMEM((IDX_HEADS * Q_BLOCK, IDX_DIM), BF16),
            pltpu.VMEM((s // kc, kc, Q_BLOCK), F32),
            pltpu.VMEM((1, nrow), F32),
            pltpu.VMEM((DSA_KV_RANK + SUM_ROWS, nrow), F32),
        ],
        compiler_params=_params("arbitrary", "arbitrary"),
        name="dsa_attn",
    )(qall, qall, wi_l, tabs, kidx, kext, ct, wk, wuv, jnp.tril(jnp.ones((kc, kc), BF16)))


DSA_PROJ_W = (DSA_Q_RANK, DSA_KV_RANK, LANES, LANES, LANES)


def _dsa_kprep_body(ckv_ref, kpe_ref, kid_ref, wid_ref, tab_ref, kvn_ref, lng_ref, lnb_ref,
                    kext_ref, kidx_ref, ct_ref, wi_ref, *, tm):
    tab = tab_ref[...]
    cf, sa, sb = tab[:, 0:LANES], tab[:, LANES:2 * LANES], tab[:, 2 * LANES:3 * LANES]
    cfi, sai, sbi = tab[:, 3 * LANES:4 * LANES], tab[:, 4 * LANES:5 * LANES], tab[:, 5 * LANES:6 * LANES]
    c_n = _rms(ckv_ref[...], kvn_ref[...])
    kext_ref[:, :DSA_KV_RANK] = c_n.astype(BF16)
    kext_ref[:, DSA_KV_RANK:] = _rope_first_vreg(kpe_ref[...], cf, sa, sb, DSA_ROPE // 2).astype(BF16)
    ct_ref[0, 0, :DSA_KV_RANK, :] = c_n.T.astype(BF16)
    ct_ref[0, 0, DSA_KV_RANK:, :] = jnp.ones((SUM_ROWS, tm), BF16)
    x = kid_ref[...]
    mu = jnp.mean(x, axis=-1, keepdims=True)
    var = jnp.mean(jnp.square(x - mu), axis=-1, keepdims=True)
    k_ln = (x - mu) * lax.rsqrt(var + NORM_EPS) * lng_ref[...] + lnb_ref[...]
    kidx_ref[...] = _rope_first_vreg(k_ln, cfi, sai, sbi, IDX_ROPE // 2).astype(BF16)
    w = wid_ref[...] * (IDX_HEADS ** -0.5 * IDX_DIM ** -0.5)
    for r in range(tm // Q_BLOCK):
        wi_ref[0, r] = w[r * Q_BLOCK:(r + 1) * Q_BLOCK, :].T[:IDX_HEADS, :]


def _dsa_kprep(proj, tabs, kv_norm, ln_g, ln_b, bsz, seq, kc):
    t = proj.shape[0]
    tm = kc
    nt = seq // tm
    col = [int(c) for c in np.cumsum((0,) + DSA_PROJ_W)]
    assert all(c % w == 0 for c, w in zip(col[1:-1], DSA_PROJ_W[1:]))

    def row(v):
        return v.reshape(1, -1)

    return pl.pallas_call(
        functools.partial(_dsa_kprep_body, tm=tm),
        grid=(t // tm,),
        in_specs=[
            pl.BlockSpec((tm, DSA_KV_RANK), lambda i: (i, col[1] // DSA_KV_RANK)),
            pl.BlockSpec((tm, LANES), lambda i: (i, col[2] // LANES)),
            pl.BlockSpec((tm, LANES), lambda i: (i, col[3] // LANES)),
            pl.BlockSpec((tm, LANES), lambda i: (i, col[4] // LANES)),
            pl.BlockSpec((tm, 6 * LANES), lambda i: (i, 0)),
            pl.BlockSpec((1, DSA_KV_RANK), lambda i: (0, 0)),
            pl.BlockSpec((1, IDX_DIM), lambda i: (0, 0)),
            pl.BlockSpec((1, IDX_DIM), lambda i: (0, 0)),
        ],
        out_specs=[
            pl.BlockSpec((tm, DSA_QX), lambda i: (i, 0)),
            pl.BlockSpec((tm, IDX_DIM), lambda i: (i, 0)),
            pl.BlockSpec((1, 1, DSA_KV_RANK + SUM_ROWS, tm), lambda i: (i // nt, i % nt, 0, 0)),
            pl.BlockSpec((1, tm // Q_BLOCK, IDX_HEADS, Q_BLOCK), lambda i: (i // nt, i % nt, 0, 0)),
        ],
        out_shape=[
            jax.ShapeDtypeStruct((t, DSA_QX), BF16),
            jax.ShapeDtypeStruct((t, IDX_DIM), BF16),
            jax.ShapeDtypeStruct((bsz, nt, DSA_KV_RANK + SUM_ROWS, tm), BF16),
            jax.ShapeDtypeStruct((bsz, seq // Q_BLOCK, IDX_HEADS, Q_BLOCK), F32),
        ],
        compiler_params=_params("arbitrary"),
        name="dsa_kprep",
    )(proj, proj, proj, proj, tabs, row(kv_norm), row(ln_g), row(ln_b))


def _dsa_mixer(h2, bsz, seq, positions, attn_norm, w_in, q_norm, kv_norm, idx_ln_g, idx_ln_b,
               w_qup, w_uk, w_uv, w_out, *, kc=512):
    t = h2.shape[0]
    kc = min(kc, seq)
    sizes = (DSA_Q_RANK, DSA_KV_RANK, DSA_ROPE, IDX_DIM, IDX_HEADS)
    cuts = np.cumsum((0,) + sizes)
    w_in_p = jnp.concatenate(
        [jnp.pad(w_in[:, int(a):int(b)], ((0, 0), (0, wp - (int(b) - int(a)))))
         for a, b, wp in zip(cuts[:-1], cuts[1:], DSA_PROJ_W)], axis=1).astype(BF16)
    proj = _mm(h2, w_in_p, gain=attn_norm, tm=1024, tn=w_in_p.shape[1])

    nq = DSA_HEADS * DSA_QK_DIM
    w_q = w_qup[:, :nq].reshape(DSA_Q_RANK, DSA_HEADS, DSA_QK_DIM)
    w_q = jnp.pad(w_q, ((0, 0), (0, 0), (0, DSA_QPAD - DSA_QK_DIM))).reshape(DSA_Q_RANK, DSA_HEADS * DSA_QPAD)
    w_qp = jnp.concatenate([w_q, w_qup[:, nq:]], axis=1).astype(BF16)
    qall = _mm(proj, w_qp, gain=q_norm, out_dtype=BF16, tn=w_qp.shape[1])

    pos = positions.reshape(t)
    tabs = jnp.concatenate([_rope_lane_tables(pos, DSA_ROPE), _rope_lane_tables(pos, IDX_ROPE)], -1)
    kext, kidx, ct, wi = _dsa_kprep(proj, tabs, kv_norm, idx_ln_g, idx_ln_b, bsz, seq, kc)

    wk = jnp.zeros((DSA_HEADS, DSA_QPAD, DSA_QX), F32)
    wk = wk.at[:, DSA_ROPE:DSA_QK_DIM, :DSA_KV_RANK].set(jnp.swapaxes(w_uk, 1, 2))
    eye = jnp.eye(DSA_ROPE, dtype=F32)
    wk = wk.at[:, :DSA_ROPE, DSA_KV_RANK:DSA_KV_RANK + DSA_ROPE].set(jnp.broadcast_to(eye, (DSA_HEADS,) + eye.shape))

    o = _dsa_attention(
        qall.reshape(bsz, seq, -1), wi, tabs.reshape(bsz, seq, -1),
        kidx.reshape(bsz, seq, IDX_DIM), kext.reshape(bsz, seq, DSA_QX), ct,
        wk.astype(BF16), w_uv.astype(BF16), kc=kc)
    return _mm(o.reshape(t, -1), w_out.astype(BF16), res=h2, tm=1024)


FOX_KX = MXU_DIM
FOX_HEAD_PAIR = 2


def _fox_body(q_ref, k_ref, v_ref, c3_ref, e_ref, o_ref, qx_ref, kx_ref, vt_ref, m_ref, acc_ref, *, tq, kc):
    i = pl.program_id(2)
    q0 = i * tq
    n_full = q0 // kc
    n_diag = tq // kc

    @pl.when(i == 0)
    def _():
        for a in range(FOX_HEAD_PAIR):
            kx_ref[a, :, :FOX_DIM] = k_ref[0, :, a * FOX_DIM:(a + 1) * FOX_DIM]
            kx_ref[a, :, FOX_DIM:] = _tn(c3_ref[0, a], e_ref[...]).astype(BF16)
            for c in range(vt_ref.shape[1]):
                vc = v_ref[0, c * kc:(c + 1) * kc, a * FOX_DIM:(a + 1) * FOX_DIM]
                vt_ref[a, c, :FOX_DIM, :] = vc.astype(F32).T.astype(BF16)
                vt_ref[a, c, FOX_DIM:, :] = jnp.ones((SUM_ROWS, kc), BF16)

    lane = lax.broadcasted_iota(jnp.int32, (tq, FOX_KX - FOX_DIM), 1)
    for a in range(FOX_HEAD_PAIR):
        qx_ref[a, :, :FOX_DIM] = q_ref[0, :, a * FOX_DIM:(a + 1) * FOX_DIM]
        qx_ref[a, :, FOX_DIM:] = jnp.where(lane < 3, 1.0, 0.0).astype(BF16)
    m_ref[...] = jnp.full_like(m_ref, MASK_NEG)
    acc_ref[...] = jnp.zeros_like(acc_ref)
    key_i = lax.broadcasted_iota(jnp.int32, (kc, tq), 0)
    t_l = q0 + lax.broadcasted_iota(jnp.int32, (kc, tq), 1)

    def step(c, masked):
        k0 = pl.multiple_of(c * kc, kc)

        def logits(a):
            s = _nt(kx_ref[a, pl.ds(k0, kc), :], qx_ref[a])
            return jnp.where((k0 + key_i) <= t_l, s, MASK_NEG) if masked else s

        s = logits(0)
        for a in range(FOX_HEAD_PAIR):
            s_next = logits(a + 1) if a + 1 < FOX_HEAD_PAIR else None
            _online_softmax_step(s, vt_ref[a, c], m_ref, acc_ref, slice(a * tq, (a + 1) * tq))
            s = s_next

    def full_chunk(c, _):
        step(c, False)
        return 0

    def diag_chunk(c, _):
        step(c, True)
        return 0

    lax.fori_loop(0, n_full, full_chunk, 0)
    lax.fori_loop(n_full, n_full + n_diag, diag_chunk, 0)
    for a in range(FOX_HEAD_PAIR):
        cols = slice(a * tq, (a + 1) * tq)
        o_a = acc_ref[:FOX_DIM, cols] / acc_ref[FOX_DIM:FOX_DIM + 1, cols]
        o_ref[0, :, a * FOX_DIM:(a + 1) * FOX_DIM] = o_a.T.astype(o_ref.dtype)


def _fox_attention(qkv, c3, *, tq, kc):
    b, s, _ = qkv.shape
    hp = FOX_HEAD_PAIR
    npair = FOX_HEADS // hp
    assert s % tq == 0 and tq % kc == 0 and FOX_HEADS % hp == 0
    e = jnp.eye(SUBLANES, FOX_KX - FOX_DIM, dtype=BF16)
    return pl.pallas_call(
        functools.partial(_fox_body, tq=tq, kc=kc),
        grid=(b, npair, s // tq),
        in_specs=[
            pl.BlockSpec((1, tq, hp * FOX_DIM), lambda bb, h, i: (bb, i, h)),
            pl.BlockSpec((1, s, hp * FOX_DIM), lambda bb, h, i: (bb, 0, npair + h)),
            pl.BlockSpec((1, s, hp * FOX_DIM), lambda bb, h, i: (bb, 0, 2 * npair + h)),
            pl.BlockSpec((1, hp, SUBLANES, s), lambda bb, h, i: (bb, h, 0, 0)),
            pl.BlockSpec((SUBLANES, FOX_KX - FOX_DIM), lambda bb, h, i: (0, 0)),
        ],
        out_specs=pl.BlockSpec((1, tq, hp * FOX_DIM), lambda bb, h, i: (bb, i, h)),
        out_shape=jax.ShapeDtypeStruct((b, s, FOX_HEADS * FOX_DIM), BF16),
        scratch_shapes=[pltpu.VMEM((hp, tq, FOX_KX), BF16), pltpu.VMEM((hp, s, FOX_KX), BF16),
                        pltpu.VMEM((hp, s // kc, FOX_DIM + SUM_ROWS, kc), BF16),
                        pltpu.VMEM((1, hp * tq), F32), pltpu.VMEM((FOX_DIM + SUM_ROWS, hp * tq), F32)],
        compiler_params=_params("arbitrary", "arbitrary", "arbitrary"),
        name="fox_attn",
    )(qkv, qkv, qkv, c3, e)


def _fox_mixer(h2, bsz, seq, attn_norm, w_in, b_f, w_out, *, tq=1024, kc=1024):
    t = h2.shape[0]
    hd = FOX_HEADS * FOX_DIM
    tq = min(tq, seq)
    kc = min(kc, tq)
    scale = FOX_DIM ** -0.5 * LOG2E
    w_qkv = jnp.concatenate([w_in[:, :hd] * scale, w_in[:, hd:3 * hd]], axis=1).astype(BF16)
    qkv = _mm(h2, w_qkv, gain=attn_norm, out_dtype=BF16, tm=1024, tn=1024)
    w_f = jnp.pad(w_in[:, 3 * hd:], ((0, 0), (0, LANES - FOX_HEADS))).astype(BF16)
    f_logit = _mm(h2, w_f, gain=attn_norm)[:, :FOX_HEADS]
    log_f = jax.nn.log_sigmoid(f_logit + b_f)
    cum = jnp.cumsum(log_f.reshape(bsz, seq, FOX_HEADS), axis=1)
    c3 = jnp.stack(_split3_bits(-cum * LOG2E), axis=0).transpose(1, 3, 0, 2)
    c3 = jnp.pad(c3, ((0, 0), (0, 0), (0, SUBLANES - 3), (0, 0)))
    o = _fox_attention(qkv.reshape(bsz, seq, 3 * hd), c3, tq=tq, kc=kc)
    return _mm(o.reshape(t, hd), w_out.astype(BF16), res=h2, tm=1024)


NSA_HEAD_GROUP = 4


def _cmp_body(x_ref, pe_ref, w1_ref, b1_ref, w2_ref, o_ref, *, half):
    x = x_ref[0, 0]
    w_lo = w1_ref[:half, :]
    w_hi = w1_ref[half:, :]
    a = jnp.dot(x, w_lo, preferred_element_type=F32)
    bnext = jnp.dot(x, w_hi, preferred_element_type=F32)
    n = x.shape[0]
    bnext = pltpu.roll(bnext, n - 1, 0)
    pe = pe_ref[...]
    pe_b = (jnp.dot(pe[:, :half], w_lo, preferred_element_type=F32)
            + jnp.dot(pe[:, half:], w_hi, preferred_element_type=F32))[0:1, :]
    hid = jax.nn.gelu(a + bnext + pe_b + b1_ref[...])
    o_ref[0, 0] = jnp.dot(hid.astype(BF16), w2_ref[...], preferred_element_type=F32).astype(o_ref.dtype)


def _compress(x, pe, w1, b1, w2):
    b, g, n, kd = x.shape
    dout = w2.shape[1]
    pe8 = jnp.broadcast_to(pe.reshape(1, 2 * kd), (8, 2 * kd)).astype(BF16)
    return pl.pallas_call(
        functools.partial(_cmp_body, half=kd),
        grid=(b, g),
        in_specs=[
            pl.BlockSpec((1, 1, n, kd), lambda bb, gg: (bb, gg, 0, 0)),
            pl.BlockSpec((8, 2 * kd), lambda bb, gg: (0, 0)),
            pl.BlockSpec((2 * kd, CMP_HIDDEN), lambda bb, gg: (0, 0)),
            pl.BlockSpec((1, CMP_HIDDEN), lambda bb, gg: (0, 0)),
            pl.BlockSpec((CMP_HIDDEN, dout), lambda bb, gg: (0, 0)),
        ],
        out_specs=pl.BlockSpec((1, 1, n, dout), lambda bb, gg: (bb, gg, 0, 0)),
        out_shape=jax.ShapeDtypeStruct((b, g, n, dout), BF16),
        compiler_params=_params("arbitrary", "arbitrary"),
        name="nsa_compress",
    )(x, pe8, w1.astype(BF16), b1.reshape(1, CMP_HIDDEN), w2.astype(BF16))


def _nsa_body(q_ref, tab_ref, gate_ref, kc_ref, vct_ref, ks_ref, vs_ref, kw_ref, vw_ref, ov_ref, exp_ref,
              o_ref, qs_ref, vst_ref, vwt_ref, m_ref, acc_ref, *, kc, n_cmp, n_slc, n_sel):
    i = pl.program_id(2)

    @pl.when(i == 0)
    def _():
        for c in range(vst_ref.shape[0]):
            vst_ref[c, :NSA_V_DIM, :] = vs_ref[0, c * kc:(c + 1) * kc, :].T.astype(BF16)
            vst_ref[c, NSA_V_DIM:, :] = jnp.ones((SUM_ROWS, kc), BF16)
        npad = WINDOW // Q_BLOCK
        for c in range(vwt_ref.shape[0]):
            if c < npad:
                vwt_ref[c] = jnp.zeros((NSA_V_DIM, Q_BLOCK), BF16)
            else:
                vwt_ref[c] = vw_ref[0, (c - npad) * Q_BLOCK:(c - npad + 1) * Q_BLOCK, :].T.astype(BF16)

    q0 = i * Q_BLOCK
    nc = (q0 + Q_BLOCK + kc - 1) // kc
    jh = NSA_HPG
    scale = NSA_QK_DIM ** -0.5 * LOG2E
    tab = tab_ref[0]
    cf, sa, sb = tab[:, 0:LANES], tab[:, LANES:2 * LANES], tab[:, 2 * LANES:3 * LANES]

    for j in range(jh):
        xj = q_ref[0, :, j * NSA_QK_DIM:(j + 1) * NSA_QK_DIM].astype(F32)
        r0 = _rope_first_vreg(xj[:, :LANES], cf, sa, sb, NSA_ROPE // 2)
        qs_ref[j * Q_BLOCK:(j + 1) * Q_BLOCK, :] = (jnp.concatenate([r0, xj[:, LANES:]], axis=1) * scale).astype(BF16)
    qs = qs_ref[...]

    n_id = lax.broadcasted_iota(jnp.int32, (n_cmp, Q_BLOCK), 0)
    t_c = q0 + lax.broadcasted_iota(jnp.int32, (n_cmp, Q_BLOCK), 1)
    cbias = jnp.where((n_id * CMP_STRIDE + (CMP_BLOCK - 1)) <= t_c, 0.0, MASK_NEG)
    pc = _softmax_cols(_nt(kc_ref[0, 0], qs) + jnp.tile(cbias, (1, jh)))
    o_c = jnp.dot(vct_ref[0, 0], pc.astype(BF16), preferred_element_type=F32)

    pcs = pc[:, 0:Q_BLOCK]
    for j in range(1, jh):
        pcs = pcs + pc[:, j * Q_BLOCK:(j + 1) * Q_BLOCK]
    ov = ov_ref[...]
    imp = sum(jnp.dot(ov, term, preferred_element_type=F32) for term in _split3(pcs))

    blk = lax.broadcasted_iota(jnp.int32, (n_slc, Q_BLOCK), 0)
    t_b = q0 + lax.broadcasted_iota(jnp.int32, (n_slc, Q_BLOCK), 1)
    cur = lax.shift_right_logical(t_b, int(np.log2(SLC_BLOCK)))
    causal_blk = blk * SLC_BLOCK <= t_b
    forced = (blk == 0) | (blk == cur) | (blk == cur - 1)
    val = jnp.where(causal_blk, jnp.where(forced, jnp.inf, imp), -jnp.inf)
    rank = jnp.zeros((n_slc, Q_BLOCK), F32)
    for mp in range(n_slc):
        vrow = val[mp:mp + 1, :]
        before = (vrow > val) | ((vrow == val) & (blk > mp))
        rank = rank + jnp.where(before, 1.0, 0.0)
    sel = jnp.where((rank < n_sel) & causal_blk, 1.0, 0.0)
    sel_p = jnp.concatenate([sel, jnp.zeros((LANES - n_slc, Q_BLOCK), F32)], axis=0).astype(BF16)

    m_ref[...] = jnp.full_like(m_ref, MASK_NEG)
    acc_ref[...] = jnp.zeros_like(acc_ref)
    key_i = lax.broadcasted_iota(jnp.int32, (kc, Q_BLOCK), 0)
    t_k = q0 + lax.broadcasted_iota(jnp.int32, (kc, Q_BLOCK), 1)
    gl = NSA_HEAD_GROUP * Q_BLOCK

    def sel_chunk(c, _):
        k0 = pl.multiple_of(c * kc, kc)
        hit = jnp.dot(exp_ref[pl.ds(k0, kc), :], sel_p, preferred_element_type=F32)
        bias = jnp.where((hit > 0.5) & ((k0 + key_i) <= t_k), 0.0, MASK_NEG)
        bias = jnp.tile(bias, (1, NSA_HEAD_GROUP))
        ksc = ks_ref[0, 0, pl.ds(k0, kc), :]
        vt = vst_ref[c]
        n_slab = jh // NSA_HEAD_GROUP
        s = _nt(ksc, qs_ref[0:gl, :]) + bias
        for hg in range(n_slab):
            s_next = _nt(ksc, qs_ref[(hg + 1) * gl:(hg + 2) * gl, :]) + bias if hg + 1 < n_slab else None
            _online_softmax_step(s, vt, m_ref, acc_ref, slice(hg * gl, (hg + 1) * gl))
            s = s_next
        return 0

    lax.fori_loop(0, nc, sel_chunk, 0)
    o_s = acc_ref[:NSA_V_DIM, :] / acc_ref[NSA_V_DIM:NSA_V_DIM + 1, :]

    wl = WINDOW + Q_BLOCK
    kstart = pl.multiple_of(q0, Q_BLOCK)
    s_pos = q0 - WINDOW + lax.broadcasted_iota(jnp.int32, (wl, Q_BLOCK), 0)
    t_w = q0 + lax.broadcasted_iota(jnp.int32, (wl, Q_BLOCK), 1)
    wbias = jnp.where((s_pos >= 0) & (s_pos <= t_w) & (s_pos > t_w - WINDOW), 0.0, MASK_NEG)
    pw = _softmax_cols(_nt(kw_ref[0, 0, pl.ds(kstart, wl), :], qs) + jnp.tile(wbias, (1, jh))).astype(BF16)
    o_w = jnp.dot(vwt_ref[i], pw[0:Q_BLOCK, :], preferred_element_type=F32)
    for c in range(1, wl // Q_BLOCK):
        o_w = o_w + jnp.dot(vwt_ref[i + c], pw[c * Q_BLOCK:(c + 1) * Q_BLOCK, :],
                            preferred_element_type=F32)

    g = gate_ref[0, 0]
    for j in range(jh):
        cj = slice(j * Q_BLOCK, (j + 1) * Q_BLOCK)
        out = (g[j:j + 1, :] * o_c[:, cj] + g[jh + j:jh + j + 1, :] * o_s[:, cj]
               + g[2 * jh + j:2 * jh + j + 1, :] * o_w[:, cj])
        o_ref[0, :, j * NSA_V_DIM:(j + 1) * NSA_V_DIM] = out.T.astype(o_ref.dtype)


def _nsa_attention(q, tabs, gates, kcmp, vcmp_t, ks, kw, rest, vs_col, vw_col, *, kc):
    b, s, _ = q.shape
    g = NSA_GROUPS
    assert s % kc == 0 and kc % SLC_BLOCK == 0
    n_cmp = kcmp.shape[2]
    n_slc = s // SLC_BLOCK
    assert n_slc <= LANES
    n_sel = min(SLC_TOPK, n_slc)
    cmp_start = np.arange(n_cmp) * CMP_STRIDE
    slc_start = np.arange(n_slc) * SLC_BLOCK
    ov = ((cmp_start[None, :] < slc_start[:, None] + SLC_BLOCK)
          & (cmp_start[None, :] + CMP_BLOCK > slc_start[:, None])).astype(np.float32)
    expand = (np.arange(s)[:, None] // SLC_BLOCK == np.arange(LANES)[None, :]).astype(np.float32)
    qw = NSA_HPG * NSA_QK_DIM
    ow = NSA_HPG * NSA_V_DIM
    nrow = NSA_HPG * Q_BLOCK
    wl = s + WINDOW
    return pl.pallas_call(
        functools.partial(_nsa_body, kc=kc, n_cmp=n_cmp, n_slc=n_slc, n_sel=float(n_sel)),
        grid=(b, g, s // Q_BLOCK),
        in_specs=[
            pl.BlockSpec((1, Q_BLOCK, qw), lambda bb, gg, i: (bb, i, gg)),
            pl.BlockSpec((1, Q_BLOCK, 3 * LANES), lambda bb, gg, i: (bb, i, 0)),
            pl.BlockSpec((1, 1, 3 * NSA_HPG, Q_BLOCK), lambda bb, gg, i: (bb, gg, 0, i)),
            pl.BlockSpec((1, 1, n_cmp, NSA_QK_DIM), lambda bb, gg, i: (bb, gg, 0, 0)),
            pl.BlockSpec((1, 1, NSA_V_DIM, n_cmp), lambda bb, gg, i: (bb, gg, 0, 0)),
            pl.BlockSpec((1, 1, s, NSA_QK_DIM), lambda bb, gg, i: (bb, gg, 0, 0)),
            pl.BlockSpec((1, s, NSA_V_DIM), lambda bb, gg, i: (bb, 0, vs_col + gg)),
            pl.BlockSpec((1, 1, wl, NSA_QK_DIM), lambda bb, gg, i: (bb, gg, 0, 0)),
            pl.BlockSpec((1, s, NSA_V_DIM), lambda bb, gg, i: (bb, 0, vw_col + gg)),
            pl.BlockSpec((n_slc, n_cmp), lambda bb, gg, i: (0, 0)),
            pl.BlockSpec((s, LANES), lambda bb, gg, i: (0, 0)),
        ],
        out_specs=pl.BlockSpec((1, Q_BLOCK, ow), lambda bb, gg, i: (bb, i, gg)),
        out_shape=jax.ShapeDtypeStruct((b, s, NSA_HEADS * NSA_V_DIM), BF16),
        scratch_shapes=[
            pltpu.VMEM((nrow, NSA_QK_DIM), BF16),
            pltpu.VMEM((s // kc, NSA_V_DIM + SUM_ROWS, kc), BF16),
            pltpu.VMEM((wl // Q_BLOCK, NSA_V_DIM, Q_BLOCK), BF16),
            pltpu.VMEM((1, nrow), F32),
            pltpu.VMEM((NSA_V_DIM + SUM_ROWS, nrow), F32),
        ],
        compiler_params=_params("arbitrary", "arbitrary", "arbitrary"),
        name="nsa_attn",
    )(q, tabs, gates, kcmp, vcmp_t, ks, rest, kw, rest, jnp.asarray(ov, BF16), jnp.asarray(expand, BF16))


def _nsa_mixer(h2, bsz, seq, positions, attn_norm, w_in, k_pe, k_w1, k_b1, k_w2, v_pe, v_w1, v_b1, v_w2, w_out,
               *, kc=1024):
    t = h2.shape[0]
    kc = min(kc, seq)
    g, jh, dk, dv = NSA_GROUPS, NSA_HPG, NSA_QK_DIM, NSA_V_DIM
    nq = NSA_HEADS * dk
    n_rest = w_in.shape[1] - nq
    n_rest_p = -(-n_rest // 512) * 512
    q = _mm(h2, w_in[:, :nq].astype(BF16), gain=attn_norm, out_dtype=BF16, tm=1024, tn=1024)
    w_rest = jnp.pad(w_in[:, nq:], ((0, 0), (0, n_rest_p - n_rest))).astype(BF16)
    rest = _mm(h2, w_rest, gain=attn_norm, tm=1024, tn=1024)
    offs = np.cumsum([0, NSA_KD, NSA_VD, NSA_KD, NSA_VD, NSA_KD, NSA_VD, 3 * NSA_HEADS])
    k_c, v_c, k_s, v_s, k_w, v_w, g_logit = [rest[:, int(a):int(b)] for a, b in zip(offs[:-1], offs[1:])]

    pos = positions.reshape(t)
    cos, sin = _rope_cos_sin(pos, NSA_ROPE)

    def rope_k(k):
        return _rope_glue(k.reshape(t, g, dk), cos[:, None, :], sin[:, None, :])

    def per_group(x, d):
        return x.reshape(bsz, seq, g, d).transpose(0, 2, 1, 3).astype(BF16)

    def chunks(x, d):
        x = x.reshape(bsz, seq // CMP_STRIDE, CMP_STRIDE, g, d).transpose(0, 3, 1, 2, 4)
        return x.reshape(bsz, g, seq // CMP_STRIDE, CMP_STRIDE * d).astype(BF16)

    kcmp = _compress(chunks(rope_k(k_c), dk), k_pe, k_w1, k_b1, k_w2)
    vcmp_t = jnp.swapaxes(_compress(chunks(v_c.reshape(t, g, dv), dv), v_pe, v_w1, v_b1, v_w2), -1, -2)
    ks = per_group(rope_k(k_s), dk)
    pad = ((0, 0), (0, 0), (WINDOW, 0), (0, 0))
    kw = jnp.pad(per_group(rope_k(k_w), dk), pad)
    gates = jax.nn.sigmoid(g_logit).reshape(bsz, seq, 3, g, jh).transpose(0, 3, 2, 4, 1).reshape(bsz, g, 3 * jh, seq)
    tabs = _rope_lane_tables(pos, NSA_ROPE).reshape(bsz, seq, 3 * LANES)

    assert int(offs[3]) % dv == 0 and int(offs[5]) % dv == 0
    o = _nsa_attention(q.reshape(bsz, seq, nq), tabs, gates, kcmp, vcmp_t, ks, kw, rest.reshape(bsz, seq, -1),
                       int(offs[3]) // dv, int(offs[5]) // dv, kc=kc)
    return _mm(o.reshape(t, NSA_HEADS * dv), w_out.astype(BF16), res=h2, tm=1024)


def _ffn_layer(h2, seq, ffn_norm, w_up, conv_w, conv_b, w_down, out_gain=None):
    return _ffn(h2, seq, ffn_norm, w_up.astype(BF16), conv_w, conv_b, w_down.astype(BF16), out_gain=out_gain)


def kernel(x, positions, l0_attn_norm, l0_dsa_w_in, l0_dsa_q_norm, l0_dsa_kv_norm, l0_dsa_idx_ln_g, l0_dsa_idx_ln_b, l0_dsa_w_qup, l0_dsa_w_uk, l0_dsa_w_uv, l0_dsa_w_out, l0_ffn_norm, l0_ffn_up, l0_ffn_conv_w, l0_ffn_conv_b, l0_ffn_down, l1_attn_norm, l1_fox_w_in, l1_fox_b_f, l1_fox_w_out, l1_ffn_norm, l1_ffn_up, l1_ffn_conv_w, l1_ffn_conv_b, l1_ffn_down, l2_attn_norm, l2_nsa_w_in, l2_nsa_k_pe, l2_nsa_k_w1, l2_nsa_k_b1, l2_nsa_k_w2, l2_nsa_v_pe, l2_nsa_v_w1, l2_nsa_v_b1, l2_nsa_v_w2, l2_nsa_w_out, l2_ffn_norm, l2_ffn_up, l2_ffn_conv_w, l2_ffn_conv_b, l2_ffn_down, l3_attn_norm, l3_dsa_w_in, l3_dsa_q_norm, l3_dsa_kv_norm, l3_dsa_idx_ln_g, l3_dsa_idx_ln_b, l3_dsa_w_qup, l3_dsa_w_uk, l3_dsa_w_uv, l3_dsa_w_out, l3_ffn_norm, l3_ffn_up, l3_ffn_conv_w, l3_ffn_conv_b, l3_ffn_down, final_norm):
    bsz, seq, d = x.shape
    h = x.reshape(bsz * seq, d)
    h = _dsa_mixer(h, bsz, seq, positions, l0_attn_norm, l0_dsa_w_in, l0_dsa_q_norm, l0_dsa_kv_norm,
                   l0_dsa_idx_ln_g, l0_dsa_idx_ln_b, l0_dsa_w_qup, l0_dsa_w_uk, l0_dsa_w_uv, l0_dsa_w_out)
    h = _ffn_layer(h, seq, l0_ffn_norm, l0_ffn_up, l0_ffn_conv_w, l0_ffn_conv_b, l0_ffn_down)
    h = _fox_mixer(h, bsz, seq, l1_attn_norm, l1_fox_w_in, l1_fox_b_f, l1_fox_w_out)
    h = _ffn_layer(h, seq, l1_ffn_norm, l1_ffn_up, l1_ffn_conv_w, l1_ffn_conv_b, l1_ffn_down)
    h = _nsa_mixer(h, bsz, seq, positions, l2_attn_norm, l2_nsa_w_in, l2_nsa_k_pe, l2_nsa_k_w1, l2_nsa_k_b1,
                   l2_nsa_k_w2, l2_nsa_v_pe, l2_nsa_v_w1, l2_nsa_v_b1, l2_nsa_v_w2, l2_nsa_w_out)
    h = _ffn_layer(h, seq, l2_ffn_norm, l2_ffn_up, l2_ffn_conv_w, l2_ffn_conv_b, l2_ffn_down)
    h = _dsa_mixer(h, bsz, seq, positions, l3_attn_norm, l3_dsa_w_in, l3_dsa_q_norm, l3_dsa_kv_norm,
                   l3_dsa_idx_ln_g, l3_dsa_idx_ln_b, l3_dsa_w_qup, l3_dsa_w_uk, l3_dsa_w_uv, l3_dsa_w_out)
    h = _ffn_layer(h, seq, l3_ffn_norm, l3_ffn_up, l3_ffn_conv_w, l3_ffn_conv_b, l3_ffn_down, out_gain=final_norm)
    return h.reshape(bsz, seq, d)
```
